```python
import math
import jax, jax.numpy as jnp
from jax import lax
import numpy as np

D_MODEL = 2048
BATCH = 2
SEQ = 8192
DEPTH = 2
DEC_BATCH = 4
DEC_SEQ = 4096
PAST_LEN = 128

GRID_W = 64
HEAD_DIM = 128
NA_HEADS = 8
NA_WIDTH = NA_HEADS * HEAD_DIM
NA_KH_MAX = 8
NA_KW = 16
FN_GROUPS = 4
FN_GROUP_DIM = 128
FN_WIDTH = FN_GROUPS * FN_GROUP_DIM
HY_GROUPS = 4
HY_WIDTH = 512
HY_EMB_DIM = 33
HY_FILTER_HIDDEN = 64
HY_FAST_DECAY = 0.3
HY_SLOW_DECAY = 1.5
HY_TARGET = 1e-2
SHORT_CONV = 3
IN_PROJ = 3 * NA_WIDTH + FN_WIDTH + 3 * HY_WIDTH
MIX_WIDTH = NA_WIDTH + FN_WIDTH + HY_WIDTH
D_FF = 5632
N_MOD = 6
EPS = 1e-6

kernel_name = "hybrid_natten_fnet_hyena_encoder"


def _rmsnorm(x, g):
    xf = x.astype(jnp.float32)
    y = xf * lax.rsqrt(jnp.mean(xf * xf, axis=-1, keepdims=True) + EPS) * g.astype(jnp.float32)
    return y.astype(x.dtype)


def _dwconv3(x, w, b):
    xp = jnp.pad(x, ((0, 0), (1, 1), (0, 0)))
    return xp[:, :-2] * w[0] + xp[:, 1:-1] * w[1] + xp[:, 2:] * w[2] + b


def _neighbourhood_attention(q, k, v, rpb):
    bsz, seq_len = q.shape[0], q.shape[1]
    rows = seq_len // GRID_W
    kh = min(NA_KH_MAX, rows)
    grid = (bsz, rows, GRID_W, NA_HEADS, HEAD_DIM)
    qg, kg, vg = q.reshape(grid), k.reshape(grid), v.reshape(grid)
    r = jnp.arange(rows)
    row_start = jnp.clip(r - kh // 2, 0, rows - kh)
    row_idx = row_start[:, None] + jnp.arange(kh)[None, :]
    k_rows = kg[:, row_idx]
    v_rows = vg[:, row_idx]
    s = jnp.einsum('brqhd,briwhd->bhrqiw', qg, k_rows).astype(jnp.float32)
    c = jnp.arange(GRID_W)
    col_start = jnp.clip(c - NA_KW // 2, 0, GRID_W - NA_KW)
    col_ok = (c[None, :] >= col_start[:, None]) & (c[None, :] < col_start[:, None] + NA_KW)
    dr = row_idx - r[:, None] + (NA_KH_MAX - 1)
    dc = jnp.clip(c[None, :] - c[:, None], -(NA_KW - 1), NA_KW - 1) + (NA_KW - 1)
    bias = rpb[:, dr[:, None, :, None], dc[None, :, None, :]]
    s = s * (HEAD_DIM ** -0.5) + bias[None].astype(jnp.float32)
    s = jnp.where(col_ok[:, None, :], s, -jnp.inf)
    p = jax.nn.softmax(s.reshape(bsz, NA_HEADS, rows, GRID_W, kh * GRID_W), axis=-1).reshape(s.shape)
    o = jnp.einsum('bhrqiw,briwhd->brqhd', p.astype(v.dtype), v_rows)
    return o.reshape(bsz, seq_len, NA_WIDTH)


def _fourier_mix(u, fn_w, fn_b):
    bsz, seq_len = u.shape[0], u.shape[1]
    ug = u.astype(jnp.float32).reshape(bsz, seq_len, FN_GROUPS, FN_GROUP_DIM)
    f = jnp.real(jnp.fft.fft2(ug, axes=(1, 3), norm='ortho'))
    y = jnp.einsum('blgc,gcd->blgd', f, fn_w.astype(jnp.float32))
    return y.reshape(bsz, seq_len, FN_WIDTH) + fn_b.astype(jnp.float32)


def _hyena_filter(seq_len, w1, b1, f1, w2, b2, f2, w3, b3):
    f32 = jnp.float32
    t = jnp.linspace(0.0, 1.0, seq_len, dtype=f32)[:, None]
    bands = (HY_EMB_DIM - 1) // 2
    w = 2.0 * math.pi * jnp.arange(seq_len, dtype=f32)[:, None] / seq_len
    fr = jnp.linspace(1e-4, bands - 1, bands, dtype=f32)[None, :]
    z = jnp.concatenate([t, jnp.cos(fr * w), -jnp.sin(fr * w)], axis=-1)
    h = jnp.sin(f1.astype(f32) * (z @ w1.astype(f32) + b1.astype(f32)))
    h = jnp.sin(f2.astype(f32) * (h @ w2.astype(f32) + b2.astype(f32)))
    h = h @ w3.astype(f32) + b3.astype(f32)
    max_decay = math.log(HY_TARGET) / HY_FAST_DECAY
    min_decay = math.log(HY_TARGET) / HY_SLOW_DECAY
    deltas = jnp.abs(jnp.linspace(min_decay, max_decay, HY_WIDTH, dtype=f32))
    h = h * jnp.exp(-t * jnp.tile(deltas, 2)[None, :])
    h_fwd, h_bwd = h[:, :HY_WIDTH], h[:, HY_WIDTH:]
    return jnp.concatenate([h_fwd, jnp.zeros((1, HY_WIDTH), f32), h_bwd[:0:-1]], axis=0)


def _hyena(u, conv_w, conv_b, w1, b1, f1, w2, b2, f2, w3, b3, d_skip):
    seq_len = u.shape[1]
    u = _dwconv3(u, conv_w, conv_b)
    x0, x1, v = jnp.split(u.astype(jnp.float32), 3, axis=-1)
    z = v * x1
    k_full = _hyena_filter(seq_len, w1, b1, f1, w2, b2, f2, w3, b3)
    zf = jnp.fft.rfft(z, n=2 * seq_len, axis=1)
    kf = jnp.fft.rfft(k_full, axis=0)
    y = jnp.fft.irfft(zf * kf[None], n=2 * seq_len, axis=1)[:, :seq_len]
    y = y + z * d_skip.astype(jnp.float32)
    return y * x0


def _layer(x, c, ada_w, ada_b, norm1_g, w_in, q_norm_g, k_norm_g, na_rpb, fn_w, fn_b,
           hy_conv_w, hy_conv_b, hy_w1, hy_b1, hy_f1, hy_w2, hy_b2, hy_f2, hy_w3, hy_b3, hy_d,
           out_norm_g, w_out, norm2_g, mlp_w_up, mlp_conv_w, mlp_conv_b, mlp_w_down):
    dt = x.dtype
    bsz, seq_len = x.shape[0], x.shape[1]
    mod = jax.nn.silu(c) @ ada_w + ada_b
    sh1, sc1, g1, sh2, sc2, g2 = [m[:, None, :] for m in jnp.split(mod, N_MOD, axis=-1)]
    h = _rmsnorm(x, norm1_g) * (1.0 + sc1) + sh1
    proj = h @ w_in
    q, k, v, u_fn, u_hy = jnp.split(
        proj, [NA_WIDTH, 2 * NA_WIDTH, 3 * NA_WIDTH, 3 * NA_WIDTH + FN_WIDTH], axis=-1)
    hs = (bsz, seq_len, NA_HEADS, HEAD_DIM)
    q = _rmsnorm(q.reshape(hs), q_norm_g)
    k = _rmsnorm(k.reshape(hs), k_norm_g)
    y_a = _neighbourhood_attention(q, k, v.reshape(hs), na_rpb)
    y_b = _fourier_mix(u_fn, fn_w, fn_b).astype(dt)
    y_c = _hyena(u_hy, hy_conv_w, hy_conv_b, hy_w1, hy_b1, hy_f1, hy_w2, hy_b2, hy_f2,
                 hy_w3, hy_b3, hy_d).astype(dt)
    ga, gb, gc = jnp.split(out_norm_g, [NA_WIDTH, NA_WIDTH + FN_WIDTH])
    y = jnp.concatenate([_rmsnorm(y_a, ga), _rmsnorm(y_b, gb), _rmsnorm(y_c, gc)], axis=-1) @ w_out
    x = (x + g1 * y).astype(dt)
    h = _rmsnorm(x, norm2_g) * (1.0 + sc2) + sh2
    a, gate = jnp.split(h @ mlp_w_up, 2, axis=-1)
    a = _dwconv3(a, mlp_conv_w, mlp_conv_b)
    m = (jax.nn.gelu(a, approximate=False) * gate) @ mlp_w_down
    return (x + g2 * m).astype(dt)


def _trunk(x, c, params):
    (ada_w, ada_b, norm1_g, w_in, q_norm_g, k_norm_g, na_rpb, fn_w, fn_b,
     hy_conv_w, hy_conv_b, hy_w1, hy_b1, hy_f1, hy_w2, hy_b2, hy_f2, hy_w3, hy_b3, hy_d,
     out_norm_g, w_out, norm2_g, mlp_w_up, mlp_conv_w, mlp_conv_b, mlp_w_down) = params
    for l in range(DEPTH):
        x = _layer(x, c, ada_w[l], ada_b[l], norm1_g[l], w_in[l], q_norm_g[l], k_norm_g[l], na_rpb[l],
                   fn_w[l], fn_b[l], hy_conv_w[l], hy_conv_b[l], hy_w1[l], hy_b1[l], hy_f1[l],
                   hy_w2[l], hy_b2[l], hy_f2[l], hy_w3[l], hy_b3[l], hy_d[l], out_norm_g[l], w_out[l],
                   norm2_g[l], mlp_w_up[l], mlp_conv_w[l], mlp_conv_b[l], mlp_w_down[l])
    return x


def setup_inputs(seed: int = 0) -> dict:
    key = jax.random.key(seed)
    ks = iter(jax.random.split(key, 40))
    f32 = jnp.float32

    def nrm(shape, scale):
        return jax.random.normal(next(ks), shape, f32) * scale

    def gain(shape):
        return 1.0 + nrm(shape, 0.05)

    return {
        "x_prompt": nrm((BATCH, SEQ, D_MODEL), 1.0),
        "x_sample": nrm((DEC_BATCH, DEC_SEQ, D_MODEL), 1.0),
        "c_prompt": nrm((BATCH, D_MODEL), 1.0),
        "c_sample": nrm((DEC_BATCH, D_MODEL), 1.0),
        "ada_w": nrm((DEPTH, D_MODEL, N_MOD * D_MODEL), 0.5 * D_MODEL ** -0.5),
        "ada_b": nrm((DEPTH, N_MOD * D_MODEL), 0.02),
        "norm1_g": gain((DEPTH, D_MODEL)),
        "w_in": nrm((DEPTH, D_MODEL, IN_PROJ), D_MODEL ** -0.5),
        "q_norm_g": gain((DEPTH, HEAD_DIM)),
        "k_norm_g": gain((DEPTH, HEAD_DIM)),
        "na_rpb": nrm((DEPTH, NA_HEADS, 2 * NA_KH_MAX - 1, 2 * NA_KW - 1), 0.1),
        "fn_w": nrm((DEPTH, FN_GROUPS, FN_GROUP_DIM, FN_GROUP_DIM), FN_GROUP_DIM ** -0.5),
        "fn_b": nrm((DEPTH, FN_WIDTH), 0.02),
        "hy_conv_w": nrm((DEPTH, SHORT_CONV, 3 * HY_WIDTH), 0.5),
        "hy_conv_b": nrm((DEPTH, 3 * HY_WIDTH), 0.02),
        "hy_w1": nrm((DEPTH, HY_EMB_DIM, HY_FILTER_HIDDEN), HY_EMB_DIM ** -0.5),
        "hy_b1": nrm((DEPTH, HY_FILTER_HIDDEN), 0.1),
        "hy_f1": gain((DEPTH, HY_FILTER_HIDDEN)),
        "hy_w2": nrm((DEPTH, HY_FILTER_HIDDEN, HY_FILTER_HIDDEN), HY_FILTER_HIDDEN ** -0.5),
        "hy_b2": nrm((DEPTH, HY_FILTER_HIDDEN), 0.1),
        "hy_f2": gain((DEPTH, HY_FILTER_HIDDEN)),
        "hy_w3": nrm((DEPTH, HY_FILTER_HIDDEN, 2 * HY_WIDTH), HY_FILTER_HIDDEN ** -0.5),
        "hy_b3": nrm((DEPTH, 2 * HY_WIDTH), 0.02),
        "hy_d": nrm((DEPTH, HY_WIDTH), 0.5),
        "out_norm_g": gain((DEPTH, MIX_WIDTH)),
        "w_out": nrm((DEPTH, MIX_WIDTH, D_MODEL), MIX_WIDTH ** -0.5),
        "norm2_g": gain((DEPTH, D_MODEL)),
        "mlp_w_up": nrm((DEPTH, D_MODEL, 2 * D_FF), D_MODEL ** -0.5),
        "mlp_conv_w": nrm((DEPTH, SHORT_CONV, D_FF), 0.5),
        "mlp_conv_b": nrm((DEPTH, D_FF), 0.02),
        "mlp_w_down": nrm((DEPTH, D_FF, D_MODEL), D_FF ** -0.5),
    }


def reference(x_prompt, x_sample, c_prompt, c_sample, ada_w, ada_b, norm1_g, w_in, q_norm_g, k_norm_g,
              na_rpb, fn_w, fn_b, hy_conv_w, hy_conv_b, hy_w1, hy_b1, hy_f1, hy_w2, hy_b2, hy_f2,
              hy_w3, hy_b3, hy_d, out_norm_g, w_out, norm2_g, mlp_w_up, mlp_conv_w, mlp_conv_b,
              mlp_w_down):
    params = (ada_w, ada_b, norm1_g, w_in, q_norm_g, k_norm_g, na_rpb, fn_w, fn_b,
              hy_conv_w, hy_conv_b, hy_w1, hy_b1, hy_f1, hy_w2, hy_b2, hy_f2, hy_w3, hy_b3, hy_d,
              out_norm_g, w_out, norm2_g, mlp_w_up, mlp_conv_w, mlp_conv_b, mlp_w_down)
    y_prompt = _trunk(x_prompt, c_prompt, params)
    y_sample = _trunk(x_sample, c_sample, params)
    return (y_prompt, y_sample)
```

```python
import functools
import math

import ml_dtypes
import numpy as np
import jax
import jax.numpy as jnp
from jax import lax
from jax.experimental import pallas as pl
from jax.experimental.pallas import tpu as pltpu

F32 = jnp.float32
BF16 = jnp.bfloat16

D_MODEL = 2048
GRID_W = 64
HEAD_DIM = 128
NA_HEADS = 8
NA_WIDTH = NA_HEADS * HEAD_DIM
NA_KH = 8
NA_KW = 16
FN_GROUPS = 4
FN_GROUP_DIM = 128
FN_WIDTH = 512
HY_WIDTH = 512
HY_EMB_DIM = 33
HY_HIDDEN = 64
HY_FAST_DECAY = 0.3
HY_SLOW_DECAY = 1.5
HY_TARGET = 1e-2
IN_PROJ = 3 * NA_WIDTH + FN_WIDTH + 3 * HY_WIDTH
D_FF = 5632
N_MOD = 6
EPS = 1e-6
NEG = -1e30

LANES = 128
DFT_N2 = 128
VMEM_LIMIT = 56 << 20

HIGHEST = lax.Precision.HIGHEST


def _cparams(sem):
    return pltpu.CompilerParams(dimension_semantics=sem, vmem_limit_bytes=VMEM_LIMIT)


def _bf16_split_np(m):
    hi = m.astype(ml_dtypes.bfloat16)
    lo = (m - hi.astype(np.float64)).astype(ml_dtypes.bfloat16)
    return hi, lo


def _cat3_cols_np(m):
    hi, lo = _bf16_split_np(m)
    return np.concatenate([hi, hi, lo], axis=1)


def _cos_sin(n_out, n_in, period):
    ang = 2.0 * np.pi * np.outer(np.arange(n_out), np.arange(n_in)) / period
    return np.cos(ang), np.sin(ang)


def _cat3_rows(x):
    xh = x.astype(BF16)
    xl = (x - xh.astype(F32)).astype(BF16)
    return jnp.concatenate([xh, xl, xh], axis=0)


def _cat3_cols(x):
    xh = x.astype(BF16)
    xl = (x - xh.astype(F32)).astype(BF16)
    return jnp.concatenate([xh, xl, xh], axis=1)


def _ada_kernel(c_ref, w_ref, b_ref, o_ref):
    c = c_ref[...]
    s = c * (1.0 / (1.0 + jnp.exp(-c)))
    sh = s.astype(BF16)
    sl = (s - sh.astype(F32)).astype(BF16)
    w = w_ref[...]
    wh = w.astype(BF16)
    wl = (w - wh.astype(F32)).astype(BF16)
    nrow = c.shape[0]
    both = jnp.dot(jnp.concatenate([sh, sl], axis=0), wh, preferred_element_type=F32)
    o_ref[...] = both[:nrow] + both[nrow:] + jnp.dot(sh, wl, preferred_element_type=F32) + b_ref[...]


def _ada(c_all, ada_w, ada_b):
    depth = ada_w.shape[0]
    nrow = c_all.shape[0]
    tn = 1024
    ncol = N_MOD * D_MODEL
    return pl.pallas_call(
        _ada_kernel,
        grid=(depth, ncol // tn),
        in_specs=[
            pl.BlockSpec((nrow, D_MODEL), lambda l, j: (0, 0)),
            pl.BlockSpec((None, D_MODEL, tn), lambda l, j: (l, 0, j)),
            pl.BlockSpec((None, 1, tn), lambda l, j: (l, 0, j)),
        ],
        out_specs=pl.BlockSpec((None, nrow, tn), lambda l, j: (l, 0, j)),
        out_shape=jax.ShapeDtypeStruct((depth, nrow, ncol), F32),
        compiler_params=_cparams(("arbitrary", "arbitrary")),
        name="ada",
    )(c_all, ada_w, ada_b)


KIN_TN = 512
KIN_QKV_TILES = 3 * NA_WIDTH // KIN_TN
KIN_FN_TILE = KIN_QKV_TILES
KIN_HY_TILES = 3 * HY_WIDTH // KIN_TN


def _kin_kernel(x_ref, mod_ref, g_ref, w_ref, oq_ref, ofn_ref, ohy_ref, h_scr):
    j = pl.program_id(1)

    @pl.when(j == 0)
    def _():
        x = x_ref[...]
        ms = jnp.mean(x * x, axis=-1, keepdims=True)
        hn = x * lax.rsqrt(ms + EPS) * g_ref[...]
        h_scr[...] = (hn * (1.0 + mod_ref[1:2, :]) + mod_ref[0:1, :]).astype(BF16)

    acc = jnp.dot(h_scr[...], w_ref[...], preferred_element_type=F32)

    @pl.when(j < KIN_QKV_TILES)
    def _():
        oq_ref[...] = acc

    @pl.when(j == KIN_FN_TILE)
    def _():
        ofn_ref[...] = acc

    @pl.when(j > KIN_FN_TILE)
    def _():
        ohy_ref[...] = acc


def _kin(x, mod, g, w, seq_len):
    t = x.shape[0]
    tm = 1024
    nj = IN_PROJ // KIN_TN
    return pl.pallas_call(
        _kin_kernel,
        grid=(t // tm, nj),
        in_specs=[
            pl.BlockSpec((tm, D_MODEL), lambda i, j: (i, 0)),
            pl.BlockSpec((None, N_MOD, D_MODEL), lambda i, j: ((i * tm) // seq_len, 0, 0)),
            pl.BlockSpec((1, D_MODEL), lambda i, j: (0, 0)),
            pl.BlockSpec((D_MODEL, KIN_TN), lambda i, j: (0, j)),
        ],
        out_specs=[
            pl.BlockSpec((tm, KIN_TN), lambda i, j: (i, jnp.minimum(j, KIN_QKV_TILES - 1))),
            pl.BlockSpec((tm, KIN_TN), lambda i, j: (i, 0)),
            pl.BlockSpec((tm, KIN_TN), lambda i, j: (i, jnp.clip(j - KIN_FN_TILE - 1, 0, KIN_HY_TILES - 1))),
        ],
        out_shape=[
            jax.ShapeDtypeStruct((t, 3 * NA_WIDTH), F32),
            jax.ShapeDtypeStruct((t, FN_WIDTH), F32),
            jax.ShapeDtypeStruct((t, 3 * HY_WIDTH), F32),
        ],
        scratch_shapes=[pltpu.VMEM((tm, D_MODEL), BF16)],
        compiler_params=_cparams(("arbitrary", "arbitrary")),
        name="kin",
    )(x, mod, g, w)


ATT_R = 4
ATT_KW = ATT_R + NA_KH
ATT_NORM_CHUNK = 512


def _attn_variant_tiles(rows, variant):
    if variant == 0:
        r0, ws = 0, 0
    elif variant == 1:
        r0 = ATT_R
        ws = r0 - NA_KH // 2
    else:
        r0, ws = rows - ATT_R, rows - ATT_KW
    tiles = []
    for a in range(ATT_R):
        r = r0 + a
        rs = min(max(r - NA_KH // 2, 0), rows - NA_KH)
        row = []
        for jp in range(ATT_KW // 2):
            kr = ws + 2 * jp
            valid_l = rs <= kr < rs + NA_KH
            valid_r = rs <= kr + 1 < rs + NA_KH
            d_left = kr - r + (NA_KH - 1)
            row.append((valid_l, valid_r, d_left))
        tiles.append(row)
    return tiles


def _attn_kernel(rpb_ref, q_ref, k_ref, v_ref, qg_ref, kg_ref, o_ref, qs, ks, vs, t2, bias, *, seq_len):
    rows = seq_len // GRID_W
    nb = rows // ATT_R
    h = pl.program_id(1)
    n_rd = 2 * NA_KH - 1
    n_cd = 2 * NA_KW - 1

    def norm_chunk(i, carry):
        sl = pl.ds(pl.multiple_of(i * ATT_NORM_CHUNK, ATT_NORM_CHUNK), ATT_NORM_CHUNK)
        q = q_ref[sl, :]
        qn = q * lax.rsqrt(jnp.mean(q * q, axis=-1, keepdims=True) + EPS) * qg_ref[...]
        qs[sl, :] = (qn * (HEAD_DIM ** -0.5)).astype(BF16)
        k = k_ref[sl, :]
        kn = k * lax.rsqrt(jnp.mean(k * k, axis=-1, keepdims=True) + EPS) * kg_ref[...]
        ks[sl, :] = kn.astype(BF16)
        vs[sl, :] = v_ref[sl, :].astype(BF16)
        return carry

    lax.fori_loop(0, seq_len // ATT_NORM_CHUNK, norm_chunk, 0)

    cidx = lax.broadcasted_iota(jnp.int32, (GRID_W, LANES), 0)
    lane = lax.broadcasted_iota(jnp.int32, (GRID_W, LANES), 1)
    widx = lane & (GRID_W - 1)
    left = lane < GRID_W
    dcidx = jnp.clip(widx - cidx, -(NA_KW - 1), NA_KW - 1) + (NA_KW - 1)
    base = h * (n_rd * n_cd)
    for dp in range(n_rd + 1):
        acc = jnp.zeros((GRID_W, LANES), F32)
        for e in range(n_cd):
            vl = rpb_ref[base + (dp - 1) * n_cd + e] if dp - 1 >= 0 else 0.0
            vr = rpb_ref[base + dp * n_cd + e] if dp < n_rd else 0.0
            acc = jnp.where(dcidx == e, jnp.where(left, vl, vr), acc)
        t2[dp] = acc

    c_lo = jnp.clip(cidx - NA_KW // 2, 0, GRID_W - NA_KW)
    col_ok = (widx >= c_lo) & (widx < c_lo + NA_KW)
    for variant in range(3):
        tiles = _attn_variant_tiles(rows, variant)
        for a in range(ATT_R):
            for jp in range(ATT_KW // 2):
                valid_l, valid_r, d_left = tiles[a][jp]
                if not (valid_l or valid_r):
                    tile = jnp.full((GRID_W, LANES), NEG, F32)
                else:
                    ok = col_ok
                    if not valid_l:
                        ok = ok & (~left)
                    if not valid_r:
                        ok = ok & left
                    tile = jnp.where(ok, t2[d_left + 1], NEG)
                bias[variant, a * GRID_W:(a + 1) * GRID_W, jp * LANES:(jp + 1) * LANES] = tile

    def block(rb, carry):
        variant = jnp.where(rb == 0, 0, jnp.where(rb == nb - 1, 2, 1))
        ws = jnp.clip(rb * ATT_R - NA_KH // 2, 0, rows - ATT_KW)
        qsl = pl.ds(pl.multiple_of(rb * (ATT_R * GRID_W), ATT_R * GRID_W), ATT_R * GRID_W)
        ksl = pl.ds(pl.multiple_of(ws * GRID_W, GRID_W), ATT_KW * GRID_W)
        s = lax.dot_general(qs[qsl, :], ks[ksl, :], (((1,), (1,)), ((), ())), preferred_element_type=F32)
        s = s + bias[variant]
        m = jnp.max(s, axis=-1, keepdims=True)
        p = jnp.exp(s - m)
        den = jnp.sum(p, axis=-1, keepdims=True)
        o = jnp.dot(p.astype(BF16), vs[ksl, :], preferred_element_type=F32)
        o_ref[qsl, :] = o * (1.0 / den)
        return carry

    lax.fori_loop(0, nb, block, 0)


def _attn(qkv, rpb_flat, qg, kg, bsz, seq_len):
    t = qkv.shape[0]
    nlb = 1
    kernel = functools.partial(_attn_kernel, seq_len=seq_len)
    return pl.pallas_call(
        kernel,
        grid=(bsz, NA_HEADS),
        in_specs=[
            pl.BlockSpec(memory_space=pltpu.SMEM),
            pl.BlockSpec((seq_len, HEAD_DIM), lambda b, h: (b, h)),
            pl.BlockSpec((seq_len, HEAD_DIM), lambda b, h: (b, NA_HEADS + h)),
            pl.BlockSpec((seq_len, HEAD_DIM), lambda b, h: (b, 2 * NA_HEADS + h)),
            pl.BlockSpec((1, HEAD_DIM), lambda b, h: (0, 0)),
            pl.BlockSpec((1, HEAD_DIM), lambda b, h: (0, 0)),
        ],
        out_specs=pl.BlockSpec((seq_len, HEAD_DIM), lambda b, h: (b, h)),
        out_shape=jax.ShapeDtypeStruct((t, NA_WIDTH), F32),
        scratch_shapes=[
            pltpu.VMEM((seq_len, HEAD_DIM), BF16),
            pltpu.VMEM((seq_len, HEAD_DIM), BF16),
            pltpu.VMEM((seq_len, HEAD_DIM), BF16),
            pltpu.VMEM((2 * NA_KH, GRID_W, LANES), F32),
            pltpu.VMEM((3, ATT_R * GRID_W, ATT_KW * GRID_W), F32),
        ],
        compiler_params=_cparams(("arbitrary", "arbitrary")),
        name="attn",
    )(rpb_flat, qkv, qkv, qkv, qg, kg)


LMM_TW = 4096


def _lmm3_kernel(m_ref, x_ref, o_ref):
    o_ref[...] = jnp.dot(m_ref[...], _cat3_rows(x_ref[...]), preferred_element_type=F32)


def _lmm3(mcat, x):
    bsz, kd, w = x.shape
    mo = mcat.shape[0]
    return pl.pallas_call(
        _lmm3_kernel,
        grid=(bsz, w // LMM_TW),
        in_specs=[
            pl.BlockSpec((mo, 3 * kd), lambda b, j: (0, 0)),
            pl.BlockSpec((None, kd, LMM_TW), lambda b, j: (b, 0, j)),
        ],
        out_specs=pl.BlockSpec((None, mo, LMM_TW), lambda b, j: (b, 0, j)),
        out_shape=jax.ShapeDtypeStruct((bsz, mo, w), F32),
        compiler_params=_cparams(("arbitrary", "arbitrary")),
        name="lmm3",
    )(mcat, x)


def _stage1_fwd_mat(n1, kd, scale=1.0):
    c, s = _cos_sin(n1, kd, n1)
    return jnp.asarray(_cat3_cols_np(np.concatenate([c, -s], axis=0) * scale))


def _stage2_mats(n2):
    c, s = _cos_sin(n2, n2, n2)
    fwd = np.block([[c, s], [-s, c]])
    inv = np.block([[c, -s], [s, c]])
    return jnp.asarray(_cat3_cols_np(fwd)), jnp.asarray(_cat3_cols_np(inv))


def _twiddle(n1, n2):
    ang = 2.0 * np.pi * np.outer(np.arange(n1), np.arange(n2)) / (n1 * n2)
    twr = jnp.asarray(np.cos(ang).astype(np.float32))
    twi = jnp.asarray((-np.sin(ang)).astype(np.float32))
    shape = (n1, n2, LANES)
    return jnp.broadcast_to(twr[:, :, None], shape), jnp.broadcast_to(twi[:, :, None], shape)


def _lane_tile(x, width):
    reps = width // x.shape[-1]
    return x if reps == 1 else jnp.concatenate([x] * reps, axis=-1)


FFT_KC = 4


def _fnw_kernel(cs_ref, w_ref, o_ref):
    o_ref[...] = jnp.dot(cs_ref[...], w_ref[...], precision=HIGHEST, preferred_element_type=F32)


def _fnw(fn_w):
    c, s = _cos_sin(FN_GROUP_DIM, FN_GROUP_DIM, FN_GROUP_DIM)
    cs = jnp.asarray(np.concatenate([c, s], axis=0).astype(np.float32))
    return pl.pallas_call(
        _fnw_kernel,
        grid=(FN_GROUPS,),
        in_specs=[
            pl.BlockSpec((2 * FN_GROUP_DIM, FN_GROUP_DIM), lambda g: (0, 0)),
            pl.BlockSpec((None, FN_GROUP_DIM, FN_GROUP_DIM), lambda g: (g, 0, 0)),
        ],
        out_specs=pl.BlockSpec((None, 2 * FN_GROUP_DIM, FN_GROUP_DIM), lambda g: (g, 0, 0)),
        out_shape=jax.ShapeDtypeStruct((FN_GROUPS, 2 * FN_GROUP_DIM, FN_GROUP_DIM), F32),
        compiler_params=_cparams(("arbitrary",)),
        name="fnw",
    )(cs, fn_w)


def _fn2_kernel(y_ref, twr_ref, twi_ref, f2_ref, g_ref, b_ref, o_ref):
    n2 = DFT_N2
    for t in range(FFT_KC):
        yr = y_ref[0, t]
        yi = y_ref[1, t]
        twr = _lane_tile(twr_ref[t], FN_WIDTH)
        twi = _lane_tile(twi_ref[t], FN_WIDTH)
        ar = yr * twr - yi * twi
        ai = yr * twi + yi * twr
        z = jnp.dot(f2_ref[...], _cat3_rows(jnp.concatenate([ar, ai], axis=0)), preferred_element_type=F32)
        zr = z[:n2]
        zi = z[n2:]
        outs = []
        for g in range(FN_GROUPS):
            sl = slice(g * FN_GROUP_DIM, (g + 1) * FN_GROUP_DIM)
            zc = jnp.concatenate([zr[:, sl], zi[:, sl]], axis=1)
            gm = g_ref[g]
            gh = gm.astype(BF16)
            gl = (gm - gh.astype(F32)).astype(BF16)
            outs.append(jnp.dot(_cat3_cols(zc), jnp.concatenate([gh, gh, gl], axis=0), preferred_element_type=F32))
        o_ref[:, t * FN_WIDTH:(t + 1) * FN_WIDTH] = jnp.concatenate(outs, axis=1) + b_ref[...]


def _fourier(u, gcat, fn_b, bsz, seq_len):
    n2 = DFT_N2
    n1 = seq_len // n2
    m1 = _stage1_fwd_mat(n1, n1, scale=1.0 / math.sqrt(seq_len * FN_GROUP_DIM))
    y = _lmm3(m1, u.reshape(bsz, n1, n2 * FN_WIDTH))
    y = y.reshape(bsz, 2, n1, n2, FN_WIDTH)
    twr, twi = _twiddle(n1, n2)
    f2, _ = _stage2_mats(n2)
    out = pl.pallas_call(
        _fn2_kernel,
        grid=(n1 // FFT_KC, bsz),
        in_specs=[
            pl.BlockSpec((None, 2, FFT_KC, n2, FN_WIDTH), lambda k, b: (b, 0, k, 0, 0)),
            pl.BlockSpec((FFT_KC, n2, LANES), lambda k, b: (k, 0, 0)),
            pl.BlockSpec((FFT_KC, n2, LANES), lambda k, b: (k, 0, 0)),
            pl.BlockSpec((2 * n2, 6 * n2), lambda k, b: (0, 0)),
            pl.BlockSpec((FN_GROUPS, 2 * FN_GROUP_DIM, FN_GROUP_DIM), lambda k, b: (0, 0, 0)),
            pl.BlockSpec((1, FN_WIDTH), lambda k, b: (0, 0)),
        ],
        out_specs=pl.BlockSpec((None, n2, FFT_KC * FN_WIDTH), lambda k, b: (b, 0, k)),
        out_shape=jax.ShapeDtypeStruct((bsz, n2, n1 * FN_WIDTH), F32),
        compiler_params=_cparams(("arbitrary", "arbitrary")),
        name="fn2",
    )(y, twr, twi, f2, gcat, fn_b)
    return out.reshape(bsz * seq_len, FN_WIDTH)


HPRE_TM = 512


def _hpre_kernel(up_ref, um_ref, un_ref, cw_ref, cb_ref, z_ref, x0_ref, *, seq_len):
    tm = HPRE_TM
    i = pl.program_id(0)
    at_start = (i * tm) % seq_len == 0
    at_end = ((i + 1) * tm) % seq_len == 0
    u = um_ref[...]
    r = lax.broadcasted_iota(jnp.int32, (tm, 1), 0)
    prev_row = jnp.where(at_start, 0.0, up_ref[7:8, :])
    next_row = jnp.where(at_end, 0.0, un_ref[0:1, :])
    upv = jnp.where(r == 0, prev_row, pltpu.roll(u, 1, 0))
    dnv = jnp.where(r == tm - 1, next_row, pltpu.roll(u, tm - 1, 0))
    y = upv * cw_ref[0:1, :] + u * cw_ref[1:2, :] + dnv * cw_ref[2:3, :] + cb_ref[...]
    z_ref[...] = y[:, 2 * HY_WIDTH:] * y[:, HY_WIDTH:2 * HY_WIDTH]
    x0_ref[...] = y[:, :HY_WIDTH]


def _hpre(u, cw, cb, seq_len):
    t = u.shape[0]
    tm = HPRE_TM
    w = 3 * HY_WIDTH
    nb8 = t // 8
    return pl.pallas_call(
        functools.partial(_hpre_kernel, seq_len=seq_len),
        grid=(t // tm,),
        in_specs=[
            pl.BlockSpec((8, w), lambda i: (jnp.maximum(i * (tm // 8) - 1, 0), 0)),
            pl.BlockSpec((tm, w), lambda i: (i, 0)),
            pl.BlockSpec((8, w), lambda i: (jnp.minimum((i + 1) * (tm // 8), nb8 - 1), 0)),
            pl.BlockSpec((3, w), lambda i: (0, 0)),
            pl.BlockSpec((1, w), lambda i: (0, 0)),
        ],
        out_specs=[
            pl.BlockSpec((tm, HY_WIDTH), lambda i: (i, 0)),
            pl.BlockSpec((tm, HY_WIDTH), lambda i: (i, 0)),
        ],
        out_shape=[jax.ShapeDtypeStruct((t, HY_WIDTH), F32), jax.ShapeDtypeStruct((t, HY_WIDTH), F32)],
        compiler_params=_cparams(("arbitrary",)),
        name="hpre",
    )(u, u, u, cw, cb)


HFILT_TL = 512
HY_EMB_PAD = 128


def _hfilt_kernel(z_ref, w1_ref, b1_ref, f1_ref, w2_ref, b2_ref, f2_ref, w3_ref, b3_ref, dl_ref, o_ref):
    z = z_ref[...]
    h = jnp.sin(f1_ref[...] * (jnp.dot(z, w1_ref[...], precision=HIGHEST, preferred_element_type=F32) + b1_ref[...]))
    h = jnp.sin(f2_ref[...] * (jnp.dot(h, w2_ref[...], precision=HIGHEST, preferred_element_type=F32) + b2_ref[...]))
    h = jnp.dot(h, w3_ref[...], precision=HIGHEST, preferred_element_type=F32) + b3_ref[...]
    t = z[:, 0:1]
    h = h * jnp.exp(-t * dl_ref[...])
    row = lax.broadcasted_iota(jnp.int32, h.shape, 0) + pl.program_id(0) * HFILT_TL
    col = lax.broadcasted_iota(jnp.int32, h.shape, 1)
    o_ref[...] = jnp.where((row == 0) & (col >= HY_WIDTH), 0.0, h)


def _hyena_emb_np(seq_len):
    t = np.linspace(0.0, 1.0, seq_len)[:, None]
    bands = (HY_EMB_DIM - 1) // 2
    w = 2.0 * np.pi * np.arange(seq_len)[:, None] / seq_len
    fr = np.linspace(1e-4, bands - 1, bands)[None, :]
    z = np.concatenate([t, np.cos(fr * w), -np.sin(fr * w)], axis=-1)
    out = np.zeros((seq_len, HY_EMB_PAD), np.float32)
    out[:, :HY_EMB_DIM] = z
    return out


def _hfilt(seq_len, w1, b1, f1, w2, b2, f2, w3, b3):
    zemb = jnp.asarray(_hyena_emb_np(seq_len))
    max_decay = math.log(HY_TARGET) / HY_FAST_DECAY
    min_decay = math.log(HY_TARGET) / HY_SLOW_DECAY
    deltas = np.abs(np.linspace(min_decay, max_decay, HY_WIDTH))
    dl = jnp.asarray(np.tile(deltas, 2)[None, :].astype(np.float32))
    w1p = jnp.pad(w1, ((0, HY_EMB_PAD - HY_EMB_DIM), (0, 0)))
    tl = HFILT_TL
    full = lambda shape: pl.BlockSpec(shape, lambda i: (0,) * len(shape))
    return pl.pallas_call(
        _hfilt_kernel,
        grid=(seq_len // tl,),
        in_specs=[
            pl.BlockSpec((tl, HY_EMB_PAD), lambda i: (i, 0)),
            full((HY_EMB_PAD, HY_HIDDEN)), full((1, HY_HIDDEN)), full((1, HY_HIDDEN)),
            full((HY_HIDDEN, HY_HIDDEN)), full((1, HY_HIDDEN)), full((1, HY_HIDDEN)),
            full((HY_HIDDEN, 2 * HY_WIDTH)), full((1, 2 * HY_WIDTH)), full((1, 2 * HY_WIDTH)),
        ],
        out_specs=pl.BlockSpec((tl, 2 * HY_WIDTH), lambda i: (i, 0)),
        out_shape=jax.ShapeDtypeStruct((seq_len, 2 * HY_WIDTH), F32),
        compiler_params=_cparams(("arbitrary",)),
        name="hfilt",
    )(zemb, w1p, b1, f1, w2, b2, f2, w3, b3, dl)


def _hk2_kernel(y_ref, twr_ref, twi_ref, f2_ref, o_ref):
    n2 = DFT_N2
    for t in range(FFT_KC):
        yr = y_ref[0, t]
        yi = y_ref[1, t]
        twr = _lane_tile(twr_ref[t], 2 * HY_WIDTH)
        twi = _lane_tile(twi_ref[t], 2 * HY_WIDTH)
        ar = yr * twr - yi * twi
        ai = yr * twi + yi * twr
        z = jnp.dot(f2_ref[...], _cat3_rows(jnp.concatenate([ar, ai], axis=0)), preferred_element_type=F32)
        zr = z[:n2]
        zi = z[n2:]
        o_ref[0, t] = zr[:, :HY_WIDTH] + zr[:, HY_WIDTH:]
        o_ref[1, t] = zi[:, :HY_WIDTH] - zi[:, HY_WIDTH:]


def _hy2_kernel(y_ref, kf_ref, twr_ref, twi_ref, f2_ref, f2i_ref, o_ref):
    n2 = DFT_N2
    for t in range(FFT_KC):
        yr = y_ref[0, t]
        yi = y_ref[1, t]
        twr = _lane_tile(twr_ref[t], HY_WIDTH)
        twi = _lane_tile(twi_ref[t], HY_WIDTH)
        ar = yr * twr - yi * twi
        ai = yr * twi + yi * twr
        z = jnp.dot(f2_ref[...], _cat3_rows(jnp.concatenate([ar, ai], axis=0)), preferred_element_type=F32)
        zr = z[:n2]
        zi = z[n2:]
        kr = kf_ref[0, t]
        ki = kf_ref[1, t]
        pr = zr * kr - zi * ki
        pi = zr * ki + zi * kr
        g = jnp.dot(f2i_ref[...], _cat3_rows(jnp.concatenate([pr, pi], axis=0)), preferred_element_type=F32)
        gr = g[:n2]
        gi = g[n2:]
        o_ref[0, t] = gr * twr + gi * twi
        o_ref[1, t] = gi * twr - gr * twi


def _hy3_kernel(m_ref, g_ref, z_ref, x0_ref, d_ref, o_ref):
    conv = jnp.dot(m_ref[...], _cat3_rows(g_ref[...]), preferred_element_type=F32)
    o_ref[...] = (conv + z_ref[...] * d_ref[...]) * x0_ref[...]


def _hyena_filter_spectrum(seq_len, w1, b1, f1, w2, b2, f2, w3, b3):
    n = 2 * seq_len
    n2 = DFT_N2
    n1 = n // n2
    h = _hfilt(seq_len, w1, b1, f1, w2, b2, f2, w3, b3)
    m1 = _stage1_fwd_mat(n1, n1 // 2)
    y = _lmm3(m1, h.reshape(1, n1 // 2, n2 * 2 * HY_WIDTH))
    y = y.reshape(2, n1, n2, 2 * HY_WIDTH)
    twr, twi = _twiddle(n1, n2)
    f2m, _ = _stage2_mats(n2)
    return pl.pallas_call(
        _hk2_kernel,
        grid=(n1 // FFT_KC,),
        in_specs=[
            pl.BlockSpec((2, FFT_KC, n2, 2 * HY_WIDTH), lambda k: (0, k, 0, 0)),
            pl.BlockSpec((FFT_KC, n2, LANES), lambda k: (k, 0, 0)),
            pl.BlockSpec((FFT_KC, n2, LANES), lambda k: (k, 0, 0)),
            pl.BlockSpec((2 * n2, 6 * n2), lambda k: (0, 0)),
        ],
        out_specs=pl.BlockSpec((2, FFT_KC, n2, HY_WIDTH), lambda k: (0, k, 0, 0)),
        out_shape=jax.ShapeDtypeStruct((2, n1, n2, HY_WIDTH), F32),
        compiler_params=_cparams(("arbitrary",)),
        name="hk2",
    )(y, twr, twi, f2m)


def _hyena_conv(z, x0, kf, d_skip, bsz, seq_len):
    n = 2 * seq_len
    n2 = DFT_N2
    n1 = n // n2
    w = n2 * HY_WIDTH
    zv = z.reshape(bsz, n1 // 2, w)
    y = _lmm3(_stage1_fwd_mat(n1, n1 // 2), zv).reshape(bsz, 2, n1, n2, HY_WIDTH)
    twr, twi = _twiddle(n1, n2)
    f2m, f2i = _stage2_mats(n2)
    g = pl.pallas_call(
        _hy2_kernel,
        grid=(n1 // FFT_KC, bsz),
        in_specs=[
            pl.BlockSpec((None, 2, FFT_KC, n2, HY_WIDTH), lambda k, b: (b, 0, k, 0, 0)),
            pl.BlockSpec((2, FFT_KC, n2, HY_WIDTH), lambda k, b: (0, k, 0, 0)),
            pl.BlockSpec((FFT_KC, n2, LANES), lambda k, b: (k, 0, 0)),
            pl.BlockSpec((FFT_KC, n2, LANES), lambda k, b: (k, 0, 0)),
            pl.BlockSpec((2 * n2, 6 * n2), lambda k, b: (0, 0)),
            pl.BlockSpec((2 * n2, 6 * n2), lambda k, b: (0, 0)),
        ],
        out_specs=pl.BlockSpec((None, 2, FFT_KC, n2, HY_WIDTH), lambda k, b: (b, 0, k, 0, 0)),
        out_shape=jax.ShapeDtypeStruct((bsz, 2, n1, n2, HY_WIDTH), F32),
        compiler_params=_cparams(("arbitrary", "arbitrary")),
        name="hy2",
    )(y, kf, twr, twi, f2m, f2i)
    c, s = _cos_sin(n1 // 2, n1, n1)
    m3 = jnp.asarray(_cat3_cols_np(np.concatenate([c, -s], axis=1) / n))
    d_row = jnp.tile(d_skip, (1, n2))
    out = pl.pallas_call(
        _hy3_kernel,
        grid=(bsz, w // LMM_TW),
        in_specs=[
            pl.BlockSpec((n1 // 2, 6 * n1), lambda b, j: (0, 0)),
            pl.BlockSpec((None, 2 * n1, LMM_TW), lambda b, j: (b, 0, j)),
            pl.BlockSpec((None, n1 // 2, LMM_TW), lambda b, j: (b, 0, j)),
            pl.BlockSpec((None, n1 // 2, LMM_TW), lambda b, j: (b, 0, j)),
            pl.BlockSpec((1, LMM_TW), lambda b, j: (0, j)),
        ],
        out_specs=pl.BlockSpec((None, n1 // 2, LMM_TW), lambda b, j: (b, 0, j)),
        out_shape=jax.ShapeDtypeStruct((bsz, n1 // 2, w), F32),
        compiler_params=_cparams(("arbitrary", "arbitrary")),
        name="hy3",
    )(m3, g.reshape(bsz, 2 * n1, w), zv, x0.reshape(bsz, n1 // 2, w), d_row)
    return out.reshape(bsz * seq_len, HY_WIDTH)


KOUT_TM = 256


def _rms(y):
    return y * lax.rsqrt(jnp.mean(y * y, axis=-1, keepdims=True) + EPS)


def _kout_kernel(ya_ref, yb_ref, yc_ref, x_ref, mod_ref, og_ref, n2g_ref, w_ref, xo_ref, h2_ref):
    n = jnp.concatenate([_rms(ya_ref[...]), _rms(yb_ref[...]), _rms(yc_ref[...])], axis=-1) * og_ref[...]
    y = jnp.dot(n.astype(BF16), w_ref[...], preferred_element_type=F32)
    xn = x_ref[...] + mod_ref[2:3, :] * y
    xo_ref[...] = xn
    h2 = _rms(xn) * n2g_ref[...]
    h2_ref[...] = (h2 * (1.0 + mod_ref[4:5, :]) + mod_ref[3:4, :]).astype(BF16)


def _kout(ya, yb, yc, x, mod, og, n2g, w, seq_len):
    t = x.shape[0]
    tm = KOUT_TM
    row = lambda width: pl.BlockSpec((tm, width), lambda i: (i, 0))
    return pl.pallas_call(
        _kout_kernel,
        grid=(t // tm,),
        in_specs=[
            row(NA_WIDTH), row(FN_WIDTH), row(HY_WIDTH), row(D_MODEL),
            pl.BlockSpec((None, N_MOD, D_MODEL), lambda i: ((i * tm) // seq_len, 0, 0)),
            pl.BlockSpec((1, D_MODEL), lambda i: (0, 0)),
            pl.BlockSpec((1, D_MODEL), lambda i: (0, 0)),
            pl.BlockSpec((D_MODEL, D_MODEL), lambda i: (0, 0)),
        ],
        out_specs=[row(D_MODEL), row(D_MODEL)],
        out_shape=[jax.ShapeDtypeStruct((t, D_MODEL), F32), jax.ShapeDtypeStruct((t, D_MODEL), BF16)],
        compiler_params=_cparams(("arbitrary",)),
        name="kout",
    )(ya, yb, yc, x, mod, og, n2g, w)


MLP_TM = 512
MLP_TF = 512
MLP_HALO = 16


def _gelu_exact(a):
    return 0.5 * a * (1.0 + lax.erf(a * (1.0 / math.sqrt(2.0))))


def _mlp_kernel(hp_ref, hm_ref, hn_ref, x_ref, mod_ref, wa_ref, wg_ref, cw_ref, cb_ref, wd_ref, o_ref,
                hext, acc, *, seq_len):
    tm = MLP_TM
    halo = MLP_HALO
    i = pl.program_id(0)
    j = pl.program_id(1)

    @pl.when(j == 0)
    def _():
        hext[0:halo, :] = hp_ref[...]
        hext[halo:halo + tm, :] = hm_ref[...]
        hext[halo + tm:, :] = hn_ref[...]
        acc[...] = jnp.zeros_like(acc)

    at_start = (i * tm) % seq_len == 0
    at_end = ((i + 1) * tm) % seq_len == 0
    a_ext = jnp.dot(hext[...], wa_ref[...], preferred_element_type=F32)
    up = pltpu.roll(a_ext, 1, 0)[halo:halo + tm]
    dn = pltpu.roll(a_ext, tm + 2 * halo - 1, 0)[halo:halo + tm]
    mid = a_ext[halo:halo + tm]
    r = lax.broadcasted_iota(jnp.int32, (tm, 1), 0)
    up = jnp.where((r == 0) & at_start, 0.0, up)
    dn = jnp.where((r == tm - 1) & at_end, 0.0, dn)
    a = up * cw_ref[0:1, :] + mid * cw_ref[1:2, :] + dn * cw_ref[2:3, :] + cb_ref[...]
    gate = jnp.dot(hext[halo:halo + tm, :], wg_ref[...], preferred_element_type=F32)
    act = (_gelu_exact(a) * gate).astype(BF16)
    acc[...] += jnp.dot(act, wd_ref[...], preferred_element_type=F32)

    @pl.when(j == pl.num_programs(1) - 1)
    def _():
        o_ref[...] = x_ref[...] + mod_ref[5:6, :] * acc[...]


def _mlp(h2, x, mod, w_up, cw, cb, w_down, seq_len):
    t = x.shape[0]
    tm, tf, halo = MLP_TM, MLP_TF, MLP_HALO
    nf = D_FF // tf
    nbh = t // halo
    return pl.pallas_call(
        functools.partial(_mlp_kernel, seq_len=seq_len),
        grid=(t // tm, nf),
        in_specs=[
            pl.BlockSpec((halo, D_MODEL), lambda i, j: (jnp.maximum(i * (tm // halo) - 1, 0), 0)),
            pl.BlockSpec((tm, D_MODEL), lambda i, j: (i, 0)),
            pl.BlockSpec((halo, D_MODEL), lambda i, j: (jnp.minimum((i + 1) * (tm // halo), nbh - 1), 0)),
            pl.BlockSpec((tm, D_MODEL), lambda i, j: (i, 0)),
            pl.BlockSpec((None, N_MOD, D_MODEL), lambda i, j: ((i * tm) // seq_len, 0, 0)),
            pl.BlockSpec((D_MODEL, tf), lambda i, j: (0, j)),
            pl.BlockSpec((D_MODEL, tf), lambda i, j: (0, nf + j)),
            pl.BlockSpec((3, tf), lambda i, j: (0, j)),
            pl.BlockSpec((1, tf), lambda i, j: (0, j)),
            pl.BlockSpec((tf, D_MODEL), lambda i, j: (j, 0)),
        ],
        out_specs=pl.BlockSpec((tm, D_MODEL), lambda i, j: (i, 0)),
        out_shape=jax.ShapeDtypeStruct((t, D_MODEL), F32),
        scratch_shapes=[pltpu.VMEM((tm + 2 * halo, D_MODEL), BF16), pltpu.VMEM((tm, D_MODEL), F32)],
        compiler_params=_cparams(("arbitrary", "arbitrary")),
        name="mlp",
    )(h2, h2, h2, x, mod, w_up, w_up, cw, cb, w_down)


def _layer(x, mod, p, l, bsz, seq_len, kf):
    qkv, u_fn, u_hy = _kin(x, mod, p["norm1_g"][l][None], p["w_in"][l], seq_len)
    ya = _attn(qkv, p["na_rpb"][l].reshape(-1), p["q_norm_g"][l][None], p["k_norm_g"][l][None], bsz, seq_len)
    yb = _fourier(u_fn, p["gcat"][l], p["fn_b"][l][None], bsz, seq_len)
    z, x0 = _hpre(u_hy, p["hy_conv_w"][l], p["hy_conv_b"][l][None], seq_len)
    yc = _hyena_conv(z, x0, kf, p["hy_d"][l][None], bsz, seq_len)
    x1, h2 = _kout(ya, yb, yc, x, mod, p["out_norm_g"][l][None], p["norm2_g"][l][None], p["w_out"][l], seq_len)
    return _mlp(h2, x1, mod, p["mlp_w_up"][l], p["mlp_conv_w"][l], p["mlp_conv_b"][l][None], p["mlp_w_down"][l],
                seq_len)


def kernel(x_prompt, x_sample, c_prompt, c_sample, ada_w, ada_b, norm1_g, w_in, q_norm_g, k_norm_g, na_rpb, fn_w, fn_b, hy_conv_w, hy_conv_b, hy_w1, hy_b1, hy_f1, hy_w2, hy_b2, hy_f2, hy_w3, hy_b3, hy_d, out_norm_g, w_out, norm2_g, mlp_w_up, mlp_conv_w, mlp_conv_b, mlp_w_down):
    depth = ada_w.shape[0]
    groups = [(x_prompt, c_prompt), (x_sample, c_sample)]
    nseq = sum(c.shape[0] for _, c in groups)
    nrow = -(-nseq // 16) * 16
    c_all = jnp.concatenate([c for _, c in groups] + [jnp.zeros((nrow - nseq, D_MODEL), F32)], axis=0)
    mod_all = _ada(c_all, ada_w, ada_b[:, None, :]).reshape(depth, nrow, N_MOD, D_MODEL)

    p = dict(
        norm1_g=norm1_g, w_in=w_in.astype(BF16), q_norm_g=q_norm_g, k_norm_g=k_norm_g, na_rpb=na_rpb,
        fn_b=fn_b, hy_conv_w=hy_conv_w, hy_conv_b=hy_conv_b, hy_d=hy_d, out_norm_g=out_norm_g,
        w_out=w_out.astype(BF16), norm2_g=norm2_g, mlp_w_up=mlp_w_up.astype(BF16), mlp_conv_w=mlp_conv_w,
        mlp_conv_b=mlp_conv_b, mlp_w_down=mlp_w_down.astype(BF16),
        gcat=[_fnw(fn_w[l]) for l in range(depth)],
    )

    outs = []
    seq_off = 0
    kf_cache = {}
    for x, c in groups:
        bsz, seq_len, _ = x.shape
        xt = x.reshape(bsz * seq_len, D_MODEL)
        for l in range(depth):
            if (l, seq_len) not in kf_cache:
                kf_cache[(l, seq_len)] = _hyena_filter_spectrum(
                    seq_len, hy_w1[l], hy_b1[l][None], hy_f1[l][None], hy_w2[l], hy_b2[l][None], hy_f2[l][None],
                    hy_w3[l], hy_b3[l][None])
            mod = mod_all[l, seq_off:seq_off + bsz]
            xt = _layer(xt, mod, p, l, bsz, seq_len, kf_cache[(l, seq_len)])
        outs.append(xt.reshape(bsz, seq_len, D_MODEL))
        seq_off += bsz
    return tuple(outs)
```

```python
import functools
import math

import ml_dtypes
import numpy as np
import jax
import jax.numpy as jnp
from jax import lax
from jax.experimental import pallas as pl
from jax.experimental.pallas import tpu as pltpu

F32 = jnp.float32
BF16 = jnp.bfloat16

D_MODEL = 2048
GRID_W = 64
HEAD_DIM = 128
NA_HEADS = 8
NA_WIDTH = NA_HEADS * HEAD_DIM
NA_KH = 8
NA_KW = 16
FN_GROUPS = 4
FN_GROUP_DIM = 128
FN_WIDTH = 512
HY_WIDTH = 512
HY_EMB_DIM = 33
HY_HIDDEN = 64
HY_FAST_DECAY = 0.3
HY_SLOW_DECAY = 1.5
HY_TARGET = 1e-2
IN_PROJ = 3 * NA_WIDTH + FN_WIDTH + 3 * HY_WIDTH
D_FF = 5632
N_MOD = 6
EPS = 1e-6
NEG = -1e30

LANES = 128
DFT_N2 = 128
VMEM_LIMIT = 56 << 20

HIGHEST = lax.Precision.HIGHEST


def _cparams(sem):
    return pltpu.CompilerParams(dimension_semantics=sem, vmem_limit_bytes=VMEM_LIMIT)


def _bf16_split_np(m):
    hi = m.astype(ml_dtypes.bfloat16)
    lo = (m - hi.astype(np.float64)).astype(ml_dtypes.bfloat16)
    return hi, lo


def _cat3_cols_np(m):
    hi, lo = _bf16_split_np(m)
    return np.concatenate([hi, hi, lo], axis=1)


def _cos_sin(n_out, n_in, period):
    ang = 2.0 * np.pi * np.outer(np.arange(n_out), np.arange(n_in)) / period
    return np.cos(ang), np.sin(ang)


def _cat3_rows(x):
    xh = x.astype(BF16)
    xl = (x - xh.astype(F32)).astype(BF16)
    return jnp.concatenate([xh, xl, xh], axis=0)


def _cat3_cols(x):
    xh = x.astype(BF16)
    xl = (x - xh.astype(F32)).astype(BF16)
    return jnp.concatenate([xh, xl, xh], axis=1)


def _ada_kernel(c_ref, w_ref, b_ref, o_ref):
    c = c_ref[...]
    s = c * (1.0 / (1.0 + jnp.exp(-c)))
    sh = s.astype(BF16)
    sl = (s - sh.astype(F32)).astype(BF16)
    w = w_ref[...]
    wh = w.astype(BF16)
    wl = (w - wh.astype(F32)).astype(BF16)
    nrow = c.shape[0]
    both = jnp.dot(jnp.concatenate([sh, sl], axis=0), wh, preferred_element_type=F32)
    o_ref[...] = both[:nrow] + both[nrow:] + jnp.dot(sh, wl, preferred_element_type=F32) + b_ref[...]


def _ada(c_all, ada_w, ada_b):
    depth = ada_w.shape[0]
    nrow = c_all.shape[0]
    tn = 1024
    ncol = N_MOD * D_MODEL
    return pl.pallas_call(
        _ada_kernel,
        grid=(depth, ncol // tn),
        in_specs=[
            pl.BlockSpec((nrow, D_MODEL), lambda l, j: (0, 0)),
            pl.BlockSpec((None, D_MODEL, tn), lambda l, j: (l, 0, j)),
            pl.BlockSpec((None, 1, tn), lambda l, j: (l, 0, j)),
        ],
        out_specs=pl.BlockSpec((None, nrow, tn), lambda l, j: (l, 0, j)),
        out_shape=jax.ShapeDtypeStruct((depth, nrow, ncol), F32),
        compiler_params=_cparams(("arbitrary", "arbitrary")),
        name="ada",
    )(c_all, ada_w, ada_b)


KIN_TN = 512
KIN_QKV_TILES = 3 * NA_WIDTH // KIN_TN
KIN_FN_TILE = KIN_QKV_TILES
KIN_HY_TILES = 3 * HY_WIDTH // KIN_TN


def _kin_kernel(x_ref, mod_ref, g_ref, w_ref, oq_ref, ofn_ref, ohy_ref, h_scr):
    j = pl.program_id(1)

    @pl.when(j == 0)
    def _():
        x = x_ref[...]
        ms = jnp.mean(x * x, axis=-1, keepdims=True)
        hn = x * lax.rsqrt(ms + EPS) * g_ref[...]
        h_scr[...] = (hn * (1.0 + mod_ref[1:2, :]) + mod_ref[0:1, :]).astype(BF16)

    acc = jnp.dot(h_scr[...], w_ref[...], preferred_element_type=F32)

    @pl.when(j < KIN_QKV_TILES)
    def _():
        oq_ref[...] = acc

    @pl.when(j == KIN_FN_TILE)
    def _():
        ofn_ref[...] = acc

    @pl.when(j > KIN_FN_TILE)
    def _():
        ohy_ref[...] = acc


def _kin(x, mod, g, w, seq_len):
    t = x.shape[0]
    tm = 1024
    nj = IN_PROJ // KIN_TN
    return pl.pallas_call(
        _kin_kernel,
        grid=(t // tm, nj),
        in_specs=[
            pl.BlockSpec((tm, D_MODEL), lambda i, j: (i, 0)),
            pl.BlockSpec((None, N_MOD, D_MODEL), lambda i, j: ((i * tm) // seq_len, 0, 0)),
            pl.BlockSpec((1, D_MODEL), lambda i, j: (0, 0)),
            pl.BlockSpec((D_MODEL, KIN_TN), lambda i, j: (0, j)),
        ],
        out_specs=[
            pl.BlockSpec((tm, KIN_TN), lambda i, j: (i, jnp.minimum(j, KIN_QKV_TILES - 1))),
            pl.BlockSpec((tm, KIN_TN), lambda i, j: (i, 0)),
            pl.BlockSpec((tm, KIN_TN), lambda i, j: (i, jnp.clip(j - KIN_FN_TILE - 1, 0, KIN_HY_TILES - 1))),
        ],
        out_shape=[
            jax.ShapeDtypeStruct((t, 3 * NA_WIDTH), F32),
            jax.ShapeDtypeStruct((t, FN_WIDTH), F32),
            jax.ShapeDtypeStruct((t, 3 * HY_WIDTH), F32),
        ],
        scratch_shapes=[pltpu.VMEM((tm, D_MODEL), BF16)],
        compiler_params=_cparams(("arbitrary", "arbitrary")),
        name="kin",
    )(x, mod, g, w)


ATT_R = 4
ATT_KW = ATT_R + NA_KH
ATT_NORM_CHUNK = 512


def _attn_variant_tiles(rows, variant):
    if variant == 0:
        r0, ws = 0, 0
    elif variant == 1:
        r0 = ATT_R
        ws = r0 - NA_KH // 2
    else:
        r0, ws = rows - ATT_R, rows - ATT_KW
    tiles = []
    for a in range(ATT_R):
        r = r0 + a
        rs = min(max(r - NA_KH // 2, 0), rows - NA_KH)
        row = []
        for jp in range(ATT_KW // 2):
            kr = ws + 2 * jp
            valid_l = rs <= kr < rs + NA_KH
            valid_r = rs <= kr + 1 < rs + NA_KH
            d_left = kr - r + (NA_KH - 1)
            row.append((valid_l, valid_r, d_left))
        tiles.append(row)
    return tiles


def _attn_kernel(rpb_ref, q_ref, k_ref, v_ref, qg_ref, kg_ref, o_ref, qs, ks, vs, t2, bias, *, seq_len):
    rows = seq_len // GRID_W
    nb = rows // ATT_R
    h = pl.program_id(1)
    n_rd = 2 * NA_KH - 1
    n_cd = 2 * NA_KW - 1

    def norm_chunk(i, carry):
        sl = pl.ds(pl.multiple_of(i * ATT_NORM_CHUNK, ATT_NORM_CHUNK), ATT_NORM_CHUNK)
        q = q_ref[sl, :]
        qn = q * lax.rsqrt(jnp.mean(q * q, axis=-1, keepdims=True) + EPS) * qg_ref[...]
        qs[sl, :] = (qn * (HEAD_DIM ** -0.5)).astype(BF16)
        k = k_ref[sl, :]
        kn = k * lax.rsqrt(jnp.mean(k * k, axis=-1, keepdims=True) + EPS) * kg_ref[...]
        ks[sl, :] = kn.astype(BF16)
        vs[sl, :] = v_ref[sl, :].astype(BF16)
        return carry

    lax.fori_loop(0, seq_len // ATT_NORM_CHUNK, norm_chunk, 0)

    cidx = lax.broadcasted_iota(jnp.int32, (GRID_W, LANES), 0)
    lane = lax.broadcasted_iota(jnp.int32, (GRID_W, LANES), 1)
    widx = lane & (GRID_W - 1)
    left = lane < GRID_W
    dcidx = jnp.clip(widx - cidx, -(NA_KW - 1), NA_KW - 1) + (NA_KW - 1)
    base = h * (n_rd * n_cd)
    for dp in range(n_rd + 1):
        acc = jnp.zeros((GRID_W, LANES), F32)
        for e in range(n_cd):
            vl = rpb_ref[base + (dp - 1) * n_cd + e] if dp - 1 >= 0 else 0.0
            vr = rpb_ref[base + dp * n_cd + e] if dp < n_rd else 0.0
            acc = jnp.where(dcidx == e, jnp.where(left, vl, vr), acc)
        t2[dp] = acc

    c_lo = jnp.clip(cidx - NA_KW // 2, 0, GRID_W - NA_KW)
    col_ok = (widx >= c_lo) & (widx < c_lo + NA_KW)
    for variant in range(3):
        tiles = _attn_variant_tiles(rows, variant)
        for a in range(ATT_R):
            for jp in range(ATT_KW // 2):
                valid_l, valid_r, d_left = tiles[a][jp]
                if not (valid_l or valid_r):
                    tile = jnp.full((GRID_W, LANES), NEG, F32)
                else:
                    ok = col_ok
                    if not valid_l:
                        ok = ok & (~left)
                    if not valid_r:
                        ok = ok & left
                    tile = jnp.where(ok, t2[d_left + 1], NEG)
                bias[variant, a * GRID_W:(a + 1) * GRID_W, jp * LANES:(jp + 1) * LANES] = tile

    def block(rb, carry):
        variant = jnp.where(rb == 0, 0, jnp.where(rb == nb - 1, 2, 1))
        ws = jnp.clip(rb * ATT_R - NA_KH // 2, 0, rows - ATT_KW)
        qsl = pl.ds(pl.multiple_of(rb * (ATT_R * GRID_W), ATT_R * GRID_W), ATT_R * GRID_W)
        ksl = pl.ds(pl.multiple_of(ws * GRID_W, GRID_W), ATT_KW * GRID_W)
        s = lax.dot_general(qs[qsl, :], ks[ksl, :], (((1,), (1,)), ((), ())), preferred_element_type=F32)
        s = s + bias[variant]
        m = jnp.max(s, axis=-1, keepdims=True)
        p = jnp.exp(s - m)
        den = jnp.sum(p, axis=-1, keepdims=True)
        o = jnp.dot(p.astype(BF16), vs[ksl, :], preferred_element_type=F32)
        o_ref[qsl, :] = o * (1.0 / den)
        return carry

    lax.fori_loop(0, nb, block, 0, unroll=2)


def _attn(qkv, rpb_flat, qg, kg, bsz, seq_len):
    t = qkv.shape[0]
    nlb = 1
    kernel = functools.partial(_attn_kernel, seq_len=seq_len)
    return pl.pallas_call(
        kernel,
        grid=(bsz, NA_HEADS),
        in_specs=[
            pl.BlockSpec(memory_space=pltpu.SMEM),
            pl.BlockSpec((seq_len, HEAD_DIM), lambda b, h: (b, h)),
            pl.BlockSpec((seq_len, HEAD_DIM), lambda b, h: (b, NA_HEADS + h)),
            pl.BlockSpec((seq_len, HEAD_DIM), lambda b, h: (b, 2 * NA_HEADS + h)),
            pl.BlockSpec((1, HEAD_DIM), lambda b, h: (0, 0)),
            pl.BlockSpec((1, HEAD_DIM), lambda b, h: (0, 0)),
        ],
        out_specs=pl.BlockSpec((seq_len, HEAD_DIM), lambda b, h: (b, h)),
        out_shape=jax.ShapeDtypeStruct((t, NA_WIDTH), F32),
        scratch_shapes=[
            pltpu.VMEM((seq_len, HEAD_DIM), BF16),
            pltpu.VMEM((seq_len, HEAD_DIM), BF16),
            pltpu.VMEM((seq_len, HEAD_DIM), BF16),
            pltpu.VMEM((2 * NA_KH, GRID_W, LANES), F32),
            pltpu.VMEM((3, ATT_R * GRID_W, ATT_KW * GRID_W), F32),
        ],
        compiler_params=_cparams(("arbitrary", "arbitrary")),
        name="attn",
    )(rpb_flat, qkv, qkv, qkv, qg, kg)


DFT_TW = 8


def _kron_cat3(m):
    hi, lo = _bf16_split_np(m)
    eye = jnp.eye(DFT_TW, dtype=BF16)
    kh = jnp.kron(jnp.asarray(hi), eye)
    return jnp.concatenate([kh, kh, jnp.kron(jnp.asarray(lo), eye)], axis=1)


def _dft1_kernel(m_ref, x_ref, o_ref):
    kd, tw, c = x_ref.shape
    x = x_ref[...].reshape(kd * tw, c)
    y = jnp.dot(m_ref[...], _cat3_rows(x), preferred_element_type=F32)
    o_ref[...] = y.reshape(o_ref.shape)


def _dft1(mk, x):
    bsz, kd, n2, c = x.shape
    mo = mk.shape[0] // DFT_TW
    return pl.pallas_call(
        _dft1_kernel,
        grid=(bsz, n2 // DFT_TW),
        in_specs=[
            pl.BlockSpec(mk.shape, lambda b, j: (0, 0)),
            pl.BlockSpec((None, kd, DFT_TW, c), lambda b, j: (b, 0, j, 0)),
        ],
        out_specs=pl.BlockSpec((None, mo, DFT_TW, c), lambda b, j: (b, 0, j, 0)),
        out_shape=jax.ShapeDtypeStruct((bsz, mo, n2, c), F32),
        compiler_params=_cparams(("arbitrary", "arbitrary")),
        name="dft1",
    )(mk, x)


def _stage1_fwd_mat(n1, kd, scale=1.0):
    c, s = _cos_sin(n1, kd, n1)
    return _kron_cat3(np.concatenate([c, -s], axis=0) * scale)


def _stage2_mats(n2):
    c, s = _cos_sin(n2, n2, n2)
    fwd = np.block([[c, s], [-s, c]])
    inv = np.block([[c, -s], [s, c]])
    return jnp.asarray(_cat3_cols_np(fwd)), jnp.asarray(_cat3_cols_np(inv))


def _twiddle(n1, n2):
    ang = 2.0 * np.pi * np.outer(np.arange(n1), np.arange(n2)) / (n1 * n2)
    twr = jnp.asarray(np.cos(ang).astype(np.float32))
    twi = jnp.asarray((-np.sin(ang)).astype(np.float32))
    shape = (n1, n2, LANES)
    return jnp.broadcast_to(twr[:, :, None], shape), jnp.broadcast_to(twi[:, :, None], shape)


def _lane_tile(x, width):
    reps = width // x.shape[-1]
    return x if reps == 1 else jnp.concatenate([x] * reps, axis=-1)


FFT_KC = 8


def _fnw_kernel(cs_ref, w_ref, o_ref):
    o_ref[...] = jnp.dot(cs_ref[...], w_ref[...], precision=HIGHEST, preferred_element_type=F32)


def _fnw(fn_w):
    c, s = _cos_sin(FN_GROUP_DIM, FN_GROUP_DIM, FN_GROUP_DIM)
    cs = jnp.asarray(np.concatenate([c, s], axis=0).astype(np.float32))
    return pl.pallas_call(
        _fnw_kernel,
        grid=(FN_GROUPS,),
        in_specs=[
            pl.BlockSpec((2 * FN_GROUP_DIM, FN_GROUP_DIM), lambda g: (0, 0)),
            pl.BlockSpec((None, FN_GROUP_DIM, FN_GROUP_DIM), lambda g: (g, 0, 0)),
        ],
        out_specs=pl.BlockSpec((None, 2 * FN_GROUP_DIM, FN_GROUP_DIM), lambda g: (g, 0, 0)),
        out_shape=jax.ShapeDtypeStruct((FN_GROUPS, 2 * FN_GROUP_DIM, FN_GROUP_DIM), F32),
        compiler_params=_cparams(("arbitrary",)),
        name="fnw",
    )(cs, fn_w)


def _fn2_kernel(y_ref, twr_ref, twi_ref, f2_ref, g_ref, b_ref, o_ref):
    n2 = DFT_N2
    for t in range(FFT_KC):
        yr = y_ref[0, t]
        yi = y_ref[1, t]
        twr = _lane_tile(twr_ref[t], FN_WIDTH)
        twi = _lane_tile(twi_ref[t], FN_WIDTH)
        ar = yr * twr - yi * twi
        ai = yr * twi + yi * twr
        z = jnp.dot(f2_ref[...], _cat3_rows(jnp.concatenate([ar, ai], axis=0)), preferred_element_type=F32)
        zr = z[:n2]
        zi = z[n2:]
        outs = []
        for g in range(FN_GROUPS):
            sl = slice(g * FN_GROUP_DIM, (g + 1) * FN_GROUP_DIM)
            zc = jnp.concatenate([zr[:, sl], zi[:, sl]], axis=1)
            gm = g_ref[g]
            gh = gm.astype(BF16)
            gl = (gm - gh.astype(F32)).astype(BF16)
            outs.append(jnp.dot(_cat3_cols(zc), jnp.concatenate([gh, gh, gl], axis=0), preferred_element_type=F32))
        o_ref[:, t, :] = jnp.concatenate(outs, axis=1) + b_ref[...]


def _fourier(u, gcat, fn_b, bsz, seq_len):
    n2 = DFT_N2
    n1 = seq_len // n2
    m1 = _stage1_fwd_mat(n1, n1, scale=1.0 / math.sqrt(seq_len * FN_GROUP_DIM))
    y = _dft1(m1, u.reshape(bsz, n1, n2, FN_WIDTH))
    y = y.reshape(bsz, 2, n1, n2, FN_WIDTH)
    twr, twi = _twiddle(n1, n2)
    f2, _ = _stage2_mats(n2)
    out = pl.pallas_call(
        _fn2_kernel,
        grid=(n1 // FFT_KC, bsz),
        in_specs=[
            pl.BlockSpec((None, 2, FFT_KC, n2, FN_WIDTH), lambda k, b: (b, 0, k, 0, 0)),
            pl.BlockSpec((FFT_KC, n2, LANES), lambda k, b: (k, 0, 0)),
            pl.BlockSpec((FFT_KC, n2, LANES), lambda k, b: (k, 0, 0)),
            pl.BlockSpec((2 * n2, 6 * n2), lambda k, b: (0, 0)),
            pl.BlockSpec((FN_GROUPS, 2 * FN_GROUP_DIM, FN_GROUP_DIM), lambda k, b: (0, 0, 0)),
            pl.BlockSpec((1, FN_WIDTH), lambda k, b: (0, 0)),
        ],
        out_specs=pl.BlockSpec((None, n2, FFT_KC, FN_WIDTH), lambda k, b: (b, 0, k, 0)),
        out_shape=jax.ShapeDtypeStruct((bsz, n2, n1, FN_WIDTH), F32),
        compiler_params=_cparams(("arbitrary", "arbitrary")),
        name="fn2",
    )(y, twr, twi, f2, gcat, fn_b)
    return out.reshape(bsz * seq_len, FN_WIDTH)


HPRE_TM = 512


def _hpre_kernel(up_ref, um_ref, un_ref, cw_ref, cb_ref, z_ref, x0_ref, *, seq_len):
    tm = HPRE_TM
    i = pl.program_id(0)
    at_start = (i * tm) % seq_len == 0
    at_end = ((i + 1) * tm) % seq_len == 0
    u = um_ref[...]
    r = lax.broadcasted_iota(jnp.int32, (tm, 1), 0)
    prev_row = jnp.where(at_start, 0.0, up_ref[7:8, :])
    next_row = jnp.where(at_end, 0.0, un_ref[0:1, :])
    upv = jnp.where(r == 0, prev_row, pltpu.roll(u, 1, 0))
    dnv = jnp.where(r == tm - 1, next_row, pltpu.roll(u, tm - 1, 0))
    y = upv * cw_ref[0:1, :] + u * cw_ref[1:2, :] + dnv * cw_ref[2:3, :] + cb_ref[...]
    z_ref[...] = y[:, 2 * HY_WIDTH:] * y[:, HY_WIDTH:2 * HY_WIDTH]
    x0_ref[...] = y[:, :HY_WIDTH]


def _hpre(u, cw, cb, seq_len):
    t = u.shape[0]
    tm = HPRE_TM
    w = 3 * HY_WIDTH
    nb8 = t // 8
    return pl.pallas_call(
        functools.partial(_hpre_kernel, seq_len=seq_len),
        grid=(t // tm,),
        in_specs=[
            pl.BlockSpec((8, w), lambda i: (jnp.maximum(i * (tm // 8) - 1, 0), 0)),
            pl.BlockSpec((tm, w), lambda i: (i, 0)),
            pl.BlockSpec((8, w), lambda i: (jnp.minimum((i + 1) * (tm // 8), nb8 - 1), 0)),
            pl.BlockSpec((3, w), lambda i: (0, 0)),
            pl.BlockSpec((1, w), lambda i: (0, 0)),
        ],
        out_specs=[
            pl.BlockSpec((tm, HY_WIDTH), lambda i: (i, 0)),
            pl.BlockSpec((tm, HY_WIDTH), lambda i: (i, 0)),
        ],
        out_shape=[jax.ShapeDtypeStruct((t, HY_WIDTH), F32), jax.ShapeDtypeStruct((t, HY_WIDTH), F32)],
        compiler_params=_cparams(("arbitrary",)),
        name="hpre",
    )(u, u, u, cw, cb)


HFILT_TL = 512
HY_EMB_PAD = 128


def _hfilt_kernel(z_ref, w1_ref, b1_ref, f1_ref, w2_ref, b2_ref, f2_ref, w3_ref, b3_ref, dl_ref, o_ref):
    z = z_ref[...]
    h = jnp.sin(f1_ref[...] * (jnp.dot(z, w1_ref[...], precision=HIGHEST, preferred_element_type=F32) + b1_ref[...]))
    h = jnp.sin(f2_ref[...] * (jnp.dot(h, w2_ref[...], precision=HIGHEST, preferred_element_type=F32) + b2_ref[...]))
    h = jnp.dot(h, w3_ref[...], precision=HIGHEST, preferred_element_type=F32) + b3_ref[...]
    t = z[:, 0:1]
    h = h * jnp.exp(-t * dl_ref[...])
    row = lax.broadcasted_iota(jnp.int32, h.shape, 0) + pl.program_id(0) * HFILT_TL
    col = lax.broadcasted_iota(jnp.int32, h.shape, 1)
    o_ref[...] = jnp.where((row == 0) & (col >= HY_WIDTH), 0.0, h)


def _hyena_emb_np(seq_len):
    t = np.linspace(0.0, 1.0, seq_len)[:, None]
    bands = (HY_EMB_DIM - 1) // 2
    w = 2.0 * np.pi * np.arange(seq_len)[:, None] / seq_len
    fr = np.linspace(1e-4, bands - 1, bands)[None, :]
    z = np.concatenate([t, np.cos(fr * w), -np.sin(fr * w)], axis=-1)
    out = np.zeros((seq_len, HY_EMB_PAD), np.float32)
    out[:, :HY_EMB_DIM] = z
    return out


def _hfilt(seq_len, w1, b1, f1, w2, b2, f2, w3, b3):
    zemb = jnp.asarray(_hyena_emb_np(seq_len))
    max_decay = math.log(HY_TARGET) / HY_FAST_DECAY
    min_decay = math.log(HY_TARGET) / HY_SLOW_DECAY
    deltas = np.abs(np.linspace(min_decay, max_decay, HY_WIDTH))
    dl = jnp.asarray(np.tile(deltas, 2)[None, :].astype(np.float32))
    w1p = jnp.pad(w1, ((0, HY_EMB_PAD - HY_EMB_DIM), (0, 0)))
    tl = HFILT_TL
    full = lambda shape: pl.BlockSpec(shape, lambda i: (0,) * len(shape))
    return pl.pallas_call(
        _hfilt_kernel,
        grid=(seq_len // tl,),
        in_specs=[
            pl.BlockSpec((tl, HY_EMB_PAD), lambda i: (i, 0)),
            full((HY_EMB_PAD, HY_HIDDEN)), full((1, HY_HIDDEN)), full((1, HY_HIDDEN)),
            full((HY_HIDDEN, HY_HIDDEN)), full((1, HY_HIDDEN)), full((1, HY_HIDDEN)),
            full((HY_HIDDEN, 2 * HY_WIDTH)), full((1, 2 * HY_WIDTH)), full((1, 2 * HY_WIDTH)),
        ],
        out_specs=pl.BlockSpec((tl, 2 * HY_WIDTH), lambda i: (i, 0)),
        out_shape=jax.ShapeDtypeStruct((seq_len, 2 * HY_WIDTH), F32),
        compiler_params=_cparams(("arbitrary",)),
        name="hfilt",
    )(zemb, w1p, b1, f1, w2, b2, f2, w3, b3, dl)


def _hk2_kernel(y_ref, twr_ref, twi_ref, f2_ref, o_ref):
    n2 = DFT_N2
    for t in range(FFT_KC):
        yr = y_ref[0, t]
        yi = y_ref[1, t]
        twr = _lane_tile(twr_ref[t], 2 * HY_WIDTH)
        twi = _lane_tile(twi_ref[t], 2 * HY_WIDTH)
        ar = yr * twr - yi * twi
        ai = yr * twi + yi * twr
        z = jnp.dot(f2_ref[...], _cat3_rows(jnp.concatenate([ar, ai], axis=0)), preferred_element_type=F32)
        zr = z[:n2]
        zi = z[n2:]
        o_ref[0, t] = zr[:, :HY_WIDTH] + zr[:, HY_WIDTH:]
        o_ref[1, t] = zi[:, :HY_WIDTH] - zi[:, HY_WIDTH:]


def _hy2_kernel(y_ref, kf_ref, twr_ref, twi_ref, f2_ref, f2i_ref, o_ref):
    n2 = DFT_N2
    for t in range(FFT_KC):
        yr = y_ref[0, t]
        yi = y_ref[1, t]
        twr = _lane_tile(twr_ref[t], HY_WIDTH)
        twi = _lane_tile(twi_ref[t], HY_WIDTH)
        ar = yr * twr - yi * twi
        ai = yr * twi + yi * twr
        z = jnp.dot(f2_ref[...], _cat3_rows(jnp.concatenate([ar, ai], axis=0)), preferred_element_type=F32)
        zr = z[:n2]
        zi = z[n2:]
        kr = kf_ref[0, t]
        ki = kf_ref[1, t]
        pr = zr * kr - zi * ki
        pi = zr * ki + zi * kr
        g = jnp.dot(f2i_ref[...], _cat3_rows(jnp.concatenate([pr, pi], axis=0)), preferred_element_type=F32)
        gr = g[:n2]
        gi = g[n2:]
        o_ref[0, t] = gr * twr + gi * twi
        o_ref[1, t] = gi * twr - gr * twi


def _hy3_kernel(m_ref, g_ref, z_ref, x0_ref, d_ref, o_ref):
    kd, tw, c = g_ref.shape
    conv = jnp.dot(m_ref[...], _cat3_rows(g_ref[...].reshape(kd * tw, c)), preferred_element_type=F32)
    conv = conv.reshape(o_ref.shape)
    o_ref[...] = (conv + z_ref[...] * d_ref[...]) * x0_ref[...]


def _hyena_filter_spectrum(seq_len, w1, b1, f1, w2, b2, f2, w3, b3):
    n = 2 * seq_len
    n2 = DFT_N2
    n1 = n // n2
    h = _hfilt(seq_len, w1, b1, f1, w2, b2, f2, w3, b3)
    m1 = _stage1_fwd_mat(n1, n1 // 2)
    y = _dft1(m1, h.reshape(1, n1 // 2, n2, 2 * HY_WIDTH))
    y = y.reshape(2, n1, n2, 2 * HY_WIDTH)
    twr, twi = _twiddle(n1, n2)
    f2m, _ = _stage2_mats(n2)
    return pl.pallas_call(
        _hk2_kernel,
        grid=(n1 // FFT_KC,),
        in_specs=[
            pl.BlockSpec((2, FFT_KC, n2, 2 * HY_WIDTH), lambda k: (0, k, 0, 0)),
            pl.BlockSpec((FFT_KC, n2, LANES), lambda k: (k, 0, 0)),
            pl.BlockSpec((FFT_KC, n2, LANES), lambda k: (k, 0, 0)),
            pl.BlockSpec((2 * n2, 6 * n2), lambda k: (0, 0)),
        ],
        out_specs=pl.BlockSpec((2, FFT_KC, n2, HY_WIDTH), lambda k: (0, k, 0, 0)),
        out_shape=jax.ShapeDtypeStruct((2, n1, n2, HY_WIDTH), F32),
        compiler_params=_cparams(("arbitrary",)),
        name="hk2",
    )(y, twr, twi, f2m)


def _hyena_conv(z, x0, kf, d_skip, bsz, seq_len):
    n = 2 * seq_len
    n2 = DFT_N2
    n1 = n // n2
    zv = z.reshape(bsz, n1 // 2, n2, HY_WIDTH)
    y = _dft1(_stage1_fwd_mat(n1, n1 // 2), zv).reshape(bsz, 2, n1, n2, HY_WIDTH)
    twr, twi = _twiddle(n1, n2)
    f2m, f2i = _stage2_mats(n2)
    g = pl.pallas_call(
        _hy2_kernel,
        grid=(n1 // FFT_KC, bsz),
        in_specs=[
            pl.BlockSpec((None, 2, FFT_KC, n2, HY_WIDTH), lambda k, b: (b, 0, k, 0, 0)),
            pl.BlockSpec((2, FFT_KC, n2, HY_WIDTH), lambda k, b: (0, k, 0, 0)),
            pl.BlockSpec((FFT_KC, n2, LANES), lambda k, b: (k, 0, 0)),
            pl.BlockSpec((FFT_KC, n2, LANES), lambda k, b: (k, 0, 0)),
            pl.BlockSpec((2 * n2, 6 * n2), lambda k, b: (0, 0)),
            pl.BlockSpec((2 * n2, 6 * n2), lambda k, b: (0, 0)),
        ],
        out_specs=pl.BlockSpec((None, 2, FFT_KC, n2, HY_WIDTH), lambda k, b: (b, 0, k, 0, 0)),
        out_shape=jax.ShapeDtypeStruct((bsz, 2, n1, n2, HY_WIDTH), F32),
        compiler_params=_cparams(("arbitrary", "arbitrary")),
        name="hy2",
    )(y, kf, twr, twi, f2m, f2i)
    c, s = _cos_sin(n1 // 2, n1, n1)
    m3 = _kron_cat3(np.concatenate([c, -s], axis=1) / n)
    half = pl.BlockSpec((None, n1 // 2, DFT_TW, HY_WIDTH), lambda b, j: (b, 0, j, 0))
    out = pl.pallas_call(
        _hy3_kernel,
        grid=(bsz, n2 // DFT_TW),
        in_specs=[
            pl.BlockSpec(m3.shape, lambda b, j: (0, 0)),
            pl.BlockSpec((None, 2 * n1, DFT_TW, HY_WIDTH), lambda b, j: (b, 0, j, 0)),
            half,
            half,
            pl.BlockSpec((1, HY_WIDTH), lambda b, j: (0, 0)),
        ],
        out_specs=half,
        out_shape=jax.ShapeDtypeStruct((bsz, n1 // 2, n2, HY_WIDTH), F32),
        compiler_params=_cparams(("arbitrary", "arbitrary")),
        name="hy3",
    )(m3, g.reshape(bsz, 2 * n1, n2, HY_WIDTH), zv, x0.reshape(bsz, n1 // 2, n2, HY_WIDTH), d_skip)
    return out.reshape(bsz * seq_len, HY_WIDTH)


KOUT_TM = 256


def _rms(y):
    return y * lax.rsqrt(jnp.mean(y * y, axis=-1, keepdims=True) + EPS)


def _kout_kernel(ya_ref, yb_ref, yc_ref, x_ref, mod_ref, og_ref, n2g_ref, w_ref, xo_ref, h2_ref):
    n = jnp.concatenate([_rms(ya_ref[...]), _rms(yb_ref[...]), _rms(yc_ref[...])], axis=-1) * og_ref[...]
    y = jnp.dot(n.astype(BF16), w_ref[...], preferred_element_type=F32)
    xn = x_ref[...] + mod_ref[2:3, :] * y
    xo_ref[...] = xn
    h2 = _rms(xn) * n2g_ref[...]
    h2_ref[...] = (h2 * (1.0 + mod_ref[4:5, :]) + mod_ref[3:4, :]).astype(BF16)


def _kout(ya, yb, yc, x, mod, og, n2g, w, seq_len):
    t = x.shape[0]
    tm = KOUT_TM
    row = lambda width: pl.BlockSpec((tm, width), lambda i: (i, 0))
    return pl.pallas_call(
        _kout_kernel,
        grid=(t // tm,),
        in_specs=[
            row(NA_WIDTH), row(FN_WIDTH), row(HY_WIDTH), row(D_MODEL),
            pl.BlockSpec((None, N_MOD, D_MODEL), lambda i: ((i * tm) // seq_len, 0, 0)),
            pl.BlockSpec((1, D_MODEL), lambda i: (0, 0)),
            pl.BlockSpec((1, D_MODEL), lambda i: (0, 0)),
            pl.BlockSpec((D_MODEL, D_MODEL), lambda i: (0, 0)),
        ],
        out_specs=[row(D_MODEL), row(D_MODEL)],
        out_shape=[jax.ShapeDtypeStruct((t, D_MODEL), F32), jax.ShapeDtypeStruct((t, D_MODEL), BF16)],
        compiler_params=_cparams(("arbitrary",)),
        name="kout",
    )(ya, yb, yc, x, mod, og, n2g, w)


MLP_TM = 512
MLP_TF = 512
MLP_HALO = 16


def _gelu_exact(a):
    return 0.5 * a * (1.0 + lax.erf(a * (1.0 / math.sqrt(2.0))))


def _mlp_kernel(hp_ref, hm_ref, hn_ref, x_ref, mod_ref, wa_ref, wg_ref, cw_ref, cb_ref, wd_ref, o_ref,
                hext, acc, *, seq_len):
    tm = MLP_TM
    halo = MLP_HALO
    i = pl.program_id(0)
    j = pl.program_id(1)

    @pl.when(j == 0)
    def _():
        hext[0:halo, :] = hp_ref[...]
        hext[halo:halo + tm, :] = hm_ref[...]
        hext[halo + tm:, :] = hn_ref[...]
        acc[...] = jnp.zeros_like(acc)

    at_start = (i * tm) % seq_len == 0
    at_end = ((i + 1) * tm) % seq_len == 0
    a_ext = jnp.dot(hext[...], wa_ref[...], preferred_element_type=F32)
    up = pltpu.roll(a_ext, 1, 0)[halo:halo + tm]
    dn = pltpu.roll(a_ext, tm + 2 * halo - 1, 0)[halo:halo + tm]
    mid = a_ext[halo:halo + tm]
    r = lax.broadcasted_iota(jnp.int32, (tm, 1), 0)
    up = jnp.where((r == 0) & at_start, 0.0, up)
    dn = jnp.where((r == tm - 1) & at_end, 0.0, dn)
    a = up * cw_ref[0:1, :] + mid * cw_ref[1:2, :] + dn * cw_ref[2:3, :] + cb_ref[...]
    gate = jnp.dot(hext[halo:halo + tm, :], wg_ref[...], preferred_element_type=F32)
    act = (_gelu_exact(a) * gate).astype(BF16)
    acc[...] += jnp.dot(act, wd_ref[...], preferred_element_type=F32)

    @pl.when(j == pl.num_programs(1) - 1)
    def _():
        o_ref[...] = x_ref[...] + mod_ref[5:6, :] * acc[...]


def _mlp(h2, x, mod, w_up, cw, cb, w_down, seq_len):
    t = x.shape[0]
    tm, tf, halo = MLP_TM, MLP_TF, MLP_HALO
    nf = D_FF // tf
    nbh = t // halo
    return pl.pallas_call(
        functools.partial(_mlp_kernel, seq_len=seq_len),
        grid=(t // tm, nf),
        in_specs=[
            pl.BlockSpec((halo, D_MODEL), lambda i, j: (jnp.maximum(i * (tm // halo) - 1, 0), 0)),
            pl.BlockSpec((tm, D_MODEL), lambda i, j: (i, 0)),
            pl.BlockSpec((halo, D_MODEL), lambda i, j: (jnp.minimum((i + 1) * (tm // halo), nbh - 1), 0)),
            pl.BlockSpec((tm, D_MODEL), lambda i, j: (i, 0)),
            pl.BlockSpec((None, N_MOD, D_MODEL), lambda i, j: ((i * tm) // seq_len, 0, 0)),
            pl.BlockSpec((D_MODEL, tf), lambda i, j: (0, j)),
            pl.BlockSpec((D_MODEL, tf), lambda i, j: (0, nf + j)),
            pl.BlockSpec((3, tf), lambda i, j: (0, j)),
            pl.BlockSpec((1, tf), lambda i, j: (0, j)),
            pl.BlockSpec((tf, D_MODEL), lambda i, j: (j, 0)),
        ],
        out_specs=pl.BlockSpec((tm, D_MODEL), lambda i, j: (i, 0)),
        out_shape=jax.ShapeDtypeStruct((t, D_MODEL), F32),
        scratch_shapes=[pltpu.VMEM((tm + 2 * halo, D_MODEL), BF16), pltpu.VMEM((tm, D_MODEL), F32)],
        compiler_params=_cparams(("arbitrary", "arbitrary")),
        name="mlp",
    )(h2, h2, h2, x, mod, w_up, w_up, cw, cb, w_down)


def _layer(x, mod, p, l, bsz, seq_len, kf):
    qkv, u_fn, u_hy = _kin(x, mod, p["norm1_g"][l][None], p["w_in"][l], seq_len)
    ya = _attn(qkv, p["na_rpb"][l].reshape(-1), p["q_norm_g"][l][None], p["k_norm_g"][l][None], bsz, seq_len)
    yb = _fourier(u_fn, p["gcat"][l], p["fn_b"][l][None], bsz, seq_len)
    z, x0 = _hpre(u_hy, p["hy_conv_w"][l], p["hy_conv_b"][l][None], seq_len)
    yc = _hyena_conv(z, x0, kf, p["hy_d"][l][None], bsz, seq_len)
    x1, h2 = _kout(ya, yb, yc, x, mod, p["out_norm_g"][l][None], p["norm2_g"][l][None], p["w_out"][l], seq_len)
    return _mlp(h2, x1, mod, p["mlp_w_up"][l], p["mlp_conv_w"][l], p["mlp_conv_b"][l][None], p["mlp_w_down"][l],
                seq_len)


def kernel(x_prompt, x_sample, c_prompt, c_sample, ada_w, ada_b, norm1_g, w_in, q_norm_g, k_norm_g, na_rpb, fn_w, fn_b, hy_conv_w, hy_conv_b, hy_w1, hy_b1, hy_f1, hy_w2, hy_b2, hy_f2, hy_w3, hy_b3, hy_d, out_norm_g, w_out, norm2_g, mlp_w_up, mlp_conv_w, mlp_conv_b, mlp_w_down):
    depth = ada_w.shape[0]
    groups = [(x_prompt, c_prompt), (x_sample, c_sample)]
    nseq = sum(c.shape[0] for _, c in groups)
    nrow = -(-nseq // 16) * 16
    c_all = jnp.concatenate([c for _, c in groups] + [jnp.zeros((nrow - nseq, D_MODEL), F32)], axis=0)
    mod_all = _ada(c_all, ada_w, ada_b[:, None, :]).reshape(depth, nrow, N_MOD, D_MODEL)

    p = dict(
        norm1_g=norm1_g, w_in=w_in.astype(BF16), q_norm_g=q_norm_g, k_norm_g=k_norm_g, na_rpb=na_rpb,
        fn_b=fn_b, hy_conv_w=hy_conv_w, hy_conv_b=hy_conv_b, hy_d=hy_d, out_norm_g=out_norm_g,
        w_out=w_out.astype(BF16), norm2_g=norm2_g, mlp_w_up=mlp_w_up.astype(BF16), mlp_conv_w=mlp_conv_w,
        mlp_conv_b=mlp_conv_b, mlp_w_down=mlp_w_down.astype(BF16),
        gcat=[_fnw(fn_w[l]) for l in range(depth)],
    )

    outs = []
    seq_off = 0
    kf_cache = {}
    for x, c in groups:
        bsz, seq_len, _ = x.shape
        xt = x.reshape(bsz * seq_len, D_MODEL)
        for l in range(depth):
            if (l, seq_len) not in kf_cache:
                kf_cache[(l, seq_len)] = _hyena_filter_spectrum(
                    seq_len, hy_w1[l], hy_b1[l][None], hy_f1[l][None], hy_w2[l], hy_b2[l][None], hy_f2[l][None],
                    hy_w3[l], hy_b3[l][None])
            mod = mod_all[l, seq_off:seq_off + bsz]
            xt = _layer(xt, mod, p, l, bsz, seq_len, kf_cache[(l, seq_len)])
        outs.append(xt.reshape(bsz, seq_len, D_MODEL))
        seq_off += bsz
    return tuple(outs)
```

```python
import functools
import math

import ml_dtypes
import numpy as np
import jax
import jax.numpy as jnp
from jax import lax
from jax.experimental import pallas as pl
from jax.experimental.pallas import tpu as pltpu

F32 = jnp.float32
BF16 = jnp.bfloat16

D_MODEL = 2048
GRID_W = 64
HEAD_DIM = 128
NA_HEADS = 8
NA_WIDTH = NA_HEADS * HEAD_DIM
NA_KH = 8
NA_KW = 16
FN_GROUPS = 4
FN_GROUP_DIM = 128
FN_WIDTH = 512
HY_WIDTH = 512
HY_EMB_DIM = 33
HY_HIDDEN = 64
HY_FAST_DECAY = 0.3
HY_SLOW_DECAY = 1.5
HY_TARGET = 1e-2
IN_PROJ = 3 * NA_WIDTH + FN_WIDTH + 3 * HY_WIDTH
D_FF = 5632
N_MOD = 6
EPS = 1e-6
NEG = -1e30

LANES = 128
DFT_N2 = 128
VMEM_LIMIT = 56 << 20

HIGHEST = lax.Precision.HIGHEST


def _cparams(sem):
    return pltpu.CompilerParams(dimension_semantics=sem, vmem_limit_bytes=VMEM_LIMIT)


def _bf16_split_np(m):
    hi = m.astype(ml_dtypes.bfloat16)
    lo = (m - hi.astype(np.float64)).astype(ml_dtypes.bfloat16)
    return hi, lo


def _cat3_cols_np(m):
    hi, lo = _bf16_split_np(m)
    return np.concatenate([hi, hi, lo], axis=1)


def _cos_sin(n_out, n_in, period):
    ang = 2.0 * np.pi * np.outer(np.arange(n_out), np.arange(n_in)) / period
    return np.cos(ang), np.sin(ang)


def _cat3_rows(x):
    xh = x.astype(BF16)
    xl = (x - xh.astype(F32)).astype(BF16)
    return jnp.concatenate([xh, xl, xh], axis=0)


def _cat3_cols(x):
    xh = x.astype(BF16)
    xl = (x - xh.astype(F32)).astype(BF16)
    return jnp.concatenate([xh, xl, xh], axis=1)


def _ada_kernel(c_ref, w_ref, b_ref, o_ref):
    c = c_ref[...]
    s = c * (1.0 / (1.0 + jnp.exp(-c)))
    sh = s.astype(BF16)
    sl = (s - sh.astype(F32)).astype(BF16)
    w = w_ref[...]
    wh = w.astype(BF16)
    wl = (w - wh.astype(F32)).astype(BF16)
    nrow = c.shape[0]
    both = jnp.dot(jnp.concatenate([sh, sl], axis=0), wh, preferred_element_type=F32)
    o_ref[...] = both[:nrow] + both[nrow:] + jnp.dot(sh, wl, preferred_element_type=F32) + b_ref[...]


def _ada(c_all, ada_w, ada_b):
    depth = ada_w.shape[0]
    nrow = c_all.shape[0]
    tn = 1024
    ncol = N_MOD * D_MODEL
    return pl.pallas_call(
        _ada_kernel,
        grid=(depth, ncol // tn),
        in_specs=[
            pl.BlockSpec((nrow, D_MODEL), lambda l, j: (0, 0)),
            pl.BlockSpec((None, D_MODEL, tn), lambda l, j: (l, 0, j)),
            pl.BlockSpec((None, 1, tn), lambda l, j: (l, 0, j)),
        ],
        out_specs=pl.BlockSpec((None, nrow, tn), lambda l, j: (l, 0, j)),
        out_shape=jax.ShapeDtypeStruct((depth, nrow, ncol), F32),
        compiler_params=_cparams(("arbitrary", "arbitrary")),
        name="ada",
    )(c_all, ada_w, ada_b)


KIN_TN = 512
KIN_QKV_TILES = 3 * NA_WIDTH // KIN_TN
KIN_FN_TILE = KIN_QKV_TILES
KIN_HY_TILES = 3 * HY_WIDTH // KIN_TN


KNORM_TM = 512


def _modnorm(x, g, shift, scale):
    ms = jnp.mean(x * x, axis=-1, keepdims=True)
    return x * lax.rsqrt(ms + EPS) * g * (1.0 + scale) + shift


def _knorm_kernel(x_ref, mod_ref, g_ref, o_ref):
    o_ref[...] = _modnorm(x_ref[...], g_ref[...], mod_ref[0:1, :], mod_ref[1:2, :]).astype(BF16)


def _knorm(x, mod, g, seq_len):
    t = x.shape[0]
    tm = KNORM_TM
    return pl.pallas_call(
        _knorm_kernel,
        grid=(t // tm,),
        in_specs=[
            pl.BlockSpec((tm, D_MODEL), lambda i: (i, 0)),
            pl.BlockSpec((None, N_MOD, D_MODEL), lambda i: ((i * tm) // seq_len, 0, 0)),
            pl.BlockSpec((1, D_MODEL), lambda i: (0, 0)),
        ],
        out_specs=pl.BlockSpec((tm, D_MODEL), lambda i: (i, 0)),
        out_shape=jax.ShapeDtypeStruct((t, D_MODEL), BF16),
        compiler_params=_cparams(("arbitrary",)),
        name="knorm",
    )(x, mod, g)


def _head_rms(acc, gain):
    outs = []
    for hh in range(KIN_TN // HEAD_DIM):
        a = acc[:, hh * HEAD_DIM:(hh + 1) * HEAD_DIM]
        outs.append(a * lax.rsqrt(jnp.mean(a * a, axis=-1, keepdims=True) + EPS) * gain)
    return jnp.concatenate(outs, axis=-1)


def _kin_kernel(h_ref, w_ref, qg_ref, kg_ref, oq_ref, ofn_ref, ohy_ref):
    j = pl.program_id(1)
    acc = jnp.dot(h_ref[...], w_ref[...], preferred_element_type=F32)
    n_head_tiles = NA_WIDTH // KIN_TN

    @pl.when(j < n_head_tiles)
    def _():
        oq_ref[...] = _head_rms(acc, qg_ref[...] * (HEAD_DIM ** -0.5)).astype(BF16)

    @pl.when((j >= n_head_tiles) & (j < 2 * n_head_tiles))
    def _():
        oq_ref[...] = _head_rms(acc, kg_ref[...]).astype(BF16)

    @pl.when((j >= 2 * n_head_tiles) & (j < KIN_QKV_TILES))
    def _():
        oq_ref[...] = acc.astype(BF16)

    @pl.when(j == KIN_FN_TILE)
    def _():
        ofn_ref[...] = acc

    @pl.when(j > KIN_FN_TILE)
    def _():
        ohy_ref[...] = acc


def _kin(h, w, qg, kg):
    t = h.shape[0]
    tm = 1024
    nj = IN_PROJ // KIN_TN
    return pl.pallas_call(
        _kin_kernel,
        grid=(t // tm, nj),
        in_specs=[
            pl.BlockSpec((tm, D_MODEL), lambda i, j: (i, 0)),
            pl.BlockSpec((D_MODEL, KIN_TN), lambda i, j: (0, j)),
            pl.BlockSpec((1, HEAD_DIM), lambda i, j: (0, 0)),
            pl.BlockSpec((1, HEAD_DIM), lambda i, j: (0, 0)),
        ],
        out_specs=[
            pl.BlockSpec((tm, KIN_TN), lambda i, j: (i, jnp.minimum(j, KIN_QKV_TILES - 1))),
            pl.BlockSpec((tm, KIN_TN), lambda i, j: (i, 0)),
            pl.BlockSpec((tm, KIN_TN), lambda i, j: (i, jnp.clip(j - KIN_FN_TILE - 1, 0, KIN_HY_TILES - 1))),
        ],
        out_shape=[
            jax.ShapeDtypeStruct((t, 3 * NA_WIDTH), BF16),
            jax.ShapeDtypeStruct((t, FN_WIDTH), F32),
            jax.ShapeDtypeStruct((t, 3 * HY_WIDTH), F32),
        ],
        compiler_params=_cparams(("arbitrary", "arbitrary")),
        name="kin",
    )(h, w, qg, kg)


ATT_R = 4
ATT_KW = ATT_R + NA_KH


def _attn_variant_tiles(rows, variant):
    if variant == 0:
        r0, ws = 0, 0
    elif variant == 1:
        r0 = ATT_R
        ws = r0 - NA_KH // 2
    else:
        r0, ws = rows - ATT_R, rows - ATT_KW
    tiles = []
    for a in range(ATT_R):
        r = r0 + a
        rs = min(max(r - NA_KH // 2, 0), rows - NA_KH)
        row = []
        for jp in range(ATT_KW // 2):
            kr = ws + 2 * jp
            valid_l = rs <= kr < rs + NA_KH
            valid_r = rs <= kr + 1 < rs + NA_KH
            d_left = kr - r + (NA_KH - 1)
            row.append((valid_l, valid_r, d_left))
        tiles.append(row)
    return tiles


def _attn_build_bias(rpb_ref, t2, bias, h, rows):
    n_rd = 2 * NA_KH - 1
    n_cd = 2 * NA_KW - 1
    cidx = lax.broadcasted_iota(jnp.int32, (GRID_W, LANES), 0)
    lane = lax.broadcasted_iota(jnp.int32, (GRID_W, LANES), 1)
    widx = lane & (GRID_W - 1)
    left = lane < GRID_W
    dcidx = jnp.clip(widx - cidx, -(NA_KW - 1), NA_KW - 1) + (NA_KW - 1)
    base = h * (n_rd * n_cd)
    for dp in range(n_rd + 1):
        acc = jnp.zeros((GRID_W, LANES), F32)
        for e in range(n_cd):
            vl = rpb_ref[base + (dp - 1) * n_cd + e] if dp - 1 >= 0 else 0.0
            vr = rpb_ref[base + dp * n_cd + e] if dp < n_rd else 0.0
            acc = jnp.where(dcidx == e, jnp.where(left, vl, vr), acc)
        t2[dp] = acc

    c_lo = jnp.clip(cidx - NA_KW // 2, 0, GRID_W - NA_KW)
    col_ok = (widx >= c_lo) & (widx < c_lo + NA_KW)
    for variant in range(3):
        tiles = _attn_variant_tiles(rows, variant)
        for a in range(ATT_R):
            for jp in range(ATT_KW // 2):
                valid_l, valid_r, d_left = tiles[a][jp]
                if not (valid_l or valid_r):
                    tile = jnp.full((GRID_W, LANES), NEG, F32)
                else:
                    ok = col_ok
                    if not valid_l:
                        ok = ok & (~left)
                    if not valid_r:
                        ok = ok & left
                    tile = jnp.where(ok, t2[d_left + 1], NEG)
                bias[variant, a * GRID_W:(a + 1) * GRID_W, jp * LANES:(jp + 1) * LANES] = tile


def _attn_kernel(rpb_ref, q_ref, k_ref, v_ref, o_ref, t2, bias, *, seq_len):
    rows = seq_len // GRID_W
    nb = rows // ATT_R

    @pl.when(pl.program_id(1) == 0)
    def _():
        _attn_build_bias(rpb_ref, t2, bias, pl.program_id(0), rows)

    def block(rb, carry):
        variant = jnp.where(rb == 0, 0, jnp.where(rb == nb - 1, 2, 1))
        ws = jnp.clip(rb * ATT_R - NA_KH // 2, 0, rows - ATT_KW)
        qsl = pl.ds(pl.multiple_of(rb * (ATT_R * GRID_W), ATT_R * GRID_W), ATT_R * GRID_W)
        ksl = pl.ds(pl.multiple_of(ws * GRID_W, GRID_W), ATT_KW * GRID_W)
        s = lax.dot_general(q_ref[qsl, :], k_ref[ksl, :], (((1,), (1,)), ((), ())), preferred_element_type=F32)
        s = s + bias[variant]
        m = jnp.max(s, axis=-1, keepdims=True)
        p = jnp.exp(s - m)
        den = jnp.sum(p, axis=-1, keepdims=True)
        o = jnp.dot(p.astype(BF16), v_ref[ksl, :], preferred_element_type=F32)
        o_ref[qsl, :] = o * (1.0 / den)
        return carry

    lax.fori_loop(0, nb, block, 0, unroll=2)


def _attn(qkv, rpb_flat, bsz, seq_len):
    t = qkv.shape[0]
    kernel = functools.partial(_attn_kernel, seq_len=seq_len)
    return pl.pallas_call(
        kernel,
        grid=(NA_HEADS, bsz),
        in_specs=[
            pl.BlockSpec(memory_space=pltpu.SMEM),
            pl.BlockSpec((seq_len, HEAD_DIM), lambda h, b: (b, h)),
            pl.BlockSpec((seq_len, HEAD_DIM), lambda h, b: (b, NA_HEADS + h)),
            pl.BlockSpec((seq_len, HEAD_DIM), lambda h, b: (b, 2 * NA_HEADS + h)),
        ],
        out_specs=pl.BlockSpec((seq_len, HEAD_DIM), lambda h, b: (b, h)),
        out_shape=jax.ShapeDtypeStruct((t, NA_WIDTH), F32),
        scratch_shapes=[
            pltpu.VMEM((2 * NA_KH, GRID_W, LANES), F32),
            pltpu.VMEM((3, ATT_R * GRID_W, ATT_KW * GRID_W), F32),
        ],
        compiler_params=_cparams(("arbitrary", "arbitrary")),
        name="attn",
    )(rpb_flat, qkv, qkv, qkv)


DFT_TW = 8


def _kron_cat3(m):
    hi, lo = _bf16_split_np(m)
    eye = np.eye(DFT_TW, dtype=ml_dtypes.bfloat16)
    kh = np.kron(hi, eye)
    return jnp.asarray(np.concatenate([kh, kh, np.kron(lo, eye)], axis=1))


def _dft1_kernel(m_ref, x_ref, o_ref):
    kd, tw, c = x_ref.shape
    x = x_ref[...].reshape(kd * tw, c)
    y = jnp.dot(m_ref[...], _cat3_rows(x), preferred_element_type=F32)
    o_ref[...] = y.reshape(o_ref.shape)


def _dft1(mk, x):
    bsz, kd, n2, c = x.shape
    mo = mk.shape[0] // DFT_TW
    return pl.pallas_call(
        _dft1_kernel,
        grid=(bsz, n2 // DFT_TW),
        in_specs=[
            pl.BlockSpec(mk.shape, lambda b, j: (0, 0)),
            pl.BlockSpec((None, kd, DFT_TW, c), lambda b, j: (b, 0, j, 0)),
        ],
        out_specs=pl.BlockSpec((None, mo, DFT_TW, c), lambda b, j: (b, 0, j, 0)),
        out_shape=jax.ShapeDtypeStruct((bsz, mo, n2, c), F32),
        compiler_params=_cparams(("arbitrary", "arbitrary")),
        name="dft1",
    )(mk, x)


def _stage1_fwd_mat(n1, kd, scale=1.0):
    c, s = _cos_sin(n1, kd, n1)
    return _kron_cat3(np.concatenate([c, -s], axis=0) * scale)


def _stage2_mats(n2):
    c, s = _cos_sin(n2, n2, n2)
    fwd = np.block([[c, s], [-s, c]])
    inv = np.block([[c, -s], [s, c]])
    return jnp.asarray(_cat3_cols_np(fwd)), jnp.asarray(_cat3_cols_np(inv))


def _twiddle(n1, n2):
    ang = 2.0 * np.pi * np.outer(np.arange(n1), np.arange(n2)) / (n1 * n2)
    twr = jnp.asarray(np.cos(ang).astype(np.float32))
    twi = jnp.asarray((-np.sin(ang)).astype(np.float32))
    shape = (n1, n2, LANES)
    return jnp.broadcast_to(twr[:, :, None], shape), jnp.broadcast_to(twi[:, :, None], shape)


def _lane_tile(x, width):
    reps = width // x.shape[-1]
    return x if reps == 1 else jnp.concatenate([x] * reps, axis=-1)


FFT_KC = 8


def _fnw_kernel(cs_ref, w_ref, o_ref):
    o_ref[...] = jnp.dot(cs_ref[...], w_ref[...], precision=HIGHEST, preferred_element_type=F32)


def _fnw(fn_w):
    c, s = _cos_sin(FN_GROUP_DIM, FN_GROUP_DIM, FN_GROUP_DIM)
    cs = jnp.asarray(np.concatenate([c, s], axis=0).astype(np.float32))
    return pl.pallas_call(
        _fnw_kernel,
        grid=(FN_GROUPS,),
        in_specs=[
            pl.BlockSpec((2 * FN_GROUP_DIM, FN_GROUP_DIM), lambda g: (0, 0)),
            pl.BlockSpec((None, FN_GROUP_DIM, FN_GROUP_DIM), lambda g: (g, 0, 0)),
        ],
        out_specs=pl.BlockSpec((None, 2 * FN_GROUP_DIM, FN_GROUP_DIM), lambda g: (g, 0, 0)),
        out_shape=jax.ShapeDtypeStruct((FN_GROUPS, 2 * FN_GROUP_DIM, FN_GROUP_DIM), F32),
        compiler_params=_cparams(("arbitrary",)),
        name="fnw",
    )(cs, fn_w)


def _fn2_kernel(y_ref, twr_ref, twi_ref, f2_ref, g_ref, b_ref, o_ref):
    n2 = DFT_N2
    for t in range(FFT_KC):
        yr = y_ref[0, t]
        yi = y_ref[1, t]
        twr = _lane_tile(twr_ref[t], FN_WIDTH)
        twi = _lane_tile(twi_ref[t], FN_WIDTH)
        ar = yr * twr - yi * twi
        ai = yr * twi + yi * twr
        z = jnp.dot(f2_ref[...], _cat3_rows(jnp.concatenate([ar, ai], axis=0)), preferred_element_type=F32)
        zr = z[:n2]
        zi = z[n2:]
        outs = []
        for g in range(FN_GROUPS):
            sl = slice(g * FN_GROUP_DIM, (g + 1) * FN_GROUP_DIM)
            zc = jnp.concatenate([zr[:, sl], zi[:, sl]], axis=1)
            gm = g_ref[g]
            gh = gm.astype(BF16)
            gl = (gm - gh.astype(F32)).astype(BF16)
            outs.append(jnp.dot(_cat3_cols(zc), jnp.concatenate([gh, gh, gl], axis=0), preferred_element_type=F32))
        o_ref[:, t, :] = jnp.concatenate(outs, axis=1) + b_ref[...]


def _fourier(u, gcat, fn_b, bsz, seq_len):
    n2 = DFT_N2
    n1 = seq_len // n2
    m1 = _stage1_fwd_mat(n1, n1, scale=1.0 / math.sqrt(seq_len * FN_GROUP_DIM))
    y = _dft1(m1, u.reshape(bsz, n1, n2, FN_WIDTH))
    y = y.reshape(bsz, 2, n1, n2, FN_WIDTH)
    twr, twi = _twiddle(n1, n2)
    f2, _ = _stage2_mats(n2)
    out = pl.pallas_call(
        _fn2_kernel,
        grid=(n1 // FFT_KC, bsz),
        in_specs=[
            pl.BlockSpec((None, 2, FFT_KC, n2, FN_WIDTH), lambda k, b: (b, 0, k, 0, 0)),
            pl.BlockSpec((FFT_KC, n2, LANES), lambda k, b: (k, 0, 0)),
            pl.BlockSpec((FFT_KC, n2, LANES), lambda k, b: (k, 0, 0)),
            pl.BlockSpec((2 * n2, 6 * n2), lambda k, b: (0, 0)),
            pl.BlockSpec((FN_GROUPS, 2 * FN_GROUP_DIM, FN_GROUP_DIM), lambda k, b: (0, 0, 0)),
            pl.BlockSpec((1, FN_WIDTH), lambda k, b: (0, 0)),
        ],
        out_specs=pl.BlockSpec((None, n2, FFT_KC, FN_WIDTH), lambda k, b: (b, 0, k, 0)),
        out_shape=jax.ShapeDtypeStruct((bsz, n2, n1, FN_WIDTH), F32),
        compiler_params=_cparams(("arbitrary", "arbitrary")),
        name="fn2",
    )(y, twr, twi, f2, gcat, fn_b)
    return out.reshape(bsz * seq_len, FN_WIDTH)


HPRE_TM = 512


def _hpre_kernel(up_ref, um_ref, un_ref, cw_ref, cb_ref, z_ref, x0_ref, *, seq_len):
    tm = HPRE_TM
    i = pl.program_id(0)
    at_start = (i * tm) % seq_len == 0
    at_end = ((i + 1) * tm) % seq_len == 0
    u = um_ref[...]
    r = lax.broadcasted_iota(jnp.int32, (tm, 1), 0)
    prev_row = jnp.where(at_start, 0.0, up_ref[7:8, :])
    next_row = jnp.where(at_end, 0.0, un_ref[0:1, :])
    upv = jnp.where(r == 0, prev_row, pltpu.roll(u, 1, 0))
    dnv = jnp.where(r == tm - 1, next_row, pltpu.roll(u, tm - 1, 0))
    y = upv * cw_ref[0:1, :] + u * cw_ref[1:2, :] + dnv * cw_ref[2:3, :] + cb_ref[...]
    z_ref[...] = y[:, 2 * HY_WIDTH:] * y[:, HY_WIDTH:2 * HY_WIDTH]
    x0_ref[...] = y[:, :HY_WIDTH]


def _hpre(u, cw, cb, seq_len):
    t = u.shape[0]
    tm = HPRE_TM
    w = 3 * HY_WIDTH
    nb8 = t // 8
    return pl.pallas_call(
        functools.partial(_hpre_kernel, seq_len=seq_len),
        grid=(t // tm,),
        in_specs=[
            pl.BlockSpec((8, w), lambda i: (jnp.maximum(i * (tm // 8) - 1, 0), 0)),
            pl.BlockSpec((tm, w), lambda i: (i, 0)),
            pl.BlockSpec((8, w), lambda i: (jnp.minimum((i + 1) * (tm // 8), nb8 - 1), 0)),
            pl.BlockSpec((3, w), lambda i: (0, 0)),
            pl.BlockSpec((1, w), lambda i: (0, 0)),
        ],
        out_specs=[
            pl.BlockSpec((tm, HY_WIDTH), lambda i: (i, 0)),
            pl.BlockSpec((tm, HY_WIDTH), lambda i: (i, 0)),
        ],
        out_shape=[jax.ShapeDtypeStruct((t, HY_WIDTH), F32), jax.ShapeDtypeStruct((t, HY_WIDTH), F32)],
        compiler_params=_cparams(("arbitrary",)),
        name="hpre",
    )(u, u, u, cw, cb)


HFILT_TL = 512
HY_EMB_PAD = 128


def _hfilt_kernel(z_ref, w1_ref, b1_ref, f1_ref, w2_ref, b2_ref, f2_ref, w3_ref, b3_ref, dl_ref, o_ref):
    z = z_ref[...]
    h = jnp.sin(f1_ref[...] * (jnp.dot(z, w1_ref[...], precision=HIGHEST, preferred_element_type=F32) + b1_ref[...]))
    h = jnp.sin(f2_ref[...] * (jnp.dot(h, w2_ref[...], precision=HIGHEST, preferred_element_type=F32) + b2_ref[...]))
    h = jnp.dot(h, w3_ref[...], precision=HIGHEST, preferred_element_type=F32) + b3_ref[...]
    t = z[:, 0:1]
    h = h * jnp.exp(-t * dl_ref[...])
    row = lax.broadcasted_iota(jnp.int32, h.shape, 0) + pl.program_id(0) * HFILT_TL
    col = lax.broadcasted_iota(jnp.int32, h.shape, 1)
    o_ref[...] = jnp.where((row == 0) & (col >= HY_WIDTH), 0.0, h)


def _hyena_emb_np(seq_len):
    t = np.linspace(0.0, 1.0, seq_len)[:, None]
    bands = (HY_EMB_DIM - 1) // 2
    w = 2.0 * np.pi * np.arange(seq_len)[:, None] / seq_len
    fr = np.linspace(1e-4, bands - 1, bands)[None, :]
    z = np.concatenate([t, np.cos(fr * w), -np.sin(fr * w)], axis=-1)
    out = np.zeros((seq_len, HY_EMB_PAD), np.float32)
    out[:, :HY_EMB_DIM] = z
    return out


def _hfilt(seq_len, w1, b1, f1, w2, b2, f2, w3, b3):
    zemb = jnp.asarray(_hyena_emb_np(seq_len))
    max_decay = math.log(HY_TARGET) / HY_FAST_DECAY
    min_decay = math.log(HY_TARGET) / HY_SLOW_DECAY
    deltas = np.abs(np.linspace(min_decay, max_decay, HY_WIDTH))
    dl = jnp.asarray(np.tile(deltas, 2)[None, :].astype(np.float32))
    w1p = jnp.pad(w1, ((0, HY_EMB_PAD - HY_EMB_DIM), (0, 0)))
    tl = HFILT_TL
    full = lambda shape: pl.BlockSpec(shape, lambda i: (0,) * len(shape))
    return pl.pallas_call(
        _hfilt_kernel,
        grid=(seq_len // tl,),
        in_specs=[
            pl.BlockSpec((tl, HY_EMB_PAD), lambda i: (i, 0)),
            full((HY_EMB_PAD, HY_HIDDEN)), full((1, HY_HIDDEN)), full((1, HY_HIDDEN)),
            full((HY_HIDDEN, HY_HIDDEN)), full((1, HY_HIDDEN)), full((1, HY_HIDDEN)),
            full((HY_HIDDEN, 2 * HY_WIDTH)), full((1, 2 * HY_WIDTH)), full((1, 2 * HY_WIDTH)),
        ],
        out_specs=pl.BlockSpec((tl, 2 * HY_WIDTH), lambda i: (i, 0)),
        out_shape=jax.ShapeDtypeStruct((seq_len, 2 * HY_WIDTH), F32),
        compiler_params=_cparams(("arbitrary",)),
        name="hfilt",
    )(zemb, w1p, b1, f1, w2, b2, f2, w3, b3, dl)


def _hk2_kernel(y_ref, twr_ref, twi_ref, f2_ref, o_ref):
    n2 = DFT_N2
    for t in range(FFT_KC):
        yr = y_ref[0, t]
        yi = y_ref[1, t]
        twr = _lane_tile(twr_ref[t], 2 * HY_WIDTH)
        twi = _lane_tile(twi_ref[t], 2 * HY_WIDTH)
        ar = yr * twr - yi * twi
        ai = yr * twi + yi * twr
        z = jnp.dot(f2_ref[...], _cat3_rows(jnp.concatenate([ar, ai], axis=0)), preferred_element_type=F32)
        zr = z[:n2]
        zi = z[n2:]
        o_ref[0, t] = zr[:, :HY_WIDTH] + zr[:, HY_WIDTH:]
        o_ref[1, t] = zi[:, :HY_WIDTH] - zi[:, HY_WIDTH:]


def _hy2_kernel(y_ref, kf_ref, twr_ref, twi_ref, f2_ref, f2i_ref, o_ref):
    n2 = DFT_N2
    for t in range(FFT_KC):
        yr = y_ref[0, t]
        yi = y_ref[1, t]
        twr = _lane_tile(twr_ref[t], HY_WIDTH)
        twi = _lane_tile(twi_ref[t], HY_WIDTH)
        ar = yr * twr - yi * twi
        ai = yr * twi + yi * twr
        z = jnp.dot(f2_ref[...], _cat3_rows(jnp.concatenate([ar, ai], axis=0)), preferred_element_type=F32)
        zr = z[:n2]
        zi = z[n2:]
        kr = kf_ref[0, t]
        ki = kf_ref[1, t]
        pr = zr * kr - zi * ki
        pi = zr * ki + zi * kr
        g = jnp.dot(f2i_ref[...], _cat3_rows(jnp.concatenate([pr, pi], axis=0)), preferred_element_type=F32)
        gr = g[:n2]
        gi = g[n2:]
        o_ref[0, t] = gr * twr + gi * twi
        o_ref[1, t] = gi * twr - gr * twi


def _hy3_kernel(m_ref, g_ref, z_ref, x0_ref, d_ref, o_ref):
    kd, tw, c = g_ref.shape
    conv = jnp.dot(m_ref[...], _cat3_rows(g_ref[...].reshape(kd * tw, c)), preferred_element_type=F32)
    conv = conv.reshape(o_ref.shape)
    o_ref[...] = (conv + z_ref[...] * d_ref[...]) * x0_ref[...]


def _hyena_filter_spectrum(seq_len, w1, b1, f1, w2, b2, f2, w3, b3):
    n = 2 * seq_len
    n2 = DFT_N2
    n1 = n // n2
    h = _hfilt(seq_len, w1, b1, f1, w2, b2, f2, w3, b3)
    m1 = _stage1_fwd_mat(n1, n1 // 2)
    y = _dft1(m1, h.reshape(1, n1 // 2, n2, 2 * HY_WIDTH))
    y = y.reshape(2, n1, n2, 2 * HY_WIDTH)
    twr, twi = _twiddle(n1, n2)
    f2m, _ = _stage2_mats(n2)
    return pl.pallas_call(
        _hk2_kernel,
        grid=(n1 // FFT_KC,),
        in_specs=[
            pl.BlockSpec((2, FFT_KC, n2, 2 * HY_WIDTH), lambda k: (0, k, 0, 0)),
            pl.BlockSpec((FFT_KC, n2, LANES), lambda k: (k, 0, 0)),
            pl.BlockSpec((FFT_KC, n2, LANES), lambda k: (k, 0, 0)),
            pl.BlockSpec((2 * n2, 6 * n2), lambda k: (0, 0)),
        ],
        out_specs=pl.BlockSpec((2, FFT_KC, n2, HY_WIDTH), lambda k: (0, k, 0, 0)),
        out_shape=jax.ShapeDtypeStruct((2, n1, n2, HY_WIDTH), F32),
        compiler_params=_cparams(("arbitrary",)),
        name="hk2",
    )(y, twr, twi, f2m)


def _hyena_conv(z, x0, kf, d_skip, bsz, seq_len):
    n = 2 * seq_len
    n2 = DFT_N2
    n1 = n // n2
    zv = z.reshape(bsz, n1 // 2, n2, HY_WIDTH)
    y = _dft1(_stage1_fwd_mat(n1, n1 // 2), zv).reshape(bsz, 2, n1, n2, HY_WIDTH)
    twr, twi = _twiddle(n1, n2)
    f2m, f2i = _stage2_mats(n2)
    g = pl.pallas_call(
        _hy2_kernel,
        grid=(n1 // FFT_KC, bsz),
        in_specs=[
            pl.BlockSpec((None, 2, FFT_KC, n2, HY_WIDTH), lambda k, b: (b, 0, k, 0, 0)),
            pl.BlockSpec((2, FFT_KC, n2, HY_WIDTH), lambda k, b: (0, k, 0, 0)),
            pl.BlockSpec((FFT_KC, n2, LANES), lambda k, b: (k, 0, 0)),
            pl.BlockSpec((FFT_KC, n2, LANES), lambda k, b: (k, 0, 0)),
            pl.BlockSpec((2 * n2, 6 * n2), lambda k, b: (0, 0)),
            pl.BlockSpec((2 * n2, 6 * n2), lambda k, b: (0, 0)),
        ],
        out_specs=pl.BlockSpec((None, 2, FFT_KC, n2, HY_WIDTH), lambda k, b: (b, 0, k, 0, 0)),
        out_shape=jax.ShapeDtypeStruct((bsz, 2, n1, n2, HY_WIDTH), F32),
        compiler_params=_cparams(("arbitrary", "arbitrary")),
        name="hy2",
    )(y, kf, twr, twi, f2m, f2i)
    c, s = _cos_sin(n1 // 2, n1, n1)
    m3 = _kron_cat3(np.concatenate([c, -s], axis=1) / n)
    half = pl.BlockSpec((None, n1 // 2, DFT_TW, HY_WIDTH), lambda b, j: (b, 0, j, 0))
    out = pl.pallas_call(
        _hy3_kernel,
        grid=(bsz, n2 // DFT_TW),
        in_specs=[
            pl.BlockSpec(m3.shape, lambda b, j: (0, 0)),
            pl.BlockSpec((None, 2 * n1, DFT_TW, HY_WIDTH), lambda b, j: (b, 0, j, 0)),
            half,
            half,
            pl.BlockSpec((1, HY_WIDTH), lambda b, j: (0, 0)),
        ],
        out_specs=half,
        out_shape=jax.ShapeDtypeStruct((bsz, n1 // 2, n2, HY_WIDTH), F32),
        compiler_params=_cparams(("arbitrary", "arbitrary")),
        name="hy3",
    )(m3, g.reshape(bsz, 2 * n1, n2, HY_WIDTH), zv, x0.reshape(bsz, n1 // 2, n2, HY_WIDTH), d_skip)
    return out.reshape(bsz * seq_len, HY_WIDTH)


KOUT_TM = 256


def _rms(y):
    return y * lax.rsqrt(jnp.mean(y * y, axis=-1, keepdims=True) + EPS)


def _kout_kernel(ya_ref, yb_ref, yc_ref, x_ref, mod_ref, og_ref, n2g_ref, w_ref, xo_ref, h2_ref):
    n = jnp.concatenate([_rms(ya_ref[...]), _rms(yb_ref[...]), _rms(yc_ref[...])], axis=-1) * og_ref[...]
    y = jnp.dot(n.astype(BF16), w_ref[...], preferred_element_type=F32)
    xn = x_ref[...] + mod_ref[2:3, :] * y
    xo_ref[...] = xn
    h2 = _rms(xn) * n2g_ref[...]
    h2_ref[...] = (h2 * (1.0 + mod_ref[4:5, :]) + mod_ref[3:4, :]).astype(BF16)


def _kout(ya, yb, yc, x, mod, og, n2g, w, seq_len):
    t = x.shape[0]
    tm = KOUT_TM
    row = lambda width: pl.BlockSpec((tm, width), lambda i: (i, 0))
    return pl.pallas_call(
        _kout_kernel,
        grid=(t // tm,),
        in_specs=[
            row(NA_WIDTH), row(FN_WIDTH), row(HY_WIDTH), row(D_MODEL),
            pl.BlockSpec((None, N_MOD, D_MODEL), lambda i: ((i * tm) // seq_len, 0, 0)),
            pl.BlockSpec((1, D_MODEL), lambda i: (0, 0)),
            pl.BlockSpec((1, D_MODEL), lambda i: (0, 0)),
            pl.BlockSpec((D_MODEL, D_MODEL), lambda i: (0, 0)),
        ],
        out_specs=[row(D_MODEL), row(D_MODEL)],
        out_shape=[jax.ShapeDtypeStruct((t, D_MODEL), F32), jax.ShapeDtypeStruct((t, D_MODEL), BF16)],
        compiler_params=_cparams(("arbitrary",)),
        name="kout",
    )(ya, yb, yc, x, mod, og, n2g, w)


MLP_TM = 512
MLP_TF = 512
MLP_HALO = 16


def _gelu_exact(a):
    return 0.5 * a * (1.0 + lax.erf(a * (1.0 / math.sqrt(2.0))))


def _mlp_kernel(hp_ref, hm_ref, hn_ref, x_ref, mod_ref, wa_ref, wg_ref, cw_ref, cb_ref, wd_ref, *rest,
                seq_len, emit_next):
    if emit_next:
        modn_ref, gn_ref, o_ref, hnext_ref, hext, acc = rest
    else:
        o_ref, hext, acc = rest
    tm = MLP_TM
    halo = MLP_HALO
    i = pl.program_id(0)
    j = pl.program_id(1)

    @pl.when(j == 0)
    def _():
        hext[0:halo, :] = hp_ref[...]
        hext[halo:halo + tm, :] = hm_ref[...]
        hext[halo + tm:, :] = hn_ref[...]
        acc[...] = jnp.zeros_like(acc)

    at_start = (i * tm) % seq_len == 0
    at_end = ((i + 1) * tm) % seq_len == 0
    a_ext = jnp.dot(hext[...], wa_ref[...], preferred_element_type=F32)
    up = pltpu.roll(a_ext, 1, 0)[halo:halo + tm]
    dn = pltpu.roll(a_ext, tm + 2 * halo - 1, 0)[halo:halo + tm]
    mid = a_ext[halo:halo + tm]
    r = lax.broadcasted_iota(jnp.int32, (tm, 1), 0)
    up = jnp.where((r == 0) & at_start, 0.0, up)
    dn = jnp.where((r == tm - 1) & at_end, 0.0, dn)
    a = up * cw_ref[0:1, :] + mid * cw_ref[1:2, :] + dn * cw_ref[2:3, :] + cb_ref[...]
    gate = jnp.dot(hext[halo:halo + tm, :], wg_ref[...], preferred_element_type=F32)
    act = (_gelu_exact(a) * gate).astype(BF16)
    acc[...] += jnp.dot(act, wd_ref[...], preferred_element_type=F32)

    @pl.when(j == pl.num_programs(1) - 1)
    def _():
        xo = x_ref[...] + mod_ref[5:6, :] * acc[...]
        o_ref[...] = xo
        if emit_next:
            hnext_ref[...] = _modnorm(xo, gn_ref[...], modn_ref[0:1, :], modn_ref[1:2, :]).astype(BF16)


def _mlp(h2, x, mod, w_up, cw, cb, w_down, seq_len, next_norm=None):
    t = x.shape[0]
    tm, tf, halo = MLP_TM, MLP_TF, MLP_HALO
    nf = D_FF // tf
    nbh = t // halo
    emit_next = next_norm is not None
    row = pl.BlockSpec((tm, D_MODEL), lambda i, j: (i, 0))
    modspec = pl.BlockSpec((None, N_MOD, D_MODEL), lambda i, j: ((i * tm) // seq_len, 0, 0))
    in_specs = [
        pl.BlockSpec((halo, D_MODEL), lambda i, j: (jnp.maximum(i * (tm // halo) - 1, 0), 0)),
        row,
        pl.BlockSpec((halo, D_MODEL), lambda i, j: (jnp.minimum((i + 1) * (tm // halo), nbh - 1), 0)),
        row,
        modspec,
        pl.BlockSpec((D_MODEL, tf), lambda i, j: (0, j)),
        pl.BlockSpec((D_MODEL, tf), lambda i, j: (0, nf + j)),
        pl.BlockSpec((3, tf), lambda i, j: (0, j)),
        pl.BlockSpec((1, tf), lambda i, j: (0, j)),
        pl.BlockSpec((tf, D_MODEL), lambda i, j: (j, 0)),
    ]
    args = [h2, h2, h2, x, mod, w_up, w_up, cw, cb, w_down]
    out_specs = [row]
    out_shape = [jax.ShapeDtypeStruct((t, D_MODEL), F32)]
    if emit_next:
        in_specs += [modspec, pl.BlockSpec((1, D_MODEL), lambda i, j: (0, 0))]
        args += list(next_norm)
        out_specs.append(row)
        out_shape.append(jax.ShapeDtypeStruct((t, D_MODEL), BF16))
    return pl.pallas_call(
        functools.partial(_mlp_kernel, seq_len=seq_len, emit_next=emit_next),
        grid=(t // tm, nf),
        in_specs=in_specs,
        out_specs=out_specs,
        out_shape=out_shape,
        scratch_shapes=[pltpu.VMEM((tm + 2 * halo, D_MODEL), BF16), pltpu.VMEM((tm, D_MODEL), F32)],
        compiler_params=_cparams(("arbitrary", "arbitrary")),
        name="mlp",
    )(*args)


def _layer(x, h, mod, p, l, bsz, seq_len, kf, next_norm):
    qkv, u_fn, u_hy = _kin(h, p["w_in"][l], p["q_norm_g"][l][None], p["k_norm_g"][l][None])
    ya = _attn(qkv, p["na_rpb"][l].reshape(-1), bsz, seq_len)
    yb = _fourier(u_fn, p["gcat"][l], p["fn_b"][l][None], bsz, seq_len)
    z, x0 = _hpre(u_hy, p["hy_conv_w"][l], p["hy_conv_b"][l][None], seq_len)
    yc = _hyena_conv(z, x0, kf, p["hy_d"][l][None], bsz, seq_len)
    x1, h2 = _kout(ya, yb, yc, x, mod, p["out_norm_g"][l][None], p["norm2_g"][l][None], p["w_out"][l], seq_len)
    out = _mlp(h2, x1, mod, p["mlp_w_up"][l], p["mlp_conv_w"][l], p["mlp_conv_b"][l][None], p["mlp_w_down"][l],
               seq_len, next_norm)
    return (out[0], out[1]) if next_norm is not None else (out[0], None)


def kernel(x_prompt, x_sample, c_prompt, c_sample, ada_w, ada_b, norm1_g, w_in, q_norm_g, k_norm_g, na_rpb, fn_w, fn_b, hy_conv_w, hy_conv_b, hy_w1, hy_b1, hy_f1, hy_w2, hy_b2, hy_f2, hy_w3, hy_b3, hy_d, out_norm_g, w_out, norm2_g, mlp_w_up, mlp_conv_w, mlp_conv_b, mlp_w_down):
    depth = ada_w.shape[0]
    groups = [(x_prompt, c_prompt), (x_sample, c_sample)]
    nseq = sum(c.shape[0] for _, c in groups)
    nrow = -(-nseq // 16) * 16
    c_all = jnp.concatenate([c for _, c in groups] + [jnp.zeros((nrow - nseq, D_MODEL), F32)], axis=0)
    mod_all = _ada(c_all, ada_w, ada_b[:, None, :]).reshape(depth, nrow, N_MOD, D_MODEL)

    def per_layer_bf16(w):
        return [w[l].astype(BF16) for l in range(depth)]

    p = dict(
        w_in=per_layer_bf16(w_in), q_norm_g=q_norm_g, k_norm_g=k_norm_g, na_rpb=na_rpb,
        fn_b=fn_b, hy_conv_w=hy_conv_w, hy_conv_b=hy_conv_b, hy_d=hy_d, out_norm_g=out_norm_g,
        w_out=per_layer_bf16(w_out), norm2_g=norm2_g, mlp_w_up=per_layer_bf16(mlp_w_up), mlp_conv_w=mlp_conv_w,
        mlp_conv_b=mlp_conv_b, mlp_w_down=per_layer_bf16(mlp_w_down),
        gcat=[_fnw(fn_w[l]) for l in range(depth)],
    )

    outs = []
    seq_off = 0
    kf_cache = {}
    for x, c in groups:
        bsz, seq_len, _ = x.shape
        xt = x.reshape(bsz * seq_len, D_MODEL)
        mods = [mod_all[l, seq_off:seq_off + bsz] for l in range(depth)]
        ht = _knorm(xt, mods[0], norm1_g[0][None], seq_len)
        for l in range(depth):
            if (l, seq_len) not in kf_cache:
                kf_cache[(l, seq_len)] = _hyena_filter_spectrum(
                    seq_len, hy_w1[l], hy_b1[l][None], hy_f1[l][None], hy_w2[l], hy_b2[l][None], hy_f2[l][None],
                    hy_w3[l], hy_b3[l][None])
            next_norm = (mods[l + 1], norm1_g[l + 1][None]) if l + 1 < depth else None
            xt, ht = _layer(xt, ht, mods[l], p, l, bsz, seq_len, kf_cache[(l, seq_len)], next_norm)
        outs.append(xt.reshape(bsz, seq_len, D_MODEL))
        seq_off += bsz
    return tuple(outs)
```

```python
import functools
import math

import ml_dtypes
import numpy as np
import jax
import jax.numpy as jnp
from jax import lax
from jax.experimental import pallas as pl
from jax.experimental.pallas import tpu as pltpu

F32 = jnp.float32
BF16 = jnp.bfloat16

D_MODEL = 2048
GRID_W = 64
HEAD_DIM = 128
NA_HEADS = 8
NA_WIDTH = NA_HEADS * HEAD_DIM
NA_KH = 8
NA_KW = 16
FN_GROUPS = 4
FN_GROUP_DIM = 128
FN_WIDTH = 512
HY_WIDTH = 512
HY_EMB_DIM = 33
HY_HIDDEN = 64
HY_FAST_DECAY = 0.3
HY_SLOW_DECAY = 1.5
HY_TARGET = 1e-2
IN_PROJ = 3 * NA_WIDTH + FN_WIDTH + 3 * HY_WIDTH
D_FF = 5632
N_MOD = 6
EPS = 1e-6
NEG = -1e30

LANES = 128
DFT_N2 = 128
VMEM_LIMIT = 56 << 20

HIGHEST = lax.Precision.HIGHEST


def _cparams(sem):
    return pltpu.CompilerParams(dimension_semantics=sem, vmem_limit_bytes=VMEM_LIMIT)


def _bf16_split_np(m):
    hi = m.astype(ml_dtypes.bfloat16)
    lo = (m - hi.astype(np.float64)).astype(ml_dtypes.bfloat16)
    return hi, lo


def _cat3_cols_np(m):
    hi, lo = _bf16_split_np(m)
    return np.concatenate([hi, hi, lo], axis=1)


def _cos_sin(n_out, n_in, period):
    ang = 2.0 * np.pi * np.outer(np.arange(n_out), np.arange(n_in)) / period
    return np.cos(ang), np.sin(ang)


def _cat3_rows(x):
    xh = x.astype(BF16)
    xl = (x - xh.astype(F32)).astype(BF16)
    return jnp.concatenate([xh, xl, xh], axis=0)


def _cat3_cols(x):
    xh = x.astype(BF16)
    xl = (x - xh.astype(F32)).astype(BF16)
    return jnp.concatenate([xh, xl, xh], axis=1)


def _ada_kernel(c_ref, w_ref, b_ref, o_ref):
    c = c_ref[...]
    s = c * (1.0 / (1.0 + jnp.exp(-c)))
    sh = s.astype(BF16)
    sl = (s - sh.astype(F32)).astype(BF16)
    w = w_ref[...]
    wh = w.astype(BF16)
    wl = (w - wh.astype(F32)).astype(BF16)
    nrow = c.shape[0]
    both = jnp.dot(jnp.concatenate([sh, sl], axis=0), wh, preferred_element_type=F32)
    o_ref[...] = both[:nrow] + both[nrow:] + jnp.dot(sh, wl, preferred_element_type=F32) + b_ref[...]


def _ada(c_all, ada_w, ada_b):
    depth = ada_w.shape[0]
    nrow = c_all.shape[0]
    tn = 1024
    ncol = N_MOD * D_MODEL
    return pl.pallas_call(
        _ada_kernel,
        grid=(depth, ncol // tn),
        in_specs=[
            pl.BlockSpec((nrow, D_MODEL), lambda l, j: (0, 0)),
            pl.BlockSpec((None, D_MODEL, tn), lambda l, j: (l, 0, j)),
            pl.BlockSpec((None, 1, tn), lambda l, j: (l, 0, j)),
        ],
        out_specs=pl.BlockSpec((None, nrow, tn), lambda l, j: (l, 0, j)),
        out_shape=jax.ShapeDtypeStruct((depth, nrow, ncol), F32),
        compiler_params=_cparams(("arbitrary", "arbitrary")),
        name="ada",
    )(c_all, ada_w, ada_b)


KIN_TN = 512
KIN_QKV_TILES = 3 * NA_WIDTH // KIN_TN
KIN_FN_TILE = KIN_QKV_TILES
KIN_HY_TILES = 3 * HY_WIDTH // KIN_TN


KNORM_TM = 512


def _modnorm(x, g, shift, scale):
    ms = jnp.mean(x * x, axis=-1, keepdims=True)
    return x * lax.rsqrt(ms + EPS) * g * (1.0 + scale) + shift


def _knorm_kernel(x_ref, mod_ref, g_ref, o_ref):
    o_ref[...] = _modnorm(x_ref[...], g_ref[...], mod_ref[0:1, :], mod_ref[1:2, :]).astype(BF16)


def _knorm(x, mod, g, seq_len):
    t = x.shape[0]
    tm = KNORM_TM
    return pl.pallas_call(
        _knorm_kernel,
        grid=(t // tm,),
        in_specs=[
            pl.BlockSpec((tm, D_MODEL), lambda i: (i, 0)),
            pl.BlockSpec((None, N_MOD, D_MODEL), lambda i: ((i * tm) // seq_len, 0, 0)),
            pl.BlockSpec((1, D_MODEL), lambda i: (0, 0)),
        ],
        out_specs=pl.BlockSpec((tm, D_MODEL), lambda i: (i, 0)),
        out_shape=jax.ShapeDtypeStruct((t, D_MODEL), BF16),
        compiler_params=_cparams(("arbitrary",)),
        name="knorm",
    )(x, mod, g)


def _head_rms(acc, gain):
    outs = []
    for hh in range(KIN_TN // HEAD_DIM):
        a = acc[:, hh * HEAD_DIM:(hh + 1) * HEAD_DIM]
        outs.append(a * lax.rsqrt(jnp.mean(a * a, axis=-1, keepdims=True) + EPS) * gain)
    return jnp.concatenate(outs, axis=-1)


def _kin_kernel(h_ref, w_ref, qg_ref, kg_ref, oq_ref, ofn_ref, ohy_ref):
    j = pl.program_id(1)
    acc = jnp.dot(h_ref[...], w_ref[...], preferred_element_type=F32)
    n_head_tiles = NA_WIDTH // KIN_TN

    @pl.when(j < n_head_tiles)
    def _():
        oq_ref[...] = _head_rms(acc, qg_ref[...] * (HEAD_DIM ** -0.5)).astype(BF16)

    @pl.when((j >= n_head_tiles) & (j < 2 * n_head_tiles))
    def _():
        oq_ref[...] = _head_rms(acc, kg_ref[...]).astype(BF16)

    @pl.when((j >= 2 * n_head_tiles) & (j < KIN_QKV_TILES))
    def _():
        oq_ref[...] = acc.astype(BF16)

    @pl.when(j == KIN_FN_TILE)
    def _():
        ofn_ref[...] = acc

    @pl.when(j > KIN_FN_TILE)
    def _():
        ohy_ref[...] = acc


def _kin(h, w, qg, kg):
    t = h.shape[0]
    tm = 1024
    nj = IN_PROJ // KIN_TN
    return pl.pallas_call(
        _kin_kernel,
        grid=(t // tm, nj),
        in_specs=[
            pl.BlockSpec((tm, D_MODEL), lambda i, j: (i, 0)),
            pl.BlockSpec((D_MODEL, KIN_TN), lambda i, j: (0, j)),
            pl.BlockSpec((1, HEAD_DIM), lambda i, j: (0, 0)),
            pl.BlockSpec((1, HEAD_DIM), lambda i, j: (0, 0)),
        ],
        out_specs=[
            pl.BlockSpec((tm, KIN_TN), lambda i, j: (i, jnp.minimum(j, KIN_QKV_TILES - 1))),
            pl.BlockSpec((tm, KIN_TN), lambda i, j: (i, 0)),
            pl.BlockSpec((tm, KIN_TN), lambda i, j: (i, jnp.clip(j - KIN_FN_TILE - 1, 0, KIN_HY_TILES - 1))),
        ],
        out_shape=[
            jax.ShapeDtypeStruct((t, 3 * NA_WIDTH), BF16),
            jax.ShapeDtypeStruct((t, FN_WIDTH), F32),
            jax.ShapeDtypeStruct((t, 3 * HY_WIDTH), F32),
        ],
        compiler_params=_cparams(("arbitrary", "arbitrary")),
        name="kin",
    )(h, w, qg, kg)


ATT_R = 4
ATT_KW = ATT_R + NA_KH


def _attn_variant_tiles(rows, variant):
    if variant == 0:
        r0, ws = 0, 0
    elif variant == 1:
        r0 = ATT_R
        ws = r0 - NA_KH // 2
    else:
        r0, ws = rows - ATT_R, rows - ATT_KW
    tiles = []
    for a in range(ATT_R):
        r = r0 + a
        rs = min(max(r - NA_KH // 2, 0), rows - NA_KH)
        row = []
        for jp in range(ATT_KW // 2):
            kr = ws + 2 * jp
            valid_l = rs <= kr < rs + NA_KH
            valid_r = rs <= kr + 1 < rs + NA_KH
            d_left = kr - r + (NA_KH - 1)
            row.append((valid_l, valid_r, d_left))
        tiles.append(row)
    return tiles


def _attn_build_bias(rpb_ref, t2, bias, h, rows):
    n_rd = 2 * NA_KH - 1
    n_cd = 2 * NA_KW - 1
    cidx = lax.broadcasted_iota(jnp.int32, (GRID_W, LANES), 0)
    lane = lax.broadcasted_iota(jnp.int32, (GRID_W, LANES), 1)
    widx = lane & (GRID_W - 1)
    left = lane < GRID_W
    dcidx = jnp.clip(widx - cidx, -(NA_KW - 1), NA_KW - 1) + (NA_KW - 1)
    base = h * (n_rd * n_cd)
    for dp in range(n_rd + 1):
        acc = jnp.zeros((GRID_W, LANES), F32)
        for e in range(n_cd):
            vl = rpb_ref[base + (dp - 1) * n_cd + e] if dp - 1 >= 0 else 0.0
            vr = rpb_ref[base + dp * n_cd + e] if dp < n_rd else 0.0
            acc = jnp.where(dcidx == e, jnp.where(left, vl, vr), acc)
        t2[dp] = acc

    c_lo = jnp.clip(cidx - NA_KW // 2, 0, GRID_W - NA_KW)
    col_ok = (widx >= c_lo) & (widx < c_lo + NA_KW)
    for variant in range(3):
        tiles = _attn_variant_tiles(rows, variant)
        for a in range(ATT_R):
            for jp in range(ATT_KW // 2):
                valid_l, valid_r, d_left = tiles[a][jp]
                if not (valid_l or valid_r):
                    tile = jnp.full((GRID_W, LANES), NEG, F32)
                else:
                    ok = col_ok
                    if not valid_l:
                        ok = ok & (~left)
                    if not valid_r:
                        ok = ok & left
                    tile = jnp.where(ok, t2[d_left + 1], NEG)
                bias[variant, a * GRID_W:(a + 1) * GRID_W, jp * LANES:(jp + 1) * LANES] = tile


def _attn_kernel(rpb_ref, q_ref, k_ref, v_ref, o_ref, t2, bias, *, seq_len):
    rows = seq_len // GRID_W
    nb = rows // ATT_R

    @pl.when(pl.program_id(1) == 0)
    def _():
        _attn_build_bias(rpb_ref, t2, bias, pl.program_id(0), rows)

    def block(rb, carry):
        variant = jnp.where(rb == 0, 0, jnp.where(rb == nb - 1, 2, 1))
        ws = jnp.clip(rb * ATT_R - NA_KH // 2, 0, rows - ATT_KW)
        qsl = pl.ds(pl.multiple_of(rb * (ATT_R * GRID_W), ATT_R * GRID_W), ATT_R * GRID_W)
        ksl = pl.ds(pl.multiple_of(ws * GRID_W, GRID_W), ATT_KW * GRID_W)
        s = lax.dot_general(q_ref[qsl, :], k_ref[ksl, :], (((1,), (1,)), ((), ())), preferred_element_type=F32)
        s = s + bias[variant]
        m = jnp.max(s, axis=-1, keepdims=True)
        p = jnp.exp(s - m)
        den = jnp.sum(p, axis=-1, keepdims=True)
        o = jnp.dot(p.astype(BF16), v_ref[ksl, :], preferred_element_type=F32)
        o_ref[qsl, :] = o * (1.0 / den)
        return carry

    lax.fori_loop(0, nb, block, 0, unroll=2)


def _attn(qkv, rpb_flat, bsz, seq_len):
    t = qkv.shape[0]
    kernel = functools.partial(_attn_kernel, seq_len=seq_len)
    return pl.pallas_call(
        kernel,
        grid=(NA_HEADS, bsz),
        in_specs=[
            pl.BlockSpec(memory_space=pltpu.SMEM),
            pl.BlockSpec((seq_len, HEAD_DIM), lambda h, b: (b, h)),
            pl.BlockSpec((seq_len, HEAD_DIM), lambda h, b: (b, NA_HEADS + h)),
            pl.BlockSpec((seq_len, HEAD_DIM), lambda h, b: (b, 2 * NA_HEADS + h)),
        ],
        out_specs=pl.BlockSpec((seq_len, HEAD_DIM), lambda h, b: (b, h)),
        out_shape=jax.ShapeDtypeStruct((t, NA_WIDTH), F32),
        scratch_shapes=[
            pltpu.VMEM((2 * NA_KH, GRID_W, LANES), F32),
            pltpu.VMEM((3, ATT_R * GRID_W, ATT_KW * GRID_W), F32),
        ],
        compiler_params=_cparams(("arbitrary", "arbitrary")),
        name="attn",
    )(rpb_flat, qkv, qkv, qkv)


DFT_TW = 8


def _kron_mat(m, split):
    hi, lo = _bf16_split_np(m)
    eye = np.eye(DFT_TW, dtype=ml_dtypes.bfloat16)
    kh = np.kron(hi, eye)
    if not split:
        return jnp.asarray(kh)
    return jnp.asarray(np.concatenate([kh, kh, np.kron(lo, eye)], axis=1))


def _mxu_rhs(x, m_ref):
    return _cat3_rows(x) if m_ref.shape[1] == 3 * x.shape[0] else x.astype(BF16)


def _dft1_kernel(m_ref, x_ref, *rest):
    o_ref = rest[-1]
    kd, tw, c = x_ref.shape
    x = x_ref[...].reshape(kd * tw, c)
    y = jnp.dot(m_ref[...], _mxu_rhs(x, m_ref), preferred_element_type=F32).reshape(o_ref.shape)
    if len(rest) == 3:
        kh = o_ref.shape[0] // 2
        twr = _lane_tile(rest[0][...], c)
        twi = _lane_tile(rest[1][...], c)
        yr = y[:kh]
        yi = y[kh:]
        o_ref[:kh] = yr * twr - yi * twi
        o_ref[kh:] = yr * twi + yi * twr
    else:
        o_ref[...] = y


def _dft1(mk, x, tw=None):
    bsz, kd, n2, c = x.shape
    mo = mk.shape[0] // DFT_TW
    in_specs = [
        pl.BlockSpec(mk.shape, lambda b, j: (0, 0)),
        pl.BlockSpec((None, kd, DFT_TW, c), lambda b, j: (b, 0, j, 0)),
    ]
    args = [mk, x]
    if tw is not None:
        in_specs += [pl.BlockSpec((mo // 2, DFT_TW, LANES), lambda b, j: (0, j, 0))] * 2
        args += list(tw)
    return pl.pallas_call(
        _dft1_kernel,
        grid=(bsz, n2 // DFT_TW),
        in_specs=in_specs,
        out_specs=pl.BlockSpec((None, mo, DFT_TW, c), lambda b, j: (b, 0, j, 0)),
        out_shape=jax.ShapeDtypeStruct((bsz, mo, n2, c), F32),
        compiler_params=_cparams(("arbitrary", "arbitrary")),
        name="dft1",
    )(*args)


def _stage1_fwd_mat(n1, kd, split, scale=1.0, rows=None):
    c, s = _cos_sin(n1 if rows is None else rows, kd, n1)
    return _kron_mat(np.concatenate([c, -s], axis=0) * scale, split)


def _stage2_mats(n2, split):
    c, s = _cos_sin(n2, n2, n2)
    fwd = np.block([[c, s], [-s, c]])
    inv = np.block([[c, -s], [s, c]])
    if split:
        return jnp.asarray(_cat3_cols_np(fwd)), jnp.asarray(_cat3_cols_np(inv))
    return jnp.asarray(fwd.astype(ml_dtypes.bfloat16)), jnp.asarray(inv.astype(ml_dtypes.bfloat16))


def _half_rows(n1):
    return n1 // 2 + DFT_TW


def _twiddle(n1, n2, rows=None):
    rows = n1 if rows is None else rows
    ang = 2.0 * np.pi * np.outer(np.arange(rows), np.arange(n2)) / (n1 * n2)
    twr = jnp.asarray(np.cos(ang).astype(np.float32))
    twi = jnp.asarray((-np.sin(ang)).astype(np.float32))
    shape = (rows, n2, LANES)
    return jnp.broadcast_to(twr[:, :, None], shape), jnp.broadcast_to(twi[:, :, None], shape)


def _lane_tile(x, width):
    reps = width // x.shape[-1]
    return x if reps == 1 else jnp.concatenate([x] * reps, axis=-1)


FFT_KC = 8


def _fnw_kernel(cs_ref, w_ref, o_ref):
    o_ref[...] = jnp.dot(cs_ref[...], w_ref[...], precision=HIGHEST, preferred_element_type=F32)


def _fnw(fn_w):
    c, s = _cos_sin(FN_GROUP_DIM, FN_GROUP_DIM, FN_GROUP_DIM)
    cs = jnp.asarray(np.concatenate([c, s], axis=0).astype(np.float32))
    return pl.pallas_call(
        _fnw_kernel,
        grid=(FN_GROUPS,),
        in_specs=[
            pl.BlockSpec((2 * FN_GROUP_DIM, FN_GROUP_DIM), lambda g: (0, 0)),
            pl.BlockSpec((None, FN_GROUP_DIM, FN_GROUP_DIM), lambda g: (g, 0, 0)),
        ],
        out_specs=pl.BlockSpec((None, 2 * FN_GROUP_DIM, FN_GROUP_DIM), lambda g: (g, 0, 0)),
        out_shape=jax.ShapeDtypeStruct((FN_GROUPS, 2 * FN_GROUP_DIM, FN_GROUP_DIM), F32),
        compiler_params=_cparams(("arbitrary",)),
        name="fnw",
    )(cs, fn_w)


def _fn2_kernel(y_ref, f2_ref, g_ref, b_ref, o_ref):
    n2 = DFT_N2
    for t in range(FFT_KC):
        rhs = jnp.concatenate([y_ref[0, t], y_ref[1, t]], axis=0).astype(BF16)
        z = jnp.dot(f2_ref[...], rhs, preferred_element_type=F32)
        zr = z[:n2]
        zi = z[n2:]
        outs = []
        for g in range(FN_GROUPS):
            sl = slice(g * FN_GROUP_DIM, (g + 1) * FN_GROUP_DIM)
            zc = jnp.concatenate([zr[:, sl], zi[:, sl]], axis=1).astype(BF16)
            outs.append(jnp.dot(zc, g_ref[g].astype(BF16), preferred_element_type=F32))
        o_ref[:, t, :] = jnp.concatenate(outs, axis=1) + b_ref[...]


def _fourier(u, gcat, fn_b, bsz, seq_len):
    n2 = DFT_N2
    n1 = seq_len // n2
    m1 = _stage1_fwd_mat(n1, n1, False, scale=1.0 / math.sqrt(seq_len * FN_GROUP_DIM))
    y = _dft1(m1, u.reshape(bsz, n1, n2, FN_WIDTH), _twiddle(n1, n2))
    y = y.reshape(bsz, 2, n1, n2, FN_WIDTH)
    f2, _ = _stage2_mats(n2, False)
    out = pl.pallas_call(
        _fn2_kernel,
        grid=(n1 // FFT_KC, bsz),
        in_specs=[
            pl.BlockSpec((None, 2, FFT_KC, n2, FN_WIDTH), lambda k, b: (b, 0, k, 0, 0)),
            pl.BlockSpec(f2.shape, lambda k, b: (0, 0)),
            pl.BlockSpec((FN_GROUPS, 2 * FN_GROUP_DIM, FN_GROUP_DIM), lambda k, b: (0, 0, 0)),
            pl.BlockSpec((1, FN_WIDTH), lambda k, b: (0, 0)),
        ],
        out_specs=pl.BlockSpec((None, n2, FFT_KC, FN_WIDTH), lambda k, b: (b, 0, k, 0)),
        out_shape=jax.ShapeDtypeStruct((bsz, n2, n1, FN_WIDTH), F32),
        compiler_params=_cparams(("arbitrary", "arbitrary")),
        name="fn2",
    )(y, f2, gcat, fn_b)
    return out.reshape(bsz * seq_len, FN_WIDTH)


HPRE_TM = 512


def _hpre_kernel(up_ref, um_ref, un_ref, cw_ref, cb_ref, z_ref, x0_ref, *, seq_len):
    tm = HPRE_TM
    i = pl.program_id(0)
    at_start = (i * tm) % seq_len == 0
    at_end = ((i + 1) * tm) % seq_len == 0
    u = um_ref[...]
    r = lax.broadcasted_iota(jnp.int32, (tm, 1), 0)
    prev_row = jnp.where(at_start, 0.0, up_ref[7:8, :])
    next_row = jnp.where(at_end, 0.0, un_ref[0:1, :])
    upv = jnp.where(r == 0, prev_row, pltpu.roll(u, 1, 0))
    dnv = jnp.where(r == tm - 1, next_row, pltpu.roll(u, tm - 1, 0))
    y = upv * cw_ref[0:1, :] + u * cw_ref[1:2, :] + dnv * cw_ref[2:3, :] + cb_ref[...]
    z_ref[...] = y[:, 2 * HY_WIDTH:] * y[:, HY_WIDTH:2 * HY_WIDTH]
    x0_ref[...] = y[:, :HY_WIDTH]


def _hpre(u, cw, cb, seq_len):
    t = u.shape[0]
    tm = HPRE_TM
    w = 3 * HY_WIDTH
    nb8 = t // 8
    return pl.pallas_call(
        functools.partial(_hpre_kernel, seq_len=seq_len),
        grid=(t // tm,),
        in_specs=[
            pl.BlockSpec((8, w), lambda i: (jnp.maximum(i * (tm // 8) - 1, 0), 0)),
            pl.BlockSpec((tm, w), lambda i: (i, 0)),
            pl.BlockSpec((8, w), lambda i: (jnp.minimum((i + 1) * (tm // 8), nb8 - 1), 0)),
            pl.BlockSpec((3, w), lambda i: (0, 0)),
            pl.BlockSpec((1, w), lambda i: (0, 0)),
        ],
        out_specs=[
            pl.BlockSpec((tm, HY_WIDTH), lambda i: (i, 0)),
            pl.BlockSpec((tm, HY_WIDTH), lambda i: (i, 0)),
        ],
        out_shape=[jax.ShapeDtypeStruct((t, HY_WIDTH), F32), jax.ShapeDtypeStruct((t, HY_WIDTH), F32)],
        compiler_params=_cparams(("arbitrary",)),
        name="hpre",
    )(u, u, u, cw, cb)


HFILT_TL = 512
HY_EMB_PAD = 128


def _hfilt_kernel(z_ref, w1_ref, b1_ref, f1_ref, w2_ref, b2_ref, f2_ref, w3_ref, b3_ref, dl_ref, o_ref):
    z = z_ref[...]
    h = jnp.sin(f1_ref[...] * (jnp.dot(z, w1_ref[...], precision=HIGHEST, preferred_element_type=F32) + b1_ref[...]))
    h = jnp.sin(f2_ref[...] * (jnp.dot(h, w2_ref[...], precision=HIGHEST, preferred_element_type=F32) + b2_ref[...]))
    h = jnp.dot(h, w3_ref[...], precision=HIGHEST, preferred_element_type=F32) + b3_ref[...]
    t = z[:, 0:1]
    h = h * jnp.exp(-t * dl_ref[...])
    row = lax.broadcasted_iota(jnp.int32, h.shape, 0) + pl.program_id(0) * HFILT_TL
    col = lax.broadcasted_iota(jnp.int32, h.shape, 1)
    o_ref[...] = jnp.where((row == 0) & (col >= HY_WIDTH), 0.0, h)


def _hyena_emb_np(seq_len):
    t = np.linspace(0.0, 1.0, seq_len)[:, None]
    bands = (HY_EMB_DIM - 1) // 2
    w = 2.0 * np.pi * np.arange(seq_len)[:, None] / seq_len
    fr = np.linspace(1e-4, bands - 1, bands)[None, :]
    z = np.concatenate([t, np.cos(fr * w), -np.sin(fr * w)], axis=-1)
    out = np.zeros((seq_len, HY_EMB_PAD), np.float32)
    out[:, :HY_EMB_DIM] = z
    return out


def _hfilt(seq_len, w1, b1, f1, w2, b2, f2, w3, b3):
    zemb = jnp.asarray(_hyena_emb_np(seq_len))
    max_decay = math.log(HY_TARGET) / HY_FAST_DECAY
    min_decay = math.log(HY_TARGET) / HY_SLOW_DECAY
    deltas = np.abs(np.linspace(min_decay, max_decay, HY_WIDTH))
    dl = jnp.asarray(np.tile(deltas, 2)[None, :].astype(np.float32))
    w1p = jnp.pad(w1, ((0, HY_EMB_PAD - HY_EMB_DIM), (0, 0)))
    tl = HFILT_TL
    full = lambda shape: pl.BlockSpec(shape, lambda i: (0,) * len(shape))
    return pl.pallas_call(
        _hfilt_kernel,
        grid=(seq_len // tl,),
        in_specs=[
            pl.BlockSpec((tl, HY_EMB_PAD), lambda i: (i, 0)),
            full((HY_EMB_PAD, HY_HIDDEN)), full((1, HY_HIDDEN)), full((1, HY_HIDDEN)),
            full((HY_HIDDEN, HY_HIDDEN)), full((1, HY_HIDDEN)), full((1, HY_HIDDEN)),
            full((HY_HIDDEN, 2 * HY_WIDTH)), full((1, 2 * HY_WIDTH)), full((1, 2 * HY_WIDTH)),
        ],
        out_specs=pl.BlockSpec((tl, 2 * HY_WIDTH), lambda i: (i, 0)),
        out_shape=jax.ShapeDtypeStruct((seq_len, 2 * HY_WIDTH), F32),
        compiler_params=_cparams(("arbitrary",)),
        name="hfilt",
    )(zemb, w1p, b1, f1, w2, b2, f2, w3, b3, dl)


def _hk2_kernel(y_ref, twr_ref, twi_ref, f2_ref, o_ref):
    n2 = DFT_N2
    for t in range(FFT_KC):
        yr = y_ref[0, t]
        yi = y_ref[1, t]
        twr = _lane_tile(twr_ref[t], 2 * HY_WIDTH)
        twi = _lane_tile(twi_ref[t], 2 * HY_WIDTH)
        ar = yr * twr - yi * twi
        ai = yr * twi + yi * twr
        z = jnp.dot(f2_ref[...], _cat3_rows(jnp.concatenate([ar, ai], axis=0)), preferred_element_type=F32)
        zr = z[:n2]
        zi = z[n2:]
        o_ref[0, t] = zr[:, :HY_WIDTH] + zr[:, HY_WIDTH:]
        o_ref[1, t] = zi[:, :HY_WIDTH] - zi[:, HY_WIDTH:]


def _hy2_kernel(y_ref, kf_ref, twr_ref, twi_ref, f2_ref, f2i_ref, o_ref):
    n2 = DFT_N2
    for t in range(FFT_KC):
        rhs = jnp.concatenate([y_ref[0, t], y_ref[1, t]], axis=0).astype(BF16)
        z = jnp.dot(f2_ref[...], rhs, preferred_element_type=F32)
        zr = z[:n2]
        zi = z[n2:]
        kr = kf_ref[0, t]
        ki = kf_ref[1, t]
        pr = zr * kr - zi * ki
        pi = zr * ki + zi * kr
        rhs = jnp.concatenate([pr, pi], axis=0).astype(BF16)
        g = jnp.dot(f2i_ref[...], rhs, preferred_element_type=F32)
        gr = g[:n2]
        gi = g[n2:]
        twr = _lane_tile(twr_ref[t], HY_WIDTH)
        twi = _lane_tile(twi_ref[t], HY_WIDTH)
        o_ref[0, t] = gr * twr + gi * twi
        o_ref[1, t] = gi * twr - gr * twi


def _hy3_kernel(m_ref, g_ref, z_ref, x0_ref, d_ref, o_ref):
    kd, tw, c = g_ref.shape
    rhs = _mxu_rhs(g_ref[...].reshape(kd * tw, c), m_ref)
    conv = jnp.dot(m_ref[...], rhs, preferred_element_type=F32).reshape(o_ref.shape)
    o_ref[...] = (conv + z_ref[...] * d_ref[...]) * x0_ref[...]


def _hyena_filter_spectrum(seq_len, w1, b1, f1, w2, b2, f2, w3, b3):
    n = 2 * seq_len
    n2 = DFT_N2
    n1 = n // n2
    kh = _half_rows(n1)
    h = _hfilt(seq_len, w1, b1, f1, w2, b2, f2, w3, b3)
    m1 = _stage1_fwd_mat(n1, n1 // 2, True, rows=kh)
    y = _dft1(m1, h.reshape(1, n1 // 2, n2, 2 * HY_WIDTH))
    y = y.reshape(2, kh, n2, 2 * HY_WIDTH)
    twr, twi = _twiddle(n1, n2, kh)
    f2m, _ = _stage2_mats(n2, True)
    return pl.pallas_call(
        _hk2_kernel,
        grid=(kh // FFT_KC,),
        in_specs=[
            pl.BlockSpec((2, FFT_KC, n2, 2 * HY_WIDTH), lambda k: (0, k, 0, 0)),
            pl.BlockSpec((FFT_KC, n2, LANES), lambda k: (k, 0, 0)),
            pl.BlockSpec((FFT_KC, n2, LANES), lambda k: (k, 0, 0)),
            pl.BlockSpec(f2m.shape, lambda k: (0, 0)),
        ],
        out_specs=pl.BlockSpec((2, FFT_KC, n2, HY_WIDTH), lambda k: (0, k, 0, 0)),
        out_shape=jax.ShapeDtypeStruct((2, kh, n2, HY_WIDTH), F32),
        compiler_params=_cparams(("arbitrary",)),
        name="hk2",
    )(y, twr, twi, f2m)


def _hyena_conv(z, x0, kf, d_skip, bsz, seq_len):
    n = 2 * seq_len
    n2 = DFT_N2
    n1 = n // n2
    kh = _half_rows(n1)
    zv = z.reshape(bsz, n1 // 2, n2, HY_WIDTH)
    tw = _twiddle(n1, n2, kh)
    y = _dft1(_stage1_fwd_mat(n1, n1 // 2, False, rows=kh), zv, tw).reshape(bsz, 2, kh, n2, HY_WIDTH)
    f2m, f2i = _stage2_mats(n2, False)
    g = pl.pallas_call(
        _hy2_kernel,
        grid=(kh // FFT_KC, bsz),
        in_specs=[
            pl.BlockSpec((None, 2, FFT_KC, n2, HY_WIDTH), lambda k, b: (b, 0, k, 0, 0)),
            pl.BlockSpec((2, FFT_KC, n2, HY_WIDTH), lambda k, b: (0, k, 0, 0)),
            pl.BlockSpec((FFT_KC, n2, LANES), lambda k, b: (k, 0, 0)),
            pl.BlockSpec((FFT_KC, n2, LANES), lambda k, b: (k, 0, 0)),
            pl.BlockSpec(f2m.shape, lambda k, b: (0, 0)),
            pl.BlockSpec(f2i.shape, lambda k, b: (0, 0)),
        ],
        out_specs=pl.BlockSpec((None, 2, FFT_KC, n2, HY_WIDTH), lambda k, b: (b, 0, k, 0, 0)),
        out_shape=jax.ShapeDtypeStruct((bsz, 2, kh, n2, HY_WIDTH), F32),
        compiler_params=_cparams(("arbitrary", "arbitrary")),
        name="hy2",
    )(y, kf, tw[0], tw[1], f2m, f2i)
    c, s = _cos_sin(n1 // 2, kh, n1)
    wt = np.where(np.arange(kh) < n1 // 2, 2.0, 0.0)
    wt[0] = 1.0
    wt[n1 // 2] = 1.0
    m3 = _kron_mat(np.concatenate([c * wt, -s * wt], axis=1) / n, False)
    half = pl.BlockSpec((None, n1 // 2, DFT_TW, HY_WIDTH), lambda b, j: (b, 0, j, 0))
    out = pl.pallas_call(
        _hy3_kernel,
        grid=(bsz, n2 // DFT_TW),
        in_specs=[
            pl.BlockSpec(m3.shape, lambda b, j: (0, 0)),
            pl.BlockSpec((None, 2 * kh, DFT_TW, HY_WIDTH), lambda b, j: (b, 0, j, 0)),
            half,
            half,
            pl.BlockSpec((1, HY_WIDTH), lambda b, j: (0, 0)),
        ],
        out_specs=half,
        out_shape=jax.ShapeDtypeStruct((bsz, n1 // 2, n2, HY_WIDTH), F32),
        compiler_params=_cparams(("arbitrary", "arbitrary")),
        name="hy3",
    )(m3, g.reshape(bsz, 2 * kh, n2, HY_WIDTH), zv, x0.reshape(bsz, n1 // 2, n2, HY_WIDTH), d_skip)
    return out.reshape(bsz * seq_len, HY_WIDTH)


KOUT_TM = 256


def _rms(y):
    return y * lax.rsqrt(jnp.mean(y * y, axis=-1, keepdims=True) + EPS)


def _kout_kernel(ya_ref, yb_ref, yc_ref, x_ref, mod_ref, og_ref, n2g_ref, w_ref, xo_ref, h2_ref):
    n = jnp.concatenate([_rms(ya_ref[...]), _rms(yb_ref[...]), _rms(yc_ref[...])], axis=-1) * og_ref[...]
    y = jnp.dot(n.astype(BF16), w_ref[...], preferred_element_type=F32)
    xn = x_ref[...] + mod_ref[2:3, :] * y
    xo_ref[...] = xn
    h2 = _rms(xn) * n2g_ref[...]
    h2_ref[...] = (h2 * (1.0 + mod_ref[4:5, :]) + mod_ref[3:4, :]).astype(BF16)


def _kout(ya, yb, yc, x, mod, og, n2g, w, seq_len):
    t = x.shape[0]
    tm = KOUT_TM
    row = lambda width: pl.BlockSpec((tm, width), lambda i: (i, 0))
    return pl.pallas_call(
        _kout_kernel,
        grid=(t // tm,),
        in_specs=[
            row(NA_WIDTH), row(FN_WIDTH), row(HY_WIDTH), row(D_MODEL),
            pl.BlockSpec((None, N_MOD, D_MODEL), lambda i: ((i * tm) // seq_len, 0, 0)),
            pl.BlockSpec((1, D_MODEL), lambda i: (0, 0)),
            pl.BlockSpec((1, D_MODEL), lambda i: (0, 0)),
            pl.BlockSpec((D_MODEL, D_MODEL), lambda i: (0, 0)),
        ],
        out_specs=[row(D_MODEL), row(D_MODEL)],
        out_shape=[jax.ShapeDtypeStruct((t, D_MODEL), F32), jax.ShapeDtypeStruct((t, D_MODEL), BF16)],
        compiler_params=_cparams(("arbitrary",)),
        name="kout",
    )(ya, yb, yc, x, mod, og, n2g, w)


MLP_TM = 512
MLP_TF = 512
MLP_HALO = 16


def _gelu_exact(a):
    return 0.5 * a * (1.0 + lax.erf(a * (1.0 / math.sqrt(2.0))))


def _mlp_kernel(hp_ref, hm_ref, hn_ref, x_ref, mod_ref, wa_ref, wg_ref, cw_ref, cb_ref, wd_ref, *rest,
                seq_len, emit_next):
    if emit_next:
        modn_ref, gn_ref, o_ref, hnext_ref, hext, acc = rest
    else:
        o_ref, hext, acc = rest
    tm = MLP_TM
    halo = MLP_HALO
    i = pl.program_id(0)
    j = pl.program_id(1)

    @pl.when(j == 0)
    def _():
        hext[0:halo, :] = hp_ref[...]
        hext[halo:halo + tm, :] = hm_ref[...]
        hext[halo + tm:, :] = hn_ref[...]
        acc[...] = jnp.zeros_like(acc)

    at_start = (i * tm) % seq_len == 0
    at_end = ((i + 1) * tm) % seq_len == 0
    a_ext = jnp.dot(hext[...], wa_ref[...], preferred_element_type=F32)
    up = pltpu.roll(a_ext, 1, 0)[halo:halo + tm]
    dn = pltpu.roll(a_ext, tm + 2 * halo - 1, 0)[halo:halo + tm]
    mid = a_ext[halo:halo + tm]
    r = lax.broadcasted_iota(jnp.int32, (tm, 1), 0)
    up = jnp.where((r == 0) & at_start, 0.0, up)
    dn = jnp.where((r == tm - 1) & at_end, 0.0, dn)
    a = up * cw_ref[0:1, :] + mid * cw_ref[1:2, :] + dn * cw_ref[2:3, :] + cb_ref[...]
    gate = jnp.dot(hext[halo:halo + tm, :], wg_ref[...], preferred_element_type=F32)
    act = (_gelu_exact(a) * gate).astype(BF16)
    acc[...] += jnp.dot(act, wd_ref[...], preferred_element_type=F32)

    @pl.when(j == pl.num_programs(1) - 1)
    def _():
        xo = x_ref[...] + mod_ref[5:6, :] * acc[...]
        o_ref[...] = xo
        if emit_next:
            hnext_ref[...] = _modnorm(xo, gn_ref[...], modn_ref[0:1, :], modn_ref[1:2, :]).astype(BF16)


def _mlp(h2, x, mod, w_up, cw, cb, w_down, seq_len, next_norm=None):
    t = x.shape[0]
    tm, tf, halo = MLP_TM, MLP_TF, MLP_HALO
    nf = D_FF // tf
    nbh = t // halo
    emit_next = next_norm is not None
    row = pl.BlockSpec((tm, D_MODEL), lambda i, j: (i, 0))
    modspec = pl.BlockSpec((None, N_MOD, D_MODEL), lambda i, j: ((i * tm) // seq_len, 0, 0))
    in_specs = [
        pl.BlockSpec((halo, D_MODEL), lambda i, j: (jnp.maximum(i * (tm // halo) - 1, 0), 0)),
        row,
        pl.BlockSpec((halo, D_MODEL), lambda i, j: (jnp.minimum((i + 1) * (tm // halo), nbh - 1), 0)),
        row,
        modspec,
        pl.BlockSpec((D_MODEL, tf), lambda i, j: (0, j)),
        pl.BlockSpec((D_MODEL, tf), lambda i, j: (0, nf + j)),
        pl.BlockSpec((3, tf), lambda i, j: (0, j)),
        pl.BlockSpec((1, tf), lambda i, j: (0, j)),
        pl.BlockSpec((tf, D_MODEL), lambda i, j: (j, 0)),
    ]
    args = [h2, h2, h2, x, mod, w_up, w_up, cw, cb, w_down]
    out_specs = [row]
    out_shape = [jax.ShapeDtypeStruct((t, D_MODEL), F32)]
    if emit_next:
        in_specs += [modspec, pl.BlockSpec((1, D_MODEL), lambda i, j: (0, 0))]
        args += list(next_norm)
        out_specs.append(row)
        out_shape.append(jax.ShapeDtypeStruct((t, D_MODEL), BF16))
    return pl.pallas_call(
        functools.partial(_mlp_kernel, seq_len=seq_len, emit_next=emit_next),
        grid=(t // tm, nf),
        in_specs=in_specs,
        out_specs=out_specs,
        out_shape=out_shape,
        scratch_shapes=[pltpu.VMEM((tm + 2 * halo, D_MODEL), BF16), pltpu.VMEM((tm, D_MODEL), F32)],
        compiler_params=_cparams(("arbitrary", "arbitrary")),
        name="mlp",
    )(*args)


def _layer(x, h, mod, p, l, bsz, seq_len, kf, next_norm):
    qkv, u_fn, u_hy = _kin(h, p["w_in"][l], p["q_norm_g"][l][None], p["k_norm_g"][l][None])
    ya = _attn(qkv, p["na_rpb"][l].reshape(-1), bsz, seq_len)
    yb = _fourier(u_fn, p["gcat"][l], p["fn_b"][l][None], bsz, seq_len)
    z, x0 = _hpre(u_hy, p["hy_conv_w"][l], p["hy_conv_b"][l][None], seq_len)
    yc = _hyena_conv(z, x0, kf, p["hy_d"][l][None], bsz, seq_len)
    x1, h2 = _kout(ya, yb, yc, x, mod, p["out_norm_g"][l][None], p["norm2_g"][l][None], p["w_out"][l], seq_len)
    out = _mlp(h2, x1, mod, p["mlp_w_up"][l], p["mlp_conv_w"][l], p["mlp_conv_b"][l][None], p["mlp_w_down"][l],
               seq_len, next_norm)
    return (out[0], out[1]) if next_norm is not None else (out[0], None)


def kernel(x_prompt, x_sample, c_prompt, c_sample, ada_w, ada_b, norm1_g, w_in, q_norm_g, k_norm_g, na_rpb, fn_w, fn_b, hy_conv_w, hy_conv_b, hy_w1, hy_b1, hy_f1, hy_w2, hy_b2, hy_f2, hy_w3, hy_b3, hy_d, out_norm_g, w_out, norm2_g, mlp_w_up, mlp_conv_w, mlp_conv_b, mlp_w_down):
    depth = ada_w.shape[0]
    groups = [(x_prompt, c_prompt), (x_sample, c_sample)]
    nseq = sum(c.shape[0] for _, c in groups)
    nrow = -(-nseq // 16) * 16
    c_all = jnp.concatenate([c for _, c in groups] + [jnp.zeros((nrow - nseq, D_MODEL), F32)], axis=0)
    mod_all = _ada(c_all, ada_w, ada_b[:, None, :]).reshape(depth, nrow, N_MOD, D_MODEL)

    def per_layer_bf16(w):
        return [w[l].astype(BF16) for l in range(depth)]

    p = dict(
        w_in=per_layer_bf16(w_in), q_norm_g=q_norm_g, k_norm_g=k_norm_g, na_rpb=na_rpb,
        fn_b=fn_b, hy_conv_w=hy_conv_w, hy_conv_b=hy_conv_b, hy_d=hy_d, out_norm_g=out_norm_g,
        w_out=per_layer_bf16(w_out), norm2_g=norm2_g, mlp_w_up=per_layer_bf16(mlp_w_up), mlp_conv_w=mlp_conv_w,
        mlp_conv_b=mlp_conv_b, mlp_w_down=per_layer_bf16(mlp_w_down),
        gcat=[_fnw(fn_w[l]) for l in range(depth)],
    )

    outs = []
    seq_off = 0
    kf_cache = {}
    for x, c in groups:
        bsz, seq_len, _ = x.shape
        xt = x.reshape(bsz * seq_len, D_MODEL)
        mods = [mod_all[l, seq_off:seq_off + bsz] for l in range(depth)]
        ht = _knorm(xt, mods[0], norm1_g[0][None], seq_len)
        for l in range(depth):
            if (l, seq_len) not in kf_cache:
                kf_cache[(l, seq_len)] = _hyena_filter_spectrum(
                    seq_len, hy_w1[l], hy_b1[l][None], hy_f1[l][None], hy_w2[l], hy_b2[l][None], hy_f2[l][None],
                    hy_w3[l], hy_b3[l][None])
            next_norm = (mods[l + 1], norm1_g[l + 1][None]) if l + 1 < depth else None
            xt, ht = _layer(xt, ht, mods[l], p, l, bsz, seq_len, kf_cache[(l, seq_len)], next_norm)
        outs.append(xt.reshape(bsz, seq_len, D_MODEL))
        seq_off += bsz
    return tuple(outs)
```

```python
import functools
import math

import ml_dtypes
import numpy as np
import jax
import jax.numpy as jnp
from jax import lax
from jax.experimental import pallas as pl
from jax.experimental.pallas import tpu as pltpu

F32 = jnp.float32
BF16 = jnp.bfloat16

D_MODEL = 2048
GRID_W = 64
HEAD_DIM = 128
NA_HEADS = 8
NA_WIDTH = NA_HEADS * HEAD_DIM
NA_KH = 8
NA_KW = 16
FN_GROUPS = 4
FN_GROUP_DIM = 128
FN_WIDTH = 512
HY_WIDTH = 512
HY_EMB_DIM = 33
HY_HIDDEN = 64
HY_FAST_DECAY = 0.3
HY_SLOW_DECAY = 1.5
HY_TARGET = 1e-2
IN_PROJ = 3 * NA_WIDTH + FN_WIDTH + 3 * HY_WIDTH
D_FF = 5632
N_MOD = 6
EPS = 1e-6
NEG = -1e30
LOG2E = math.log2(math.e)

LANES = 128
DFT_N2 = 128
VMEM_LIMIT = 56 << 20

HIGHEST = lax.Precision.HIGHEST


def _cparams(sem):
    return pltpu.CompilerParams(dimension_semantics=sem, vmem_limit_bytes=VMEM_LIMIT)


def _cos_sin(n_out, n_in, period):
    ang = 2.0 * np.pi * np.outer(np.arange(n_out), np.arange(n_in)) / period
    return np.cos(ang), np.sin(ang)


def _ada_kernel(c_ref, w_ref, b_ref, o_ref):
    c = c_ref[...]
    s = c * (1.0 / (1.0 + jnp.exp(-c)))
    sh = s.astype(BF16)
    sl = (s - sh.astype(F32)).astype(BF16)
    w = w_ref[...]
    wh = w.astype(BF16)
    wl = (w - wh.astype(F32)).astype(BF16)
    nrow = c.shape[0]
    both = jnp.dot(jnp.concatenate([sh, sl], axis=0), wh, preferred_element_type=F32)
    o_ref[...] = both[:nrow] + both[nrow:] + jnp.dot(sh, wl, preferred_element_type=F32) + b_ref[...]


def _ada(c_all, ada_w, ada_b):
    depth = ada_w.shape[0]
    nrow = c_all.shape[0]
    tn = 1024
    ncol = N_MOD * D_MODEL
    return pl.pallas_call(
        _ada_kernel,
        grid=(depth, ncol // tn),
        in_specs=[
            pl.BlockSpec((nrow, D_MODEL), lambda l, j: (0, 0)),
            pl.BlockSpec((None, D_MODEL, tn), lambda l, j: (l, 0, j)),
            pl.BlockSpec((None, 1, tn), lambda l, j: (l, 0, j)),
        ],
        out_specs=pl.BlockSpec((None, nrow, tn), lambda l, j: (l, 0, j)),
        out_shape=jax.ShapeDtypeStruct((depth, nrow, ncol), F32),
        compiler_params=_cparams(("arbitrary", "arbitrary")),
        name="ada",
    )(c_all, ada_w, ada_b)


KIN_TN = 512
KIN_QKV_TILES = 3 * NA_WIDTH // KIN_TN
KIN_FN_TILE = KIN_QKV_TILES
KIN_HY_TILES = 3 * HY_WIDTH // KIN_TN


KNORM_TM = 512


def _modnorm(x, g, shift, scale):
    ms = jnp.mean(x * x, axis=-1, keepdims=True)
    return x * lax.rsqrt(ms + EPS) * g * (1.0 + scale) + shift


def _knorm_kernel(x_ref, mod_ref, g_ref, o_ref):
    o_ref[...] = _modnorm(x_ref[...], g_ref[...], mod_ref[0:1, :], mod_ref[1:2, :]).astype(BF16)


def _knorm(x, mod, g, seq_len):
    t = x.shape[0]
    tm = KNORM_TM
    return pl.pallas_call(
        _knorm_kernel,
        grid=(t // tm,),
        in_specs=[
            pl.BlockSpec((tm, D_MODEL), lambda i: (i, 0)),
            pl.BlockSpec((None, N_MOD, D_MODEL), lambda i: ((i * tm) // seq_len, 0, 0)),
            pl.BlockSpec((1, D_MODEL), lambda i: (0, 0)),
        ],
        out_specs=pl.BlockSpec((tm, D_MODEL), lambda i: (i, 0)),
        out_shape=jax.ShapeDtypeStruct((t, D_MODEL), BF16),
        compiler_params=_cparams(("arbitrary",)),
        name="knorm",
    )(x, mod, g)


def _head_rms(acc, gain):
    outs = []
    for hh in range(KIN_TN // HEAD_DIM):
        a = acc[:, hh * HEAD_DIM:(hh + 1) * HEAD_DIM]
        outs.append(a * lax.rsqrt(jnp.mean(a * a, axis=-1, keepdims=True) + EPS) * gain)
    return jnp.concatenate(outs, axis=-1)


def _kin_kernel(h_ref, w_ref, qg_ref, kg_ref, oq_ref, ofn_ref, ohy_ref):
    j = pl.program_id(1)
    acc = jnp.dot(h_ref[...], w_ref[...], preferred_element_type=F32)
    n_head_tiles = NA_WIDTH // KIN_TN

    @pl.when(j < n_head_tiles)
    def _():
        oq_ref[...] = _head_rms(acc, qg_ref[...] * (HEAD_DIM ** -0.5 * LOG2E)).astype(BF16)

    @pl.when((j >= n_head_tiles) & (j < 2 * n_head_tiles))
    def _():
        oq_ref[...] = _head_rms(acc, kg_ref[...]).astype(BF16)

    @pl.when((j >= 2 * n_head_tiles) & (j < KIN_QKV_TILES))
    def _():
        oq_ref[...] = acc.astype(BF16)

    @pl.when(j == KIN_FN_TILE)
    def _():
        ofn_ref[...] = acc

    @pl.when(j > KIN_FN_TILE)
    def _():
        ohy_ref[...] = acc


def _kin(h, w, l, qg, kg):
    t = h.shape[0]
    tm = 1024
    nj = IN_PROJ // KIN_TN
    return pl.pallas_call(
        _kin_kernel,
        grid=(t // tm, nj),
        in_specs=[
            pl.BlockSpec((tm, D_MODEL), lambda i, j: (i, 0)),
            pl.BlockSpec((None, D_MODEL, KIN_TN), lambda i, j: (l, 0, j)),
            pl.BlockSpec((1, HEAD_DIM), lambda i, j: (0, 0)),
            pl.BlockSpec((1, HEAD_DIM), lambda i, j: (0, 0)),
        ],
        out_specs=[
            pl.BlockSpec((tm, KIN_TN), lambda i, j: (i, jnp.minimum(j, KIN_QKV_TILES - 1))),
            pl.BlockSpec((tm, KIN_TN), lambda i, j: (i, 0)),
            pl.BlockSpec((tm, KIN_TN), lambda i, j: (i, jnp.clip(j - KIN_FN_TILE - 1, 0, KIN_HY_TILES - 1))),
        ],
        out_shape=[
            jax.ShapeDtypeStruct((t, 3 * NA_WIDTH), BF16),
            jax.ShapeDtypeStruct((t, FN_WIDTH), F32),
            jax.ShapeDtypeStruct((t, 3 * HY_WIDTH), F32),
        ],
        compiler_params=_cparams(("arbitrary", "arbitrary")),
        name="kin",
    )(h, w, qg, kg)


ATT_R = 4
ATT_KW = ATT_R + NA_KH


def _attn_variant_tiles(rows, variant):
    if variant == 0:
        r0, ws = 0, 0
    elif variant == 1:
        r0 = ATT_R
        ws = r0 - NA_KH // 2
    else:
        r0, ws = rows - ATT_R, rows - ATT_KW
    tiles = []
    for a in range(ATT_R):
        r = r0 + a
        rs = min(max(r - NA_KH // 2, 0), rows - NA_KH)
        row = []
        for jp in range(ATT_KW // 2):
            kr = ws + 2 * jp
            valid_l = rs <= kr < rs + NA_KH
            valid_r = rs <= kr + 1 < rs + NA_KH
            d_left = kr - r + (NA_KH - 1)
            row.append((valid_l, valid_r, d_left))
        tiles.append(row)
    return tiles


def _attn_build_bias(rpb_ref, t2, bias, h, rows):
    n_rd = 2 * NA_KH - 1
    n_cd = 2 * NA_KW - 1
    cidx = lax.broadcasted_iota(jnp.int32, (GRID_W, LANES), 0)
    lane = lax.broadcasted_iota(jnp.int32, (GRID_W, LANES), 1)
    widx = lane & (GRID_W - 1)
    left = lane < GRID_W
    dcidx = jnp.clip(widx - cidx, -(NA_KW - 1), NA_KW - 1) + (NA_KW - 1)
    base = h * (n_rd * n_cd)
    for dp in range(n_rd + 1):
        acc = jnp.zeros((GRID_W, LANES), F32)
        for e in range(n_cd):
            vl = rpb_ref[base + (dp - 1) * n_cd + e] if dp - 1 >= 0 else 0.0
            vr = rpb_ref[base + dp * n_cd + e] if dp < n_rd else 0.0
            acc = jnp.where(dcidx == e, jnp.where(left, vl, vr), acc)
        t2[dp] = acc * LOG2E

    c_lo = jnp.clip(cidx - NA_KW // 2, 0, GRID_W - NA_KW)
    col_ok = (widx >= c_lo) & (widx < c_lo + NA_KW)
    for variant in range(3):
        tiles = _attn_variant_tiles(rows, variant)
        for a in range(ATT_R):
            for jp in range(ATT_KW // 2):
                valid_l, valid_r, d_left = tiles[a][jp]
                if not (valid_l or valid_r):
                    tile = jnp.full((GRID_W, LANES), NEG, F32)
                else:
                    ok = col_ok
                    if not valid_l:
                        ok = ok & (~left)
                    if not valid_r:
                        ok = ok & left
                    tile = jnp.where(ok, t2[d_left + 1], NEG)
                bias[variant, a * GRID_W:(a + 1) * GRID_W, jp * LANES:(jp + 1) * LANES] = tile


def _attn_kernel(rpb_ref, q_ref, k_ref, v_ref, o_ref, t2, bias, *, seq_len):
    rows = seq_len // GRID_W
    nb = rows // ATT_R

    @pl.when(pl.program_id(1) == 0)
    def _():
        _attn_build_bias(rpb_ref, t2, bias, pl.program_id(0), rows)

    def block(rb, carry):
        variant = jnp.where(rb == 0, 0, jnp.where(rb == nb - 1, 2, 1))
        ws = jnp.clip(rb * ATT_R - NA_KH // 2, 0, rows - ATT_KW)
        qsl = pl.ds(pl.multiple_of(rb * (ATT_R * GRID_W), ATT_R * GRID_W), ATT_R * GRID_W)
        ksl = pl.ds(pl.multiple_of(ws * GRID_W, GRID_W), ATT_KW * GRID_W)
        s = lax.dot_general(q_ref[qsl, :], k_ref[ksl, :], (((1,), (1,)), ((), ())), preferred_element_type=F32)
        s = s + bias[variant]
        m = jnp.max(s, axis=-1, keepdims=True)
        p = jnp.exp2(s - m)
        den = jnp.sum(p, axis=-1, keepdims=True)
        o = jnp.dot(p.astype(BF16), v_ref[ksl, :], preferred_element_type=F32)
        o_ref[qsl, :] = o * (1.0 / den)
        return carry

    lax.fori_loop(0, nb, block, 0, unroll=4)


def _attn(qkv, rpb_flat, bsz, seq_len):
    t = qkv.shape[0]
    kernel = functools.partial(_attn_kernel, seq_len=seq_len)
    return pl.pallas_call(
        kernel,
        grid=(NA_HEADS, bsz),
        in_specs=[
            pl.BlockSpec(memory_space=pltpu.SMEM),
            pl.BlockSpec((seq_len, HEAD_DIM), lambda h, b: (b, h)),
            pl.BlockSpec((seq_len, HEAD_DIM), lambda h, b: (b, NA_HEADS + h)),
            pl.BlockSpec((seq_len, HEAD_DIM), lambda h, b: (b, 2 * NA_HEADS + h)),
        ],
        out_specs=pl.BlockSpec((seq_len, HEAD_DIM), lambda h, b: (b, h)),
        out_shape=jax.ShapeDtypeStruct((t, NA_WIDTH), F32),
        scratch_shapes=[
            pltpu.VMEM((2 * NA_KH, GRID_W, LANES), F32),
            pltpu.VMEM((3, ATT_R * GRID_W, ATT_KW * GRID_W), F32),
        ],
        compiler_params=_cparams(("arbitrary", "arbitrary")),
        name="attn",
    )(rpb_flat, qkv, qkv, qkv)


DFT_TW = 8


def _kron_mat(m):
    eye = np.eye(DFT_TW, dtype=ml_dtypes.bfloat16)
    return jnp.asarray(np.kron(m.astype(ml_dtypes.bfloat16), eye))


def _dft1_kernel(m_ref, x_ref, *rest):
    o_ref = rest[-1]
    kd, rows, c = x_ref.shape
    mo = o_ref.shape[0]
    for s in range(rows // DFT_TW):
        sl = slice(s * DFT_TW, (s + 1) * DFT_TW)
        x = x_ref[:, sl, :].reshape(kd * DFT_TW, c)
        y = jnp.dot(m_ref[...], x.astype(BF16), preferred_element_type=F32).reshape(mo, DFT_TW, c)
        if len(rest) == 3:
            kh = mo // 2
            twr = _lane_tile(rest[0][:, sl, :], c)
            twi = _lane_tile(rest[1][:, sl, :], c)
            yr = y[:kh]
            yi = y[kh:]
            o_ref[:kh, sl, :] = yr * twr - yi * twi
            o_ref[kh:, sl, :] = yr * twi + yi * twr
        else:
            o_ref[:, sl, :] = y


def _dft_block_rows(mo, c):
    rows = DFT_TW
    while rows < 32 and mo * 2 * rows * c * 4 <= (8 << 20):
        rows *= 2
    return rows


def _dft1(mk, x, tw=None):
    bsz, kd, n2, c = x.shape
    mo = mk.shape[0] // DFT_TW
    rows = _dft_block_rows(mo, c)
    in_specs = [
        pl.BlockSpec(mk.shape, lambda b, j: (0, 0)),
        pl.BlockSpec((None, kd, rows, c), lambda b, j: (b, 0, j, 0)),
    ]
    args = [mk, x]
    if tw is not None:
        in_specs += [pl.BlockSpec((mo // 2, rows, LANES), lambda b, j: (0, j, 0))] * 2
        args += list(tw)
    return pl.pallas_call(
        _dft1_kernel,
        grid=(bsz, n2 // rows),
        in_specs=in_specs,
        out_specs=pl.BlockSpec((None, mo, rows, c), lambda b, j: (b, 0, j, 0)),
        out_shape=jax.ShapeDtypeStruct((bsz, mo, n2, c), F32),
        compiler_params=_cparams(("arbitrary", "arbitrary")),
        name="dft1",
    )(*args)


def _stage1_fwd_mat(n1, kd, scale=1.0, rows=None):
    c, s = _cos_sin(n1 if rows is None else rows, kd, n1)
    return _kron_mat(np.concatenate([c, -s], axis=0) * scale)


def _stage2_mats(n2):
    c, s = _cos_sin(n2, n2, n2)
    fwd = np.block([[c, s], [-s, c]])
    inv = np.block([[c, -s], [s, c]])
    return jnp.asarray(fwd.astype(ml_dtypes.bfloat16)), jnp.asarray(inv.astype(ml_dtypes.bfloat16))


def _half_rows(n1):
    return n1 // 2 + DFT_TW


def _twiddle(n1, n2, rows=None):
    rows = n1 if rows is None else rows
    ang = 2.0 * np.pi * np.outer(np.arange(rows), np.arange(n2)) / (n1 * n2)
    twr = jnp.asarray(np.cos(ang).astype(np.float32))
    twi = jnp.asarray((-np.sin(ang)).astype(np.float32))
    shape = (rows, n2, LANES)
    return jnp.broadcast_to(twr[:, :, None], shape), jnp.broadcast_to(twi[:, :, None], shape)


def _lane_tile(x, width):
    reps = width // x.shape[-1]
    return x if reps == 1 else jnp.concatenate([x] * reps, axis=-1)


FFT_KC = 8


def _fnw_kernel(cs_ref, w_ref, o_ref):
    o_ref[...] = jnp.dot(cs_ref[...], w_ref[...], precision=HIGHEST, preferred_element_type=F32)


def _fnw(fn_w):
    c, s = _cos_sin(FN_GROUP_DIM, FN_GROUP_DIM, FN_GROUP_DIM)
    cs = jnp.asarray(np.concatenate([c, s], axis=0).astype(np.float32))
    return pl.pallas_call(
        _fnw_kernel,
        grid=(FN_GROUPS,),
        in_specs=[
            pl.BlockSpec((2 * FN_GROUP_DIM, FN_GROUP_DIM), lambda g: (0, 0)),
            pl.BlockSpec((None, FN_GROUP_DIM, FN_GROUP_DIM), lambda g: (g, 0, 0)),
        ],
        out_specs=pl.BlockSpec((None, 2 * FN_GROUP_DIM, FN_GROUP_DIM), lambda g: (g, 0, 0)),
        out_shape=jax.ShapeDtypeStruct((FN_GROUPS, 2 * FN_GROUP_DIM, FN_GROUP_DIM), F32),
        compiler_params=_cparams(("arbitrary",)),
        name="fnw",
    )(cs, fn_w)


def _fn2_kernel(y_ref, f2_ref, g_ref, b_ref, o_ref):
    n2 = DFT_N2
    for t in range(FFT_KC):
        rhs = jnp.concatenate([y_ref[0, t], y_ref[1, t]], axis=0).astype(BF16)
        z = jnp.dot(f2_ref[...], rhs, preferred_element_type=F32)
        zr = z[:n2]
        zi = z[n2:]
        outs = []
        for g in range(FN_GROUPS):
            sl = slice(g * FN_GROUP_DIM, (g + 1) * FN_GROUP_DIM)
            zc = jnp.concatenate([zr[:, sl], zi[:, sl]], axis=1).astype(BF16)
            outs.append(jnp.dot(zc, g_ref[g].astype(BF16), preferred_element_type=F32))
        o_ref[:, t, :] = jnp.concatenate(outs, axis=1) + b_ref[...]


def _fourier(u, gcat, fn_b, bsz, seq_len):
    n2 = DFT_N2
    n1 = seq_len // n2
    m1 = _stage1_fwd_mat(n1, n1, scale=1.0 / math.sqrt(seq_len * FN_GROUP_DIM))
    y = _dft1(m1, u.reshape(bsz, n1, n2, FN_WIDTH), _twiddle(n1, n2))
    y = y.reshape(bsz, 2, n1, n2, FN_WIDTH)
    f2, _ = _stage2_mats(n2)
    out = pl.pallas_call(
        _fn2_kernel,
        grid=(n1 // FFT_KC, bsz),
        in_specs=[
            pl.BlockSpec((None, 2, FFT_KC, n2, FN_WIDTH), lambda k, b: (b, 0, k, 0, 0)),
            pl.BlockSpec(f2.shape, lambda k, b: (0, 0)),
            pl.BlockSpec((FN_GROUPS, 2 * FN_GROUP_DIM, FN_GROUP_DIM), lambda k, b: (0, 0, 0)),
            pl.BlockSpec((1, FN_WIDTH), lambda k, b: (0, 0)),
        ],
        out_specs=pl.BlockSpec((None, n2, FFT_KC, FN_WIDTH), lambda k, b: (b, 0, k, 0)),
        out_shape=jax.ShapeDtypeStruct((bsz, n2, n1, FN_WIDTH), F32),
        compiler_params=_cparams(("arbitrary", "arbitrary")),
        name="fn2",
    )(y, f2, gcat, fn_b)
    return out.reshape(bsz * seq_len, FN_WIDTH)


HPRE_TM = 512


def _hpre_kernel(up_ref, um_ref, un_ref, cw_ref, cb_ref, z_ref, x0_ref, *, seq_len):
    tm = HPRE_TM
    i = pl.program_id(0)
    at_start = (i * tm) % seq_len == 0
    at_end = ((i + 1) * tm) % seq_len == 0
    u = um_ref[...]
    r = lax.broadcasted_iota(jnp.int32, (tm, 1), 0)
    prev_row = jnp.where(at_start, 0.0, up_ref[7:8, :])
    next_row = jnp.where(at_end, 0.0, un_ref[0:1, :])
    upv = jnp.where(r == 0, prev_row, pltpu.roll(u, 1, 0))
    dnv = jnp.where(r == tm - 1, next_row, pltpu.roll(u, tm - 1, 0))
    y = upv * cw_ref[0:1, :] + u * cw_ref[1:2, :] + dnv * cw_ref[2:3, :] + cb_ref[...]
    z_ref[...] = y[:, 2 * HY_WIDTH:] * y[:, HY_WIDTH:2 * HY_WIDTH]
    x0_ref[...] = y[:, :HY_WIDTH]


def _hpre(u, cw, cb, seq_len):
    t = u.shape[0]
    tm = HPRE_TM
    w = 3 * HY_WIDTH
    nb8 = t // 8
    return pl.pallas_call(
        functools.partial(_hpre_kernel, seq_len=seq_len),
        grid=(t // tm,),
        in_specs=[
            pl.BlockSpec((8, w), lambda i: (jnp.maximum(i * (tm // 8) - 1, 0), 0)),
            pl.BlockSpec((tm, w), lambda i: (i, 0)),
            pl.BlockSpec((8, w), lambda i: (jnp.minimum((i + 1) * (tm // 8), nb8 - 1), 0)),
            pl.BlockSpec((3, w), lambda i: (0, 0)),
            pl.BlockSpec((1, w), lambda i: (0, 0)),
        ],
        out_specs=[
            pl.BlockSpec((tm, HY_WIDTH), lambda i: (i, 0)),
            pl.BlockSpec((tm, HY_WIDTH), lambda i: (i, 0)),
        ],
        out_shape=[jax.ShapeDtypeStruct((t, HY_WIDTH), F32), jax.ShapeDtypeStruct((t, HY_WIDTH), F32)],
        compiler_params=_cparams(("arbitrary",)),
        name="hpre",
    )(u, u, u, cw, cb)


HFILT_TL = 512
HY_EMB_PAD = 128


def _hfilt_kernel(z_ref, w1_ref, b1_ref, f1_ref, w2_ref, b2_ref, f2_ref, w3_ref, b3_ref, dl_ref, o_ref):
    z = z_ref[...]
    h = jnp.sin(f1_ref[...] * (jnp.dot(z, w1_ref[...], precision=HIGHEST, preferred_element_type=F32) + b1_ref[...]))
    h = jnp.sin(f2_ref[...] * (jnp.dot(h, w2_ref[...], precision=HIGHEST, preferred_element_type=F32) + b2_ref[...]))
    h = jnp.dot(h, w3_ref[...], precision=HIGHEST, preferred_element_type=F32) + b3_ref[...]
    t = z[:, 0:1]
    h = h * jnp.exp(-t * dl_ref[...])
    row = lax.broadcasted_iota(jnp.int32, h.shape, 0) + pl.program_id(0) * HFILT_TL
    col = lax.broadcasted_iota(jnp.int32, h.shape, 1)
    o_ref[...] = jnp.where((row == 0) & (col >= HY_WIDTH), 0.0, h)


def _hyena_emb_np(seq_len):
    t = np.linspace(0.0, 1.0, seq_len)[:, None]
    bands = (HY_EMB_DIM - 1) // 2
    w = 2.0 * np.pi * np.arange(seq_len)[:, None] / seq_len
    fr = np.linspace(1e-4, bands - 1, bands)[None, :]
    z = np.concatenate([t, np.cos(fr * w), -np.sin(fr * w)], axis=-1)
    out = np.zeros((seq_len, HY_EMB_PAD), np.float32)
    out[:, :HY_EMB_DIM] = z
    return out


def _hfilt(seq_len, w1, b1, f1, w2, b2, f2, w3, b3):
    zemb = jnp.asarray(_hyena_emb_np(seq_len))
    max_decay = math.log(HY_TARGET) / HY_FAST_DECAY
    min_decay = math.log(HY_TARGET) / HY_SLOW_DECAY
    deltas = np.abs(np.linspace(min_decay, max_decay, HY_WIDTH))
    dl = jnp.asarray(np.tile(deltas, 2)[None, :].astype(np.float32))
    w1p = jnp.pad(w1, ((0, HY_EMB_PAD - HY_EMB_DIM), (0, 0)))
    tl = HFILT_TL
    full = lambda shape: pl.BlockSpec(shape, lambda i: (0,) * len(shape))
    return pl.pallas_call(
        _hfilt_kernel,
        grid=(seq_len // tl,),
        in_specs=[
            pl.BlockSpec((tl, HY_EMB_PAD), lambda i: (i, 0)),
            full((HY_EMB_PAD, HY_HIDDEN)), full((1, HY_HIDDEN)), full((1, HY_HIDDEN)),
            full((HY_HIDDEN, HY_HIDDEN)), full((1, HY_HIDDEN)), full((1, HY_HIDDEN)),
            full((HY_HIDDEN, 2 * HY_WIDTH)), full((1, 2 * HY_WIDTH)), full((1, 2 * HY_WIDTH)),
        ],
        out_specs=pl.BlockSpec((tl, 2 * HY_WIDTH), lambda i: (i, 0)),
        out_shape=jax.ShapeDtypeStruct((seq_len, 2 * HY_WIDTH), F32),
        compiler_params=_cparams(("arbitrary",)),
        name="hfilt",
    )(zemb, w1p, b1, f1, w2, b2, f2, w3, b3, dl)


def _hk2_kernel(y_ref, f2_ref, o_ref):
    n2 = DFT_N2
    for t in range(FFT_KC):
        rhs = jnp.concatenate([y_ref[0, t], y_ref[1, t]], axis=0).astype(BF16)
        z = jnp.dot(f2_ref[...], rhs, preferred_element_type=F32)
        zr = z[:n2]
        zi = z[n2:]
        o_ref[0, t] = zr[:, :HY_WIDTH] + zr[:, HY_WIDTH:]
        o_ref[1, t] = zi[:, :HY_WIDTH] - zi[:, HY_WIDTH:]


def _hy2_kernel(y_ref, kf_ref, twr_ref, twi_ref, f2_ref, f2i_ref, o_ref):
    n2 = DFT_N2
    for t in range(FFT_KC):
        rhs = jnp.concatenate([y_ref[0, t], y_ref[1, t]], axis=0).astype(BF16)
        z = jnp.dot(f2_ref[...], rhs, preferred_element_type=F32)
        zr = z[:n2]
        zi = z[n2:]
        kr = kf_ref[0, t]
        ki = kf_ref[1, t]
        pr = zr * kr - zi * ki
        pi = zr * ki + zi * kr
        rhs = jnp.concatenate([pr, pi], axis=0).astype(BF16)
        g = jnp.dot(f2i_ref[...], rhs, preferred_element_type=F32)
        gr = g[:n2]
        gi = g[n2:]
        twr = _lane_tile(twr_ref[t], HY_WIDTH)
        twi = _lane_tile(twi_ref[t], HY_WIDTH)
        o_ref[0, t] = gr * twr + gi * twi
        o_ref[1, t] = gi * twr - gr * twi


def _hy3_kernel(m_ref, g_ref, z_ref, x0_ref, d_ref, o_ref):
    kd, rows, c = g_ref.shape
    mo = o_ref.shape[0]
    for s in range(rows // DFT_TW):
        sl = slice(s * DFT_TW, (s + 1) * DFT_TW)
        rhs = g_ref[:, sl, :].reshape(kd * DFT_TW, c).astype(BF16)
        conv = jnp.dot(m_ref[...], rhs, preferred_element_type=F32).reshape(mo, DFT_TW, c)
        o_ref[:, sl, :] = (conv + z_ref[:, sl, :] * d_ref[...]) * x0_ref[:, sl, :]


def _hyena_filter_spectrum(seq_len, w1, b1, f1, w2, b2, f2, w3, b3):
    n = 2 * seq_len
    n2 = DFT_N2
    n1 = n // n2
    kh = _half_rows(n1)
    h = _hfilt(seq_len, w1, b1, f1, w2, b2, f2, w3, b3)
    m1 = _stage1_fwd_mat(n1, n1 // 2, rows=kh)
    y = _dft1(m1, h.reshape(1, n1 // 2, n2, 2 * HY_WIDTH), _twiddle(n1, n2, kh))
    y = y.reshape(2, kh, n2, 2 * HY_WIDTH)
    f2m, _ = _stage2_mats(n2)
    return pl.pallas_call(
        _hk2_kernel,
        grid=(kh // FFT_KC,),
        in_specs=[
            pl.BlockSpec((2, FFT_KC, n2, 2 * HY_WIDTH), lambda k: (0, k, 0, 0)),
            pl.BlockSpec(f2m.shape, lambda k: (0, 0)),
        ],
        out_specs=pl.BlockSpec((2, FFT_KC, n2, HY_WIDTH), lambda k: (0, k, 0, 0)),
        out_shape=jax.ShapeDtypeStruct((2, kh, n2, HY_WIDTH), F32),
        compiler_params=_cparams(("arbitrary",)),
        name="hk2",
    )(y, f2m)


def _hyena_conv(z, x0, kf, d_skip, bsz, seq_len):
    n = 2 * seq_len
    n2 = DFT_N2
    n1 = n // n2
    kh = _half_rows(n1)
    zv = z.reshape(bsz, n1 // 2, n2, HY_WIDTH)
    tw = _twiddle(n1, n2, kh)
    y = _dft1(_stage1_fwd_mat(n1, n1 // 2, rows=kh), zv, tw).reshape(bsz, 2, kh, n2, HY_WIDTH)
    f2m, f2i = _stage2_mats(n2)
    g = pl.pallas_call(
        _hy2_kernel,
        grid=(kh // FFT_KC, bsz),
        in_specs=[
            pl.BlockSpec((None, 2, FFT_KC, n2, HY_WIDTH), lambda k, b: (b, 0, k, 0, 0)),
            pl.BlockSpec((2, FFT_KC, n2, HY_WIDTH), lambda k, b: (0, k, 0, 0)),
            pl.BlockSpec((FFT_KC, n2, LANES), lambda k, b: (k, 0, 0)),
            pl.BlockSpec((FFT_KC, n2, LANES), lambda k, b: (k, 0, 0)),
            pl.BlockSpec(f2m.shape, lambda k, b: (0, 0)),
            pl.BlockSpec(f2i.shape, lambda k, b: (0, 0)),
        ],
        out_specs=pl.BlockSpec((None, 2, FFT_KC, n2, HY_WIDTH), lambda k, b: (b, 0, k, 0, 0)),
        out_shape=jax.ShapeDtypeStruct((bsz, 2, kh, n2, HY_WIDTH), F32),
        compiler_params=_cparams(("arbitrary", "arbitrary")),
        name="hy2",
    )(y, kf, tw[0], tw[1], f2m, f2i)
    c, s = _cos_sin(n1 // 2, kh, n1)
    wt = np.where(np.arange(kh) < n1 // 2, 2.0, 0.0)
    wt[0] = 1.0
    wt[n1 // 2] = 1.0
    m3 = _kron_mat(np.concatenate([c * wt, -s * wt], axis=1) / n)
    rows = 2 * DFT_TW
    half = pl.BlockSpec((None, n1 // 2, rows, HY_WIDTH), lambda b, j: (b, 0, j, 0))
    out = pl.pallas_call(
        _hy3_kernel,
        grid=(bsz, n2 // rows),
        in_specs=[
            pl.BlockSpec(m3.shape, lambda b, j: (0, 0)),
            pl.BlockSpec((None, 2 * kh, rows, HY_WIDTH), lambda b, j: (b, 0, j, 0)),
            half,
            half,
            pl.BlockSpec((1, HY_WIDTH), lambda b, j: (0, 0)),
        ],
        out_specs=half,
        out_shape=jax.ShapeDtypeStruct((bsz, n1 // 2, n2, HY_WIDTH), F32),
        compiler_params=_cparams(("arbitrary", "arbitrary")),
        name="hy3",
    )(m3, g.reshape(bsz, 2 * kh, n2, HY_WIDTH), zv, x0.reshape(bsz, n1 // 2, n2, HY_WIDTH), d_skip)
    return out.reshape(bsz * seq_len, HY_WIDTH)


KOUT_TM = 256


def _rms(y):
    return y * lax.rsqrt(jnp.mean(y * y, axis=-1, keepdims=True) + EPS)


def _kout_kernel(ya_ref, yb_ref, yc_ref, x_ref, mod_ref, og_ref, n2g_ref, w_ref, xo_ref, h2_ref):
    n = jnp.concatenate([_rms(ya_ref[...]), _rms(yb_ref[...]), _rms(yc_ref[...])], axis=-1) * og_ref[...]
    y = jnp.dot(n.astype(BF16), w_ref[...], preferred_element_type=F32)
    xn = x_ref[...] + mod_ref[2:3, :] * y
    xo_ref[...] = xn
    h2 = _rms(xn) * n2g_ref[...]
    h2_ref[...] = (h2 * (1.0 + mod_ref[4:5, :]) + mod_ref[3:4, :]).astype(BF16)


def _kout(ya, yb, yc, x, mod, og, n2g, w, l, seq_len):
    t = x.shape[0]
    tm = KOUT_TM
    row = lambda width: pl.BlockSpec((tm, width), lambda i: (i, 0))
    return pl.pallas_call(
        _kout_kernel,
        grid=(t // tm,),
        in_specs=[
            row(NA_WIDTH), row(FN_WIDTH), row(HY_WIDTH), row(D_MODEL),
            pl.BlockSpec((None, N_MOD, D_MODEL), lambda i: ((i * tm) // seq_len, 0, 0)),
            pl.BlockSpec((1, D_MODEL), lambda i: (0, 0)),
            pl.BlockSpec((1, D_MODEL), lambda i: (0, 0)),
            pl.BlockSpec((None, D_MODEL, D_MODEL), lambda i: (l, 0, 0)),
        ],
        out_specs=[row(D_MODEL), row(D_MODEL)],
        out_shape=[jax.ShapeDtypeStruct((t, D_MODEL), F32), jax.ShapeDtypeStruct((t, D_MODEL), BF16)],
        compiler_params=_cparams(("arbitrary",)),
        name="kout",
    )(ya, yb, yc, x, mod, og, n2g, w)


MLP_TM = 512
MLP_TF = 512
MLP_HALO = 16


def _gelu_exact(a):
    return 0.5 * a * (1.0 + lax.erf(a * (1.0 / math.sqrt(2.0))))


def _mlp_kernel(hp_ref, hm_ref, hn_ref, x_ref, mod_ref, wa_ref, wg_ref, cw_ref, cb_ref, wd_ref, *rest,
                seq_len, emit_next):
    if emit_next:
        modn_ref, gn_ref, o_ref, hnext_ref, hext, acc = rest
    else:
        o_ref, hext, acc = rest
    tm = MLP_TM
    halo = MLP_HALO
    i = pl.program_id(0)
    j = pl.program_id(1)

    @pl.when(j == 0)
    def _():
        hext[0:halo, :] = hp_ref[...]
        hext[halo:halo + tm, :] = hm_ref[...]
        hext[halo + tm:, :] = hn_ref[...]
        acc[...] = jnp.zeros_like(acc)

    at_start = (i * tm) % seq_len == 0
    at_end = ((i + 1) * tm) % seq_len == 0
    a_ext = jnp.dot(hext[...], wa_ref[...], preferred_element_type=F32)
    up = pltpu.roll(a_ext, 1, 0)[halo:halo + tm]
    dn = pltpu.roll(a_ext, tm + 2 * halo - 1, 0)[halo:halo + tm]
    mid = a_ext[halo:halo + tm]
    r = lax.broadcasted_iota(jnp.int32, (tm, 1), 0)
    up = jnp.where((r == 0) & at_start, 0.0, up)
    dn = jnp.where((r == tm - 1) & at_end, 0.0, dn)
    a = up * cw_ref[0:1, :] + mid * cw_ref[1:2, :] + dn * cw_ref[2:3, :] + cb_ref[...]
    gate = jnp.dot(hext[halo:halo + tm, :], wg_ref[...], preferred_element_type=F32)
    act = (_gelu_exact(a) * gate).astype(BF16)
    acc[...] += jnp.dot(act, wd_ref[...], preferred_element_type=F32)

    @pl.when(j == pl.num_programs(1) - 1)
    def _():
        xo = x_ref[...] + mod_ref[5:6, :] * acc[...]
        o_ref[...] = xo
        if emit_next:
            hnext_ref[...] = _modnorm(xo, gn_ref[...], modn_ref[0:1, :], modn_ref[1:2, :]).astype(BF16)


def _mlp(h2, x, mod, w_up, cw, cb, w_down, l, seq_len, next_norm=None):
    t = x.shape[0]
    tm, tf, halo = MLP_TM, MLP_TF, MLP_HALO
    nf = D_FF // tf
    nbh = t // halo
    emit_next = next_norm is not None
    row = pl.BlockSpec((tm, D_MODEL), lambda i, j: (i, 0))
    modspec = pl.BlockSpec((None, N_MOD, D_MODEL), lambda i, j: ((i * tm) // seq_len, 0, 0))
    in_specs = [
        pl.BlockSpec((halo, D_MODEL), lambda i, j: (jnp.maximum(i * (tm // halo) - 1, 0), 0)),
        row,
        pl.BlockSpec((halo, D_MODEL), lambda i, j: (jnp.minimum((i + 1) * (tm // halo), nbh - 1), 0)),
        row,
        modspec,
        pl.BlockSpec((None, D_MODEL, tf), lambda i, j: (l, 0, j)),
        pl.BlockSpec((None, D_MODEL, tf), lambda i, j: (l, 0, nf + j)),
        pl.BlockSpec((3, tf), lambda i, j: (0, j)),
        pl.BlockSpec((1, tf), lambda i, j: (0, j)),
        pl.BlockSpec((None, tf, D_MODEL), lambda i, j: (l, j, 0)),
    ]
    args = [h2, h2, h2, x, mod, w_up, w_up, cw, cb, w_down]
    out_specs = [row]
    out_shape = [jax.ShapeDtypeStruct((t, D_MODEL), F32)]
    if emit_next:
        in_specs += [modspec, pl.BlockSpec((1, D_MODEL), lambda i, j: (0, 0))]
        args += list(next_norm)
        out_specs.append(row)
        out_shape.append(jax.ShapeDtypeStruct((t, D_MODEL), BF16))
    return pl.pallas_call(
        functools.partial(_mlp_kernel, seq_len=seq_len, emit_next=emit_next),
        grid=(t // tm, nf),
        in_specs=in_specs,
        out_specs=out_specs,
        out_shape=out_shape,
        scratch_shapes=[pltpu.VMEM((tm + 2 * halo, D_MODEL), BF16), pltpu.VMEM((tm, D_MODEL), F32)],
        compiler_params=_cparams(("arbitrary", "arbitrary")),
        name="mlp",
    )(*args)


def _layer(x, h, mod, p, l, bsz, seq_len, kf, next_norm):
    qkv, u_fn, u_hy = _kin(h, p["w_in"], l, p["q_norm_g"][l][None], p["k_norm_g"][l][None])
    ya = _attn(qkv, p["na_rpb"][l].reshape(-1), bsz, seq_len)
    yb = _fourier(u_fn, p["gcat"][l], p["fn_b"][l][None], bsz, seq_len)
    z, x0 = _hpre(u_hy, p["hy_conv_w"][l], p["hy_conv_b"][l][None], seq_len)
    yc = _hyena_conv(z, x0, kf, p["hy_d"][l][None], bsz, seq_len)
    x1, h2 = _kout(ya, yb, yc, x, mod, p["out_norm_g"][l][None], p["norm2_g"][l][None], p["w_out"], l, seq_len)
    out = _mlp(h2, x1, mod, p["mlp_w_up"], p["mlp_conv_w"][l], p["mlp_conv_b"][l][None], p["mlp_w_down"], l,
               seq_len, next_norm)
    return (out[0], out[1]) if next_norm is not None else (out[0], None)


def kernel(x_prompt, x_sample, c_prompt, c_sample, ada_w, ada_b, norm1_g, w_in, q_norm_g, k_norm_g, na_rpb, fn_w, fn_b, hy_conv_w, hy_conv_b, hy_w1, hy_b1, hy_f1, hy_w2, hy_b2, hy_f2, hy_w3, hy_b3, hy_d, out_norm_g, w_out, norm2_g, mlp_w_up, mlp_conv_w, mlp_conv_b, mlp_w_down):
    depth = ada_w.shape[0]
    groups = [(x_prompt, c_prompt), (x_sample, c_sample)]
    nseq = sum(c.shape[0] for _, c in groups)
    nrow = -(-nseq // 16) * 16
    c_all = jnp.concatenate([c for _, c in groups] + [jnp.zeros((nrow - nseq, D_MODEL), F32)], axis=0)
    mod_all = _ada(c_all, ada_w, ada_b[:, None, :]).reshape(depth, nrow, N_MOD, D_MODEL)

    p = dict(
        w_in=w_in.astype(BF16), q_norm_g=q_norm_g, k_norm_g=k_norm_g, na_rpb=na_rpb,
        fn_b=fn_b, hy_conv_w=hy_conv_w, hy_conv_b=hy_conv_b, hy_d=hy_d, out_norm_g=out_norm_g,
        w_out=w_out.astype(BF16), norm2_g=norm2_g, mlp_w_up=mlp_w_up.astype(BF16), mlp_conv_w=mlp_conv_w,
        mlp_conv_b=mlp_conv_b, mlp_w_down=mlp_w_down.astype(BF16),
        gcat=[_fnw(fn_w[l]) for l in range(depth)],
    )

    outs = []
    seq_off = 0
    kf_cache = {}
    for x, c in groups:
        bsz, seq_len, _ = x.shape
        xt = x.reshape(bsz * seq_len, D_MODEL)
        mods = [mod_all[l, seq_off:seq_off + bsz] for l in range(depth)]
        ht = _knorm(xt, mods[0], norm1_g[0][None], seq_len)
        for l in range(depth):
            if (l, seq_len) not in kf_cache:
                kf_cache[(l, seq_len)] = _hyena_filter_spectrum(
                    seq_len, hy_w1[l], hy_b1[l][None], hy_f1[l][None], hy_w2[l], hy_b2[l][None], hy_f2[l][None],
                    hy_w3[l], hy_b3[l][None])
            next_norm = (mods[l + 1], norm1_g[l + 1][None]) if l + 1 < depth else None
            xt, ht = _layer(xt, ht, mods[l], p, l, bsz, seq_len, kf_cache[(l, seq_len)], next_norm)
        outs.append(xt.reshape(bsz, seq_len, D_MODEL))
        seq_off += bsz
    return tuple(outs)
```

```python
import functools
import math

import ml_dtypes
import numpy as np
import jax
import jax.numpy as jnp
from jax import lax
from jax.experimental import pallas as pl
from jax.experimental.pallas import tpu as pltpu

F32 = jnp.float32
BF16 = jnp.bfloat16

D_MODEL = 2048
GRID_W = 64
HEAD_DIM = 128
NA_HEADS = 8
NA_WIDTH = NA_HEADS * HEAD_DIM
NA_KH = 8
NA_KW = 16
FN_GROUPS = 4
FN_GROUP_DIM = 128
FN_WIDTH = 512
HY_WIDTH = 512
HY_EMB_DIM = 33
HY_HIDDEN = 64
HY_FAST_DECAY = 0.3
HY_SLOW_DECAY = 1.5
HY_TARGET = 1e-2
IN_PROJ = 3 * NA_WIDTH + FN_WIDTH + 3 * HY_WIDTH
D_FF = 5632
N_MOD = 6
EPS = 1e-6
NEG = -1e30
LOG2E = math.log2(math.e)

LANES = 128
DFT_N2 = 128
VMEM_LIMIT = 56 << 20

HIGHEST = lax.Precision.HIGHEST


def _cparams(sem):
    return pltpu.CompilerParams(dimension_semantics=sem, vmem_limit_bytes=VMEM_LIMIT)


def _cos_sin(n_out, n_in, period):
    ang = 2.0 * np.pi * np.outer(np.arange(n_out), np.arange(n_in)) / period
    return np.cos(ang), np.sin(ang)


def _ada_kernel(c_ref, w_ref, b_ref, o_ref):
    c = c_ref[...]
    s = c * (1.0 / (1.0 + jnp.exp(-c)))
    sh = s.astype(BF16)
    sl = (s - sh.astype(F32)).astype(BF16)
    w = w_ref[...]
    wh = w.astype(BF16)
    wl = (w - wh.astype(F32)).astype(BF16)
    nrow = c.shape[0]
    both = jnp.dot(jnp.concatenate([sh, sl], axis=0), wh, preferred_element_type=F32)
    o_ref[...] = both[:nrow] + both[nrow:] + jnp.dot(sh, wl, preferred_element_type=F32) + b_ref[...]


def _ada(c_all, ada_w, ada_b):
    depth = ada_w.shape[0]
    nrow = c_all.shape[0]
    tn = 1024
    ncol = N_MOD * D_MODEL
    return pl.pallas_call(
        _ada_kernel,
        grid=(depth, ncol // tn),
        in_specs=[
            pl.BlockSpec((nrow, D_MODEL), lambda l, j: (0, 0)),
            pl.BlockSpec((None, D_MODEL, tn), lambda l, j: (l, 0, j)),
            pl.BlockSpec((None, 1, tn), lambda l, j: (l, 0, j)),
        ],
        out_specs=pl.BlockSpec((None, nrow, tn), lambda l, j: (l, 0, j)),
        out_shape=jax.ShapeDtypeStruct((depth, nrow, ncol), F32),
        compiler_params=_cparams(("arbitrary", "arbitrary")),
        name="ada",
    )(c_all, ada_w, ada_b)


KIN_TN = 512
KIN_QKV_TILES = 3 * NA_WIDTH // KIN_TN
KIN_FN_TILE = KIN_QKV_TILES
KIN_HY_TILES = 3 * HY_WIDTH // KIN_TN


KNORM_TM = 512


def _modnorm(x, g, shift, scale):
    ms = jnp.mean(x * x, axis=-1, keepdims=True)
    return x * lax.rsqrt(ms + EPS) * g * (1.0 + scale) + shift


def _knorm_kernel(x_ref, mod_ref, g_ref, o_ref):
    o_ref[...] = _modnorm(x_ref[...], g_ref[...], mod_ref[0:1, :], mod_ref[1:2, :]).astype(BF16)


def _knorm(x, mod, g, seq_len):
    t = x.shape[0]
    tm = KNORM_TM
    return pl.pallas_call(
        _knorm_kernel,
        grid=(t // tm,),
        in_specs=[
            pl.BlockSpec((tm, D_MODEL), lambda i: (i, 0)),
            pl.BlockSpec((None, N_MOD, D_MODEL), lambda i: ((i * tm) // seq_len, 0, 0)),
            pl.BlockSpec((1, D_MODEL), lambda i: (0, 0)),
        ],
        out_specs=pl.BlockSpec((tm, D_MODEL), lambda i: (i, 0)),
        out_shape=jax.ShapeDtypeStruct((t, D_MODEL), BF16),
        compiler_params=_cparams(("arbitrary",)),
        name="knorm",
    )(x, mod, g)


def _head_rms(acc, gain):
    outs = []
    for hh in range(KIN_TN // HEAD_DIM):
        a = acc[:, hh * HEAD_DIM:(hh + 1) * HEAD_DIM]
        outs.append(a * lax.rsqrt(jnp.mean(a * a, axis=-1, keepdims=True) + EPS) * gain)
    return jnp.concatenate(outs, axis=-1)


def _kin_kernel(h_ref, w_ref, qg_ref, kg_ref, oq_ref, ofn_ref, ohy_ref):
    j = pl.program_id(1)
    acc = jnp.dot(h_ref[...], w_ref[...], preferred_element_type=F32)
    n_head_tiles = NA_WIDTH // KIN_TN

    @pl.when(j < n_head_tiles)
    def _():
        oq_ref[...] = _head_rms(acc, qg_ref[...] * (HEAD_DIM ** -0.5 * LOG2E)).astype(BF16)

    @pl.when((j >= n_head_tiles) & (j < 2 * n_head_tiles))
    def _():
        oq_ref[...] = _head_rms(acc, kg_ref[...]).astype(BF16)

    @pl.when((j >= 2 * n_head_tiles) & (j < KIN_QKV_TILES))
    def _():
        oq_ref[...] = acc.astype(BF16)

    @pl.when(j == KIN_FN_TILE)
    def _():
        ofn_ref[...] = acc

    @pl.when(j > KIN_FN_TILE)
    def _():
        ohy_ref[...] = acc


def _kin(h, w, l, qg, kg):
    t = h.shape[0]
    tm = 1024
    nj = IN_PROJ // KIN_TN
    return pl.pallas_call(
        _kin_kernel,
        grid=(t // tm, nj),
        in_specs=[
            pl.BlockSpec((tm, D_MODEL), lambda i, j: (i, 0)),
            pl.BlockSpec((None, D_MODEL, KIN_TN), lambda i, j: (l, 0, j)),
            pl.BlockSpec((1, HEAD_DIM), lambda i, j: (0, 0)),
            pl.BlockSpec((1, HEAD_DIM), lambda i, j: (0, 0)),
        ],
        out_specs=[
            pl.BlockSpec((tm, KIN_TN), lambda i, j: (i, jnp.minimum(j, KIN_QKV_TILES - 1))),
            pl.BlockSpec((tm, KIN_TN), lambda i, j: (i, 0)),
            pl.BlockSpec((tm, KIN_TN), lambda i, j: (i, jnp.clip(j - KIN_FN_TILE - 1, 0, KIN_HY_TILES - 1))),
        ],
        out_shape=[
            jax.ShapeDtypeStruct((t, 3 * NA_WIDTH), BF16),
            jax.ShapeDtypeStruct((t, FN_WIDTH), F32),
            jax.ShapeDtypeStruct((t, 3 * HY_WIDTH), F32),
        ],
        compiler_params=_cparams(("arbitrary", "arbitrary")),
        name="kin",
    )(h, w, qg, kg)


ATT_R = 4
ATT_KW = ATT_R + NA_KH


def _attn_variant_tiles(rows, variant):
    if variant == 0:
        r0, ws = 0, 0
    elif variant == 1:
        r0 = ATT_R
        ws = r0 - NA_KH // 2
    else:
        r0, ws = rows - ATT_R, rows - ATT_KW
    tiles = []
    for a in range(ATT_R):
        r = r0 + a
        rs = min(max(r - NA_KH // 2, 0), rows - NA_KH)
        row = []
        for jp in range(ATT_KW // 2):
            kr = ws + 2 * jp
            valid_l = rs <= kr < rs + NA_KH
            valid_r = rs <= kr + 1 < rs + NA_KH
            d_left = kr - r + (NA_KH - 1)
            row.append((valid_l, valid_r, d_left))
        tiles.append(row)
    return tiles


def _attn_build_bias(rpb_ref, t2, bias, h, rows):
    n_rd = 2 * NA_KH - 1
    n_cd = 2 * NA_KW - 1
    cidx = lax.broadcasted_iota(jnp.int32, (GRID_W, LANES), 0)
    lane = lax.broadcasted_iota(jnp.int32, (GRID_W, LANES), 1)
    widx = lane & (GRID_W - 1)
    left = lane < GRID_W
    dcidx = jnp.clip(widx - cidx, -(NA_KW - 1), NA_KW - 1) + (NA_KW - 1)
    base = h * (n_rd * n_cd)
    for dp in range(n_rd + 1):
        acc = jnp.zeros((GRID_W, LANES), F32)
        for e in range(n_cd):
            vl = rpb_ref[base + (dp - 1) * n_cd + e] if dp - 1 >= 0 else 0.0
            vr = rpb_ref[base + dp * n_cd + e] if dp < n_rd else 0.0
            acc = jnp.where(dcidx == e, jnp.where(left, vl, vr), acc)
        t2[dp] = acc * LOG2E

    c_lo = jnp.clip(cidx - NA_KW // 2, 0, GRID_W - NA_KW)
    col_ok = (widx >= c_lo) & (widx < c_lo + NA_KW)
    for variant in range(3):
        tiles = _attn_variant_tiles(rows, variant)
        for a in range(ATT_R):
            for jp in range(ATT_KW // 2):
                valid_l, valid_r, d_left = tiles[a][jp]
                if not (valid_l or valid_r):
                    tile = jnp.full((GRID_W, LANES), NEG, F32)
                else:
                    ok = col_ok
                    if not valid_l:
                        ok = ok & (~left)
                    if not valid_r:
                        ok = ok & left
                    tile = jnp.where(ok, t2[d_left + 1], NEG)
                bias[variant, a * GRID_W:(a + 1) * GRID_W, jp * LANES:(jp + 1) * LANES] = tile


def _attn_kernel(rpb_ref, q_ref, k_ref, v_ref, o_ref, t2, bias, s_buf, p_buf, rden_buf, *, seq_len):
    rows = seq_len // GRID_W
    nb = rows // ATT_R

    @pl.when(pl.program_id(1) == 0)
    def _():
        _attn_build_bias(rpb_ref, t2, bias, pl.program_id(0), rows)

    def q_rows(rb):
        if isinstance(rb, int):
            return pl.ds(rb * (ATT_R * GRID_W), ATT_R * GRID_W)
        return pl.ds(pl.multiple_of(rb * (ATT_R * GRID_W), ATT_R * GRID_W), ATT_R * GRID_W)

    def k_rows(rb):
        if isinstance(rb, int):
            return pl.ds(min(max(rb * ATT_R - NA_KH // 2, 0), rows - ATT_KW) * GRID_W, ATT_KW * GRID_W)
        ws = jnp.clip(rb * ATT_R - NA_KH // 2, 0, rows - ATT_KW)
        return pl.ds(pl.multiple_of(ws * GRID_W, GRID_W), ATT_KW * GRID_W)

    def scores(rb, slot):
        if isinstance(rb, int):
            variant = 0 if rb == 0 else (2 if rb == nb - 1 else 1)
        else:
            variant = jnp.where(rb == 0, 0, jnp.where(rb == nb - 1, 2, 1))
        s = lax.dot_general(q_ref[q_rows(rb), :], k_ref[k_rows(rb), :], (((1,), (1,)), ((), ())),
                            preferred_element_type=F32)
        s_buf[slot] = s + bias[variant]

    def softmax(slot):
        s = s_buf[slot]
        m = jnp.max(s, axis=-1, keepdims=True)
        p = jnp.exp2(s - m)
        rden_buf[slot] = 1.0 / jnp.sum(p, axis=-1, keepdims=True)
        p_buf[slot] = p.astype(BF16)

    def values(rb, slot):
        o = jnp.dot(p_buf[slot], v_ref[k_rows(rb), :], preferred_element_type=F32)
        o_ref[q_rows(rb), :] = o * rden_buf[slot]

    scores(0, 0)
    scores(1, 1)
    softmax(0)

    def body(v, carry):
        t = 2 * v
        values(t, 0)
        softmax(1)
        scores(t + 2, 0)
        values(t + 1, 1)
        softmax(0)
        scores(t + 3, 1)
        return carry

    lax.fori_loop(0, nb // 2 - 1, body, 0)
    values(nb - 2, 0)
    softmax(1)
    values(nb - 1, 1)


def _attn(qkv, rpb_flat, bsz, seq_len):
    t = qkv.shape[0]
    kernel = functools.partial(_attn_kernel, seq_len=seq_len)
    return pl.pallas_call(
        kernel,
        grid=(NA_HEADS, bsz),
        in_specs=[
            pl.BlockSpec(memory_space=pltpu.SMEM),
            pl.BlockSpec((seq_len, HEAD_DIM), lambda h, b: (b, h)),
            pl.BlockSpec((seq_len, HEAD_DIM), lambda h, b: (b, NA_HEADS + h)),
            pl.BlockSpec((seq_len, HEAD_DIM), lambda h, b: (b, 2 * NA_HEADS + h)),
        ],
        out_specs=pl.BlockSpec((seq_len, HEAD_DIM), lambda h, b: (b, h)),
        out_shape=jax.ShapeDtypeStruct((t, NA_WIDTH), F32),
        scratch_shapes=[
            pltpu.VMEM((2 * NA_KH, GRID_W, LANES), F32),
            pltpu.VMEM((3, ATT_R * GRID_W, ATT_KW * GRID_W), F32),
            pltpu.VMEM((2, ATT_R * GRID_W, ATT_KW * GRID_W), F32),
            pltpu.VMEM((2, ATT_R * GRID_W, ATT_KW * GRID_W), BF16),
            pltpu.VMEM((2, ATT_R * GRID_W, 1), F32),
        ],
        compiler_params=_cparams(("arbitrary", "arbitrary")),
        name="attn",
    )(rpb_flat, qkv, qkv, qkv)


DFT_TW = 8


def _kron_mat(m):
    eye = np.eye(DFT_TW, dtype=ml_dtypes.bfloat16)
    return jnp.asarray(np.kron(m.astype(ml_dtypes.bfloat16), eye))


def _dft1_kernel(m_ref, x_ref, *rest):
    o_ref = rest[-1]
    kd, rows, c = x_ref.shape
    mo = o_ref.shape[0]
    for s in range(rows // DFT_TW):
        sl = slice(s * DFT_TW, (s + 1) * DFT_TW)
        x = x_ref[:, sl, :].reshape(kd * DFT_TW, c)
        y = jnp.dot(m_ref[...], x.astype(BF16), preferred_element_type=F32).reshape(mo, DFT_TW, c)
        if len(rest) == 3:
            kh = mo // 2
            twr = _lane_tile(rest[0][:, sl, :], c)
            twi = _lane_tile(rest[1][:, sl, :], c)
            yr = y[:kh]
            yi = y[kh:]
            o_ref[:kh, sl, :] = yr * twr - yi * twi
            o_ref[kh:, sl, :] = yr * twi + yi * twr
        else:
            o_ref[:, sl, :] = y


def _dft_block_rows(mo, c):
    rows = DFT_TW
    while rows < 32 and mo * 2 * rows * c * 4 <= (8 << 20):
        rows *= 2
    return rows


def _dft1(mk, x, tw=None):
    bsz, kd, n2, c = x.shape
    mo = mk.shape[0] // DFT_TW
    rows = _dft_block_rows(mo, c)
    in_specs = [
        pl.BlockSpec(mk.shape, lambda b, j: (0, 0)),
        pl.BlockSpec((None, kd, rows, c), lambda b, j: (b, 0, j, 0)),
    ]
    args = [mk, x]
    if tw is not None:
        in_specs += [pl.BlockSpec((mo // 2, rows, LANES), lambda b, j: (0, j, 0))] * 2
        args += list(tw)
    return pl.pallas_call(
        _dft1_kernel,
        grid=(bsz, n2 // rows),
        in_specs=in_specs,
        out_specs=pl.BlockSpec((None, mo, rows, c), lambda b, j: (b, 0, j, 0)),
        out_shape=jax.ShapeDtypeStruct((bsz, mo, n2, c), F32),
        compiler_params=_cparams(("arbitrary", "arbitrary")),
        name="dft1",
    )(*args)


def _stage1_fwd_mat(n1, kd, scale=1.0, rows=None):
    c, s = _cos_sin(n1 if rows is None else rows, kd, n1)
    return _kron_mat(np.concatenate([c, -s], axis=0) * scale)


def _stage2_mats(n2):
    c, s = _cos_sin(n2, n2, n2)
    fwd = np.block([[c, s], [-s, c]])
    inv = np.block([[c, -s], [s, c]])
    return jnp.asarray(fwd.astype(ml_dtypes.bfloat16)), jnp.asarray(inv.astype(ml_dtypes.bfloat16))


def _half_rows(n1):
    return n1 // 2 + DFT_TW


def _twiddle(n1, n2, rows=None):
    rows = n1 if rows is None else rows
    ang = 2.0 * np.pi * np.outer(np.arange(rows), np.arange(n2)) / (n1 * n2)
    twr = jnp.asarray(np.cos(ang).astype(np.float32))
    twi = jnp.asarray((-np.sin(ang)).astype(np.float32))
    shape = (rows, n2, LANES)
    return jnp.broadcast_to(twr[:, :, None], shape), jnp.broadcast_to(twi[:, :, None], shape)


def _lane_tile(x, width):
    reps = width // x.shape[-1]
    return x if reps == 1 else jnp.concatenate([x] * reps, axis=-1)


FFT_KC = 8


def _fnw_kernel(cs_ref, w_ref, o_ref):
    o_ref[...] = jnp.dot(cs_ref[...], w_ref[...], precision=HIGHEST, preferred_element_type=F32)


def _fnw(fn_w):
    c, s = _cos_sin(FN_GROUP_DIM, FN_GROUP_DIM, FN_GROUP_DIM)
    cs = jnp.asarray(np.concatenate([c, s], axis=0).astype(np.float32))
    return pl.pallas_call(
        _fnw_kernel,
        grid=(FN_GROUPS,),
        in_specs=[
            pl.BlockSpec((2 * FN_GROUP_DIM, FN_GROUP_DIM), lambda g: (0, 0)),
            pl.BlockSpec((None, FN_GROUP_DIM, FN_GROUP_DIM), lambda g: (g, 0, 0)),
        ],
        out_specs=pl.BlockSpec((None, 2 * FN_GROUP_DIM, FN_GROUP_DIM), lambda g: (g, 0, 0)),
        out_shape=jax.ShapeDtypeStruct((FN_GROUPS, 2 * FN_GROUP_DIM, FN_GROUP_DIM), F32),
        compiler_params=_cparams(("arbitrary",)),
        name="fnw",
    )(cs, fn_w)


def _fn2_kernel(y_ref, f2_ref, g_ref, b_ref, o_ref):
    n2 = DFT_N2
    for t in range(FFT_KC):
        rhs = jnp.concatenate([y_ref[0, t], y_ref[1, t]], axis=0).astype(BF16)
        z = jnp.dot(f2_ref[...], rhs, preferred_element_type=F32)
        zr = z[:n2]
        zi = z[n2:]
        outs = []
        for g in range(FN_GROUPS):
            sl = slice(g * FN_GROUP_DIM, (g + 1) * FN_GROUP_DIM)
            zc = jnp.concatenate([zr[:, sl], zi[:, sl]], axis=1).astype(BF16)
            outs.append(jnp.dot(zc, g_ref[g].astype(BF16), preferred_element_type=F32))
        o_ref[:, t, :] = jnp.concatenate(outs, axis=1) + b_ref[...]


def _fourier(u, gcat, fn_b, bsz, seq_len):
    n2 = DFT_N2
    n1 = seq_len // n2
    m1 = _stage1_fwd_mat(n1, n1, scale=1.0 / math.sqrt(seq_len * FN_GROUP_DIM))
    y = _dft1(m1, u.reshape(bsz, n1, n2, FN_WIDTH), _twiddle(n1, n2))
    y = y.reshape(bsz, 2, n1, n2, FN_WIDTH)
    f2, _ = _stage2_mats(n2)
    out = pl.pallas_call(
        _fn2_kernel,
        grid=(n1 // FFT_KC, bsz),
        in_specs=[
            pl.BlockSpec((None, 2, FFT_KC, n2, FN_WIDTH), lambda k, b: (b, 0, k, 0, 0)),
            pl.BlockSpec(f2.shape, lambda k, b: (0, 0)),
            pl.BlockSpec((FN_GROUPS, 2 * FN_GROUP_DIM, FN_GROUP_DIM), lambda k, b: (0, 0, 0)),
            pl.BlockSpec((1, FN_WIDTH), lambda k, b: (0, 0)),
        ],
        out_specs=pl.BlockSpec((None, n2, FFT_KC, FN_WIDTH), lambda k, b: (b, 0, k, 0)),
        out_shape=jax.ShapeDtypeStruct((bsz, n2, n1, FN_WIDTH), F32),
        compiler_params=_cparams(("arbitrary", "arbitrary")),
        name="fn2",
    )(y, f2, gcat, fn_b)
    return out.reshape(bsz * seq_len, FN_WIDTH)


HPRE_TM = 512


def _hpre_kernel(up_ref, um_ref, un_ref, cw_ref, cb_ref, z_ref, x0_ref, *, seq_len):
    tm = HPRE_TM
    i = pl.program_id(0)
    at_start = (i * tm) % seq_len == 0
    at_end = ((i + 1) * tm) % seq_len == 0
    u = um_ref[...]
    r = lax.broadcasted_iota(jnp.int32, (tm, 1), 0)
    prev_row = jnp.where(at_start, 0.0, up_ref[7:8, :])
    next_row = jnp.where(at_end, 0.0, un_ref[0:1, :])
    upv = jnp.where(r == 0, prev_row, pltpu.roll(u, 1, 0))
    dnv = jnp.where(r == tm - 1, next_row, pltpu.roll(u, tm - 1, 0))
    y = upv * cw_ref[0:1, :] + u * cw_ref[1:2, :] + dnv * cw_ref[2:3, :] + cb_ref[...]
    z_ref[...] = y[:, 2 * HY_WIDTH:] * y[:, HY_WIDTH:2 * HY_WIDTH]
    x0_ref[...] = y[:, :HY_WIDTH]


def _hpre(u, cw, cb, seq_len):
    t = u.shape[0]
    tm = HPRE_TM
    w = 3 * HY_WIDTH
    nb8 = t // 8
    return pl.pallas_call(
        functools.partial(_hpre_kernel, seq_len=seq_len),
        grid=(t // tm,),
        in_specs=[
            pl.BlockSpec((8, w), lambda i: (jnp.maximum(i * (tm // 8) - 1, 0), 0)),
            pl.BlockSpec((tm, w), lambda i: (i, 0)),
            pl.BlockSpec((8, w), lambda i: (jnp.minimum((i + 1) * (tm // 8), nb8 - 1), 0)),
            pl.BlockSpec((3, w), lambda i: (0, 0)),
            pl.BlockSpec((1, w), lambda i: (0, 0)),
        ],
        out_specs=[
            pl.BlockSpec((tm, HY_WIDTH), lambda i: (i, 0)),
            pl.BlockSpec((tm, HY_WIDTH), lambda i: (i, 0)),
        ],
        out_shape=[jax.ShapeDtypeStruct((t, HY_WIDTH), F32), jax.ShapeDtypeStruct((t, HY_WIDTH), F32)],
        compiler_params=_cparams(("arbitrary",)),
        name="hpre",
    )(u, u, u, cw, cb)


HFILT_TL = 512
HY_EMB_PAD = 128


def _hfilt_kernel(z_ref, w1_ref, b1_ref, f1_ref, w2_ref, b2_ref, f2_ref, w3_ref, b3_ref, dl_ref, o_ref):
    z = z_ref[...]
    h = jnp.sin(f1_ref[...] * (jnp.dot(z, w1_ref[...], precision=HIGHEST, preferred_element_type=F32) + b1_ref[...]))
    h = jnp.sin(f2_ref[...] * (jnp.dot(h, w2_ref[...], precision=HIGHEST, preferred_element_type=F32) + b2_ref[...]))
    h = jnp.dot(h, w3_ref[...], precision=HIGHEST, preferred_element_type=F32) + b3_ref[...]
    t = z[:, 0:1]
    h = h * jnp.exp(-t * dl_ref[...])
    row = lax.broadcasted_iota(jnp.int32, h.shape, 0) + pl.program_id(0) * HFILT_TL
    col = lax.broadcasted_iota(jnp.int32, h.shape, 1)
    o_ref[...] = jnp.where((row == 0) & (col >= HY_WIDTH), 0.0, h)


def _hyena_emb_np(seq_len):
    t = np.linspace(0.0, 1.0, seq_len)[:, None]
    bands = (HY_EMB_DIM - 1) // 2
    w = 2.0 * np.pi * np.arange(seq_len)[:, None] / seq_len
    fr = np.linspace(1e-4, bands - 1, bands)[None, :]
    z = np.concatenate([t, np.cos(fr * w), -np.sin(fr * w)], axis=-1)
    out = np.zeros((seq_len, HY_EMB_PAD), np.float32)
    out[:, :HY_EMB_DIM] = z
    return out


def _hfilt(seq_len, w1, b1, f1, w2, b2, f2, w3, b3):
    zemb = jnp.asarray(_hyena_emb_np(seq_len))
    max_decay = math.log(HY_TARGET) / HY_FAST_DECAY
    min_decay = math.log(HY_TARGET) / HY_SLOW_DECAY
    deltas = np.abs(np.linspace(min_decay, max_decay, HY_WIDTH))
    dl = jnp.asarray(np.tile(deltas, 2)[None, :].astype(np.float32))
    w1p = jnp.pad(w1, ((0, HY_EMB_PAD - HY_EMB_DIM), (0, 0)))
    tl = HFILT_TL
    full = lambda shape: pl.BlockSpec(shape, lambda i: (0,) * len(shape))
    return pl.pallas_call(
        _hfilt_kernel,
        grid=(seq_len // tl,),
        in_specs=[
            pl.BlockSpec((tl, HY_EMB_PAD), lambda i: (i, 0)),
            full((HY_EMB_PAD, HY_HIDDEN)), full((1, HY_HIDDEN)), full((1, HY_HIDDEN)),
            full((HY_HIDDEN, HY_HIDDEN)), full((1, HY_HIDDEN)), full((1, HY_HIDDEN)),
            full((HY_HIDDEN, 2 * HY_WIDTH)), full((1, 2 * HY_WIDTH)), full((1, 2 * HY_WIDTH)),
        ],
        out_specs=pl.BlockSpec((tl, 2 * HY_WIDTH), lambda i: (i, 0)),
        out_shape=jax.ShapeDtypeStruct((seq_len, 2 * HY_WIDTH), F32),
        compiler_params=_cparams(("arbitrary",)),
        name="hfilt",
    )(zemb, w1p, b1, f1, w2, b2, f2, w3, b3, dl)


def _hk2_kernel(y_ref, f2_ref, o_ref):
    n2 = DFT_N2
    for t in range(FFT_KC):
        rhs = jnp.concatenate([y_ref[0, t], y_ref[1, t]], axis=0).astype(BF16)
        z = jnp.dot(f2_ref[...], rhs, preferred_element_type=F32)
        zr = z[:n2]
        zi = z[n2:]
        o_ref[0, t] = zr[:, :HY_WIDTH] + zr[:, HY_WIDTH:]
        o_ref[1, t] = zi[:, :HY_WIDTH] - zi[:, HY_WIDTH:]


def _hy2_kernel(y_ref, kf_ref, twr_ref, twi_ref, f2_ref, f2i_ref, o_ref):
    n2 = DFT_N2
    for t in range(FFT_KC):
        rhs = jnp.concatenate([y_ref[0, t], y_ref[1, t]], axis=0).astype(BF16)
        z = jnp.dot(f2_ref[...], rhs, preferred_element_type=F32)
        zr = z[:n2]
        zi = z[n2:]
        kr = kf_ref[0, t]
        ki = kf_ref[1, t]
        pr = zr * kr - zi * ki
        pi = zr * ki + zi * kr
        rhs = jnp.concatenate([pr, pi], axis=0).astype(BF16)
        g = jnp.dot(f2i_ref[...], rhs, preferred_element_type=F32)
        gr = g[:n2]
        gi = g[n2:]
        twr = _lane_tile(twr_ref[t], HY_WIDTH)
        twi = _lane_tile(twi_ref[t], HY_WIDTH)
        o_ref[0, t] = gr * twr + gi * twi
        o_ref[1, t] = gi * twr - gr * twi


def _hy3_kernel(m_ref, g_ref, z_ref, x0_ref, d_ref, o_ref):
    kd, rows, c = g_ref.shape
    mo = o_ref.shape[0]
    for s in range(rows // DFT_TW):
        sl = slice(s * DFT_TW, (s + 1) * DFT_TW)
        rhs = g_ref[:, sl, :].reshape(kd * DFT_TW, c).astype(BF16)
        conv = jnp.dot(m_ref[...], rhs, preferred_element_type=F32).reshape(mo, DFT_TW, c)
        o_ref[:, sl, :] = (conv + z_ref[:, sl, :] * d_ref[...]) * x0_ref[:, sl, :]


def _hyena_filter_spectrum(seq_len, w1, b1, f1, w2, b2, f2, w3, b3):
    n = 2 * seq_len
    n2 = DFT_N2
    n1 = n // n2
    kh = _half_rows(n1)
    h = _hfilt(seq_len, w1, b1, f1, w2, b2, f2, w3, b3)
    m1 = _stage1_fwd_mat(n1, n1 // 2, rows=kh)
    y = _dft1(m1, h.reshape(1, n1 // 2, n2, 2 * HY_WIDTH), _twiddle(n1, n2, kh))
    y = y.reshape(2, kh, n2, 2 * HY_WIDTH)
    f2m, _ = _stage2_mats(n2)
    return pl.pallas_call(
        _hk2_kernel,
        grid=(kh // FFT_KC,),
        in_specs=[
            pl.BlockSpec((2, FFT_KC, n2, 2 * HY_WIDTH), lambda k: (0, k, 0, 0)),
            pl.BlockSpec(f2m.shape, lambda k: (0, 0)),
        ],
        out_specs=pl.BlockSpec((2, FFT_KC, n2, HY_WIDTH), lambda k: (0, k, 0, 0)),
        out_shape=jax.ShapeDtypeStruct((2, kh, n2, HY_WIDTH), F32),
        compiler_params=_cparams(("arbitrary",)),
        name="hk2",
    )(y, f2m)


def _hyena_conv(z, x0, kf, d_skip, bsz, seq_len):
    n = 2 * seq_len
    n2 = DFT_N2
    n1 = n // n2
    kh = _half_rows(n1)
    zv = z.reshape(bsz, n1 // 2, n2, HY_WIDTH)
    tw = _twiddle(n1, n2, kh)
    y = _dft1(_stage1_fwd_mat(n1, n1 // 2, rows=kh), zv, tw).reshape(bsz, 2, kh, n2, HY_WIDTH)
    f2m, f2i = _stage2_mats(n2)
    g = pl.pallas_call(
        _hy2_kernel,
        grid=(kh // FFT_KC, bsz),
        in_specs=[
            pl.BlockSpec((None, 2, FFT_KC, n2, HY_WIDTH), lambda k, b: (b, 0, k, 0, 0)),
            pl.BlockSpec((2, FFT_KC, n2, HY_WIDTH), lambda k, b: (0, k, 0, 0)),
            pl.BlockSpec((FFT_KC, n2, LANES), lambda k, b: (k, 0, 0)),
            pl.BlockSpec((FFT_KC, n2, LANES), lambda k, b: (k, 0, 0)),
            pl.BlockSpec(f2m.shape, lambda k, b: (0, 0)),
            pl.BlockSpec(f2i.shape, lambda k, b: (0, 0)),
        ],
        out_specs=pl.BlockSpec((None, 2, FFT_KC, n2, HY_WIDTH), lambda k, b: (b, 0, k, 0, 0)),
        out_shape=jax.ShapeDtypeStruct((bsz, 2, kh, n2, HY_WIDTH), F32),
        compiler_params=_cparams(("arbitrary", "arbitrary")),
        name="hy2",
    )(y, kf, tw[0], tw[1], f2m, f2i)
    c, s = _cos_sin(n1 // 2, kh, n1)
    wt = np.where(np.arange(kh) < n1 // 2, 2.0, 0.0)
    wt[0] = 1.0
    wt[n1 // 2] = 1.0
    m3 = _kron_mat(np.concatenate([c * wt, -s * wt], axis=1) / n)
    rows = 2 * DFT_TW
    half = pl.BlockSpec((None, n1 // 2, rows, HY_WIDTH), lambda b, j: (b, 0, j, 0))
    out = pl.pallas_call(
        _hy3_kernel,
        grid=(bsz, n2 // rows),
        in_specs=[
            pl.BlockSpec(m3.shape, lambda b, j: (0, 0)),
            pl.BlockSpec((None, 2 * kh, rows, HY_WIDTH), lambda b, j: (b, 0, j, 0)),
            half,
            half,
            pl.BlockSpec((1, HY_WIDTH), lambda b, j: (0, 0)),
        ],
        out_specs=half,
        out_shape=jax.ShapeDtypeStruct((bsz, n1 // 2, n2, HY_WIDTH), F32),
        compiler_params=_cparams(("arbitrary", "arbitrary")),
        name="hy3",
    )(m3, g.reshape(bsz, 2 * kh, n2, HY_WIDTH), zv, x0.reshape(bsz, n1 // 2, n2, HY_WIDTH), d_skip)
    return out.reshape(bsz * seq_len, HY_WIDTH)


KOUT_TM = 512
KOUT_SUB = 256


def _rms(y):
    return y * lax.rsqrt(jnp.mean(y * y, axis=-1, keepdims=True) + EPS)


def _kout_kernel(ya_ref, yb_ref, yc_ref, x_ref, mod_ref, og_ref, n2g_ref, w_ref, xo_ref, h2_ref):
    for s in range(KOUT_TM // KOUT_SUB):
        sl = slice(s * KOUT_SUB, (s + 1) * KOUT_SUB)
        n = jnp.concatenate([_rms(ya_ref[sl, :]), _rms(yb_ref[sl, :]), _rms(yc_ref[sl, :])], axis=-1) * og_ref[...]
        y = jnp.dot(n.astype(BF16), w_ref[...], preferred_element_type=F32)
        xn = x_ref[sl, :] + mod_ref[2:3, :] * y
        xo_ref[sl, :] = xn
        h2 = _rms(xn) * n2g_ref[...]
        h2_ref[sl, :] = (h2 * (1.0 + mod_ref[4:5, :]) + mod_ref[3:4, :]).astype(BF16)


def _kout(ya, yb, yc, x, mod, og, n2g, w, l, seq_len):
    t = x.shape[0]
    tm = KOUT_TM
    row = lambda width: pl.BlockSpec((tm, width), lambda i: (i, 0))
    return pl.pallas_call(
        _kout_kernel,
        grid=(t // tm,),
        in_specs=[
            row(NA_WIDTH), row(FN_WIDTH), row(HY_WIDTH), row(D_MODEL),
            pl.BlockSpec((None, N_MOD, D_MODEL), lambda i: ((i * tm) // seq_len, 0, 0)),
            pl.BlockSpec((1, D_MODEL), lambda i: (0, 0)),
            pl.BlockSpec((1, D_MODEL), lambda i: (0, 0)),
            pl.BlockSpec((None, D_MODEL, D_MODEL), lambda i: (l, 0, 0)),
        ],
        out_specs=[row(D_MODEL), row(D_MODEL)],
        out_shape=[jax.ShapeDtypeStruct((t, D_MODEL), F32), jax.ShapeDtypeStruct((t, D_MODEL), BF16)],
        compiler_params=_cparams(("arbitrary",)),
        name="kout",
    )(ya, yb, yc, x, mod, og, n2g, w)


MLP_TM = 512
MLP_TF = 512
MLP_HALO = 16


def _gelu_exact(a):
    return 0.5 * a * (1.0 + lax.erf(a * (1.0 / math.sqrt(2.0))))


def _mlp_kernel(hp_ref, hm_ref, hn_ref, x_ref, mod_ref, wa_ref, wg_ref, cw_ref, cb_ref, wd_ref, *rest,
                seq_len, emit_next):
    if emit_next:
        modn_ref, gn_ref, o_ref, hnext_ref, hext, acc = rest
    else:
        o_ref, hext, acc = rest
    tm = MLP_TM
    halo = MLP_HALO
    i = pl.program_id(0)
    j = pl.program_id(1)

    @pl.when(j == 0)
    def _():
        hext[0:halo, :] = hp_ref[...]
        hext[halo:halo + tm, :] = hm_ref[...]
        hext[halo + tm:, :] = hn_ref[...]
        acc[...] = jnp.zeros_like(acc)

    at_start = (i * tm) % seq_len == 0
    at_end = ((i + 1) * tm) % seq_len == 0
    a_ext = jnp.dot(hext[...], wa_ref[...], preferred_element_type=F32)
    up = pltpu.roll(a_ext, 1, 0)[halo:halo + tm]
    dn = pltpu.roll(a_ext, tm + 2 * halo - 1, 0)[halo:halo + tm]
    mid = a_ext[halo:halo + tm]
    r = lax.broadcasted_iota(jnp.int32, (tm, 1), 0)
    up = jnp.where((r == 0) & at_start, 0.0, up)
    dn = jnp.where((r == tm - 1) & at_end, 0.0, dn)
    a = up * cw_ref[0:1, :] + mid * cw_ref[1:2, :] + dn * cw_ref[2:3, :] + cb_ref[...]
    gate = jnp.dot(hext[halo:halo + tm, :], wg_ref[...], preferred_element_type=F32)
    act = (_gelu_exact(a) * gate).astype(BF16)
    acc[...] += jnp.dot(act, wd_ref[...], preferred_element_type=F32)

    @pl.when(j == pl.num_programs(1) - 1)
    def _():
        xo = x_ref[...] + mod_ref[5:6, :] * acc[...]
        o_ref[...] = xo
        if emit_next:
            hnext_ref[...] = _modnorm(xo, gn_ref[...], modn_ref[0:1, :], modn_ref[1:2, :]).astype(BF16)


def _mlp(h2, x, mod, w_up, cw, cb, w_down, l, seq_len, next_norm=None):
    t = x.shape[0]
    tm, tf, halo = MLP_TM, MLP_TF, MLP_HALO
    nf = D_FF // tf
    nbh = t // halo
    emit_next = next_norm is not None
    row = pl.BlockSpec((tm, D_MODEL), lambda i, j: (i, 0))
    modspec = pl.BlockSpec((None, N_MOD, D_MODEL), lambda i, j: ((i * tm) // seq_len, 0, 0))
    in_specs = [
        pl.BlockSpec((halo, D_MODEL), lambda i, j: (jnp.maximum(i * (tm // halo) - 1, 0), 0)),
        row,
        pl.BlockSpec((halo, D_MODEL), lambda i, j: (jnp.minimum((i + 1) * (tm // halo), nbh - 1), 0)),
        row,
        modspec,
        pl.BlockSpec((None, D_MODEL, tf), lambda i, j: (l, 0, j)),
        pl.BlockSpec((None, D_MODEL, tf), lambda i, j: (l, 0, nf + j)),
        pl.BlockSpec((3, tf), lambda i, j: (0, j)),
        pl.BlockSpec((1, tf), lambda i, j: (0, j)),
        pl.BlockSpec((None, tf, D_MODEL), lambda i, j: (l, j, 0)),
    ]
    args = [h2, h2, h2, x, mod, w_up, w_up, cw, cb, w_down]
    out_specs = [row]
    out_shape = [jax.ShapeDtypeStruct((t, D_MODEL), F32)]
    if emit_next:
        in_specs += [modspec, pl.BlockSpec((1, D_MODEL), lambda i, j: (0, 0))]
        args += list(next_norm)
        out_specs.append(row)
        out_shape.append(jax.ShapeDtypeStruct((t, D_MODEL), BF16))
    return pl.pallas_call(
        functools.partial(_mlp_kernel, seq_len=seq_len, emit_next=emit_next),
        grid=(t // tm, nf),
        in_specs=in_specs,
        out_specs=out_specs,
        out_shape=out_shape,
        scratch_shapes=[pltpu.VMEM((tm + 2 * halo, D_MODEL), BF16), pltpu.VMEM((tm, D_MODEL), F32)],
        compiler_params=_cparams(("arbitrary", "arbitrary")),
        name="mlp",
    )(*args)


def _layer(x, h, mod, p, l, bsz, seq_len, kf, next_norm):
    qkv, u_fn, u_hy = _kin(h, p["w_in"], l, p["q_norm_g"][l][None], p["k_norm_g"][l][None])
    ya = _attn(qkv, p["na_rpb"][l].reshape(-1), bsz, seq_len)
    yb = _fourier(u_fn, p["gcat"][l], p["fn_b"][l][None], bsz, seq_len)
    z, x0 = _hpre(u_hy, p["hy_conv_w"][l], p["hy_conv_b"][l][None], seq_len)
    yc = _hyena_conv(z, x0, kf, p["hy_d"][l][None], bsz, seq_len)
    x1, h2 = _kout(ya, yb, yc, x, mod, p["out_norm_g"][l][None], p["norm2_g"][l][None], p["w_out"], l, seq_len)
    out = _mlp(h2, x1, mod, p["mlp_w_up"], p["mlp_conv_w"][l], p["mlp_conv_b"][l][None], p["mlp_w_down"], l,
               seq_len, next_norm)
    return (out[0], out[1]) if next_norm is not None else (out[0], None)


def kernel(x_prompt, x_sample, c_prompt, c_sample, ada_w, ada_b, norm1_g, w_in, q_norm_g, k_norm_g, na_rpb, fn_w, fn_b, hy_conv_w, hy_conv_b, hy_w1, hy_b1, hy_f1, hy_w2, hy_b2, hy_f2, hy_w3, hy_b3, hy_d, out_norm_g, w_out, norm2_g, mlp_w_up, mlp_conv_w, mlp_conv_b, mlp_w_down):
    depth = ada_w.shape[0]
    groups = [(x_prompt, c_prompt), (x_sample, c_sample)]
    nseq = sum(c.shape[0] for _, c in groups)
    nrow = -(-nseq // 16) * 16
    c_all = jnp.concatenate([c for _, c in groups] + [jnp.zeros((nrow - nseq, D_MODEL), F32)], axis=0)
    mod_all = _ada(c_all, ada_w, ada_b[:, None, :]).reshape(depth, nrow, N_MOD, D_MODEL)

    p = dict(
        w_in=w_in.astype(BF16), q_norm_g=q_norm_g, k_norm_g=k_norm_g, na_rpb=na_rpb,
        fn_b=fn_b, hy_conv_w=hy_conv_w, hy_conv_b=hy_conv_b, hy_d=hy_d, out_norm_g=out_norm_g,
        w_out=w_out.astype(BF16), norm2_g=norm2_g, mlp_w_up=mlp_w_up.astype(BF16), mlp_conv_w=mlp_conv_w,
        mlp_conv_b=mlp_conv_b, mlp_w_down=mlp_w_down.astype(BF16),
        gcat=[_fnw(fn_w[l]) for l in range(depth)],
    )

    outs = []
    seq_off = 0
    kf_cache = {}
    for x, c in groups:
        bsz, seq_len, _ = x.shape
        xt = x.reshape(bsz * seq_len, D_MODEL)
        mods = [mod_all[l, seq_off:seq_off + bsz] for l in range(depth)]
        ht = _knorm(xt, mods[0], norm1_g[0][None], seq_len)
        for l in range(depth):
            if (l, seq_len) not in kf_cache:
                kf_cache[(l, seq_len)] = _hyena_filter_spectrum(
                    seq_len, hy_w1[l], hy_b1[l][None], hy_f1[l][None], hy_w2[l], hy_b2[l][None], hy_f2[l][None],
                    hy_w3[l], hy_b3[l][None])
            next_norm = (mods[l + 1], norm1_g[l + 1][None]) if l + 1 < depth else None
            xt, ht = _layer(xt, ht, mods[l], p, l, bsz, seq_len, kf_cache[(l, seq_len)], next_norm)
        outs.append(xt.reshape(bsz, seq_len, D_MODEL))
        seq_off += bsz
    return tuple(outs)
```

```python
import functools
import math

import ml_dtypes
import numpy as np
import jax
import jax.numpy as jnp
from jax import lax
from jax.experimental import pallas as pl
from jax.experimental.pallas import tpu as pltpu

F32 = jnp.float32
BF16 = jnp.bfloat16

D_MODEL = 2048
GRID_W = 64
HEAD_DIM = 128
NA_HEADS = 8
NA_WIDTH = NA_HEADS * HEAD_DIM
NA_KH = 8
NA_KW = 16
FN_GROUPS = 4
FN_GROUP_DIM = 128
FN_WIDTH = 512
HY_WIDTH = 512
HY_EMB_DIM = 33
HY_HIDDEN = 64
HY_FAST_DECAY = 0.3
HY_SLOW_DECAY = 1.5
HY_TARGET = 1e-2
IN_PROJ = 3 * NA_WIDTH + FN_WIDTH + 3 * HY_WIDTH
D_FF = 5632
N_MOD = 6
EPS = 1e-6
NEG = -1e30
LOG2E = math.log2(math.e)

LANES = 128
DFT_N2 = 128
VMEM_LIMIT = 56 << 20

HIGHEST = lax.Precision.HIGHEST


def _cparams(sem):
    return pltpu.CompilerParams(dimension_semantics=sem, vmem_limit_bytes=VMEM_LIMIT)


def _cos_sin(n_out, n_in, period):
    ang = 2.0 * np.pi * np.outer(np.arange(n_out), np.arange(n_in)) / period
    return np.cos(ang), np.sin(ang)


def _ada_kernel(c_ref, w_ref, b_ref, o_ref):
    c = c_ref[...]
    s = c * (1.0 / (1.0 + jnp.exp(-c)))
    sh = s.astype(BF16)
    sl = (s - sh.astype(F32)).astype(BF16)
    w = w_ref[...]
    wh = w.astype(BF16)
    wl = (w - wh.astype(F32)).astype(BF16)
    nrow = c.shape[0]
    both = jnp.dot(jnp.concatenate([sh, sl], axis=0), wh, preferred_element_type=F32)
    o_ref[...] = both[:nrow] + both[nrow:] + jnp.dot(sh, wl, preferred_element_type=F32) + b_ref[...]


def _ada(c_all, ada_w, ada_b):
    depth = ada_w.shape[0]
    nrow = c_all.shape[0]
    tn = 1024
    ncol = N_MOD * D_MODEL
    return pl.pallas_call(
        _ada_kernel,
        grid=(depth, ncol // tn),
        in_specs=[
            pl.BlockSpec((nrow, D_MODEL), lambda l, j: (0, 0)),
            pl.BlockSpec((None, D_MODEL, tn), lambda l, j: (l, 0, j)),
            pl.BlockSpec((None, 1, tn), lambda l, j: (l, 0, j)),
        ],
        out_specs=pl.BlockSpec((None, nrow, tn), lambda l, j: (l, 0, j)),
        out_shape=jax.ShapeDtypeStruct((depth, nrow, ncol), F32),
        compiler_params=_cparams(("arbitrary", "arbitrary")),
        name="ada",
    )(c_all, ada_w, ada_b)


KIN_TN = 512
KIN_Q_TILES = NA_WIDTH // KIN_TN
KIN_QK_TILES = 2 * KIN_Q_TILES
KIN_QKV_TILES = 3 * NA_WIDTH // KIN_TN
KIN_FN_TILE = KIN_QKV_TILES
KIN_HY_TILES = 3 * HY_WIDTH // KIN_TN


KNORM_TM = 512


def _modnorm(x, g, shift, scale):
    ms = jnp.mean(x * x, axis=-1, keepdims=True)
    return x * lax.rsqrt(ms + EPS) * g * (1.0 + scale) + shift


def _knorm_kernel(x_ref, mod_ref, g_ref, o_ref):
    o_ref[...] = _modnorm(x_ref[...], g_ref[...], mod_ref[0:1, :], mod_ref[1:2, :]).astype(BF16)


def _knorm(x, mod, g, seq_len):
    t = x.shape[0]
    tm = KNORM_TM
    return pl.pallas_call(
        _knorm_kernel,
        grid=(t // tm,),
        in_specs=[
            pl.BlockSpec((tm, D_MODEL), lambda i: (i, 0)),
            pl.BlockSpec((None, N_MOD, D_MODEL), lambda i: ((i * tm) // seq_len, 0, 0)),
            pl.BlockSpec((1, D_MODEL), lambda i: (0, 0)),
        ],
        out_specs=pl.BlockSpec((tm, D_MODEL), lambda i: (i, 0)),
        out_shape=jax.ShapeDtypeStruct((t, D_MODEL), BF16),
        compiler_params=_cparams(("arbitrary",)),
        name="knorm",
    )(x, mod, g)


def _head_rms(acc, gain):
    outs = []
    for hh in range(KIN_TN // HEAD_DIM):
        a = acc[:, hh * HEAD_DIM:(hh + 1) * HEAD_DIM]
        outs.append(a * lax.rsqrt(jnp.mean(a * a, axis=-1, keepdims=True) + EPS) * gain)
    return jnp.concatenate(outs, axis=-1)


def _kin_kernel(h_ref, w_ref, qg_ref, kg_ref, oqk_ref, ov_ref, ofn_ref, ohy_ref, raw):
    i = pl.program_id(0)
    j = pl.program_id(1)

    @pl.when((i == 0) & (j == 0))
    def _():
        raw[1] = jnp.zeros(raw.shape[1:], F32)

    gain = jnp.where(j - 1 < KIN_Q_TILES, qg_ref[...] * (HEAD_DIM ** -0.5 * LOG2E), kg_ref[...])

    def step(wslot, rslot):
        acc = jnp.dot(h_ref[...], w_ref[...], preferred_element_type=F32)
        oqk_ref[...] = _head_rms(raw[rslot], gain).astype(BF16)
        raw[wslot] = acc

        @pl.when((j >= KIN_QK_TILES) & (j < KIN_QKV_TILES))
        def _():
            ov_ref[...] = acc.astype(BF16)

        @pl.when(j == KIN_FN_TILE)
        def _():
            ofn_ref[...] = acc

        @pl.when(j > KIN_FN_TILE)
        def _():
            ohy_ref[...] = acc

    @pl.when(j % 2 == 0)
    def _():
        step(0, 1)

    @pl.when(j % 2 == 1)
    def _():
        step(1, 0)


def _kin(h, w, l, qg, kg):
    t = h.shape[0]
    tm = 1024
    nj = IN_PROJ // KIN_TN
    return pl.pallas_call(
        _kin_kernel,
        grid=(t // tm, nj),
        in_specs=[
            pl.BlockSpec((tm, D_MODEL), lambda i, j: (i, 0)),
            pl.BlockSpec((None, D_MODEL, KIN_TN), lambda i, j: (l, 0, j)),
            pl.BlockSpec((1, HEAD_DIM), lambda i, j: (0, 0)),
            pl.BlockSpec((1, HEAD_DIM), lambda i, j: (0, 0)),
        ],
        out_specs=[
            pl.BlockSpec((tm, KIN_TN), lambda i, j: (i, jnp.clip(j - 1, 0, KIN_QK_TILES))),
            pl.BlockSpec((tm, KIN_TN), lambda i, j: (i, jnp.clip(j - KIN_QK_TILES, 0, KIN_Q_TILES - 1))),
            pl.BlockSpec((tm, KIN_TN), lambda i, j: (i, 0)),
            pl.BlockSpec((tm, KIN_TN), lambda i, j: (i, jnp.clip(j - KIN_FN_TILE - 1, 0, KIN_HY_TILES - 1))),
        ],
        out_shape=[
            jax.ShapeDtypeStruct((t, (KIN_QK_TILES + 1) * KIN_TN), BF16),
            jax.ShapeDtypeStruct((t, NA_WIDTH), BF16),
            jax.ShapeDtypeStruct((t, FN_WIDTH), F32),
            jax.ShapeDtypeStruct((t, 3 * HY_WIDTH), F32),
        ],
        scratch_shapes=[pltpu.VMEM((2, tm, KIN_TN), F32)],
        compiler_params=_cparams(("arbitrary", "arbitrary")),
        name="kin",
    )(h, w, qg, kg)


ATT_R = 4
ATT_KW = ATT_R + NA_KH


def _attn_variant_tiles(rows, variant):
    if variant == 0:
        r0, ws = 0, 0
    elif variant == 1:
        r0 = ATT_R
        ws = r0 - NA_KH // 2
    else:
        r0, ws = rows - ATT_R, rows - ATT_KW
    tiles = []
    for a in range(ATT_R):
        r = r0 + a
        rs = min(max(r - NA_KH // 2, 0), rows - NA_KH)
        row = []
        for jp in range(ATT_KW // 2):
            kr = ws + 2 * jp
            valid_l = rs <= kr < rs + NA_KH
            valid_r = rs <= kr + 1 < rs + NA_KH
            d_left = kr - r + (NA_KH - 1)
            row.append((valid_l, valid_r, d_left))
        tiles.append(row)
    return tiles


def _attn_build_bias(rpb_ref, t2, bias, h, rows):
    n_rd = 2 * NA_KH - 1
    n_cd = 2 * NA_KW - 1
    cidx = lax.broadcasted_iota(jnp.int32, (GRID_W, LANES), 0)
    lane = lax.broadcasted_iota(jnp.int32, (GRID_W, LANES), 1)
    widx = lane & (GRID_W - 1)
    left = lane < GRID_W
    dcidx = jnp.clip(widx - cidx, -(NA_KW - 1), NA_KW - 1) + (NA_KW - 1)
    base = h * (n_rd * n_cd)
    for dp in range(n_rd + 1):
        acc = jnp.zeros((GRID_W, LANES), F32)
        for e in range(n_cd):
            vl = rpb_ref[base + (dp - 1) * n_cd + e] if dp - 1 >= 0 else 0.0
            vr = rpb_ref[base + dp * n_cd + e] if dp < n_rd else 0.0
            acc = jnp.where(dcidx == e, jnp.where(left, vl, vr), acc)
        t2[dp] = acc * LOG2E

    c_lo = jnp.clip(cidx - NA_KW // 2, 0, GRID_W - NA_KW)
    col_ok = (widx >= c_lo) & (widx < c_lo + NA_KW)
    for variant in range(3):
        tiles = _attn_variant_tiles(rows, variant)
        for a in range(ATT_R):
            for jp in range(ATT_KW // 2):
                valid_l, valid_r, d_left = tiles[a][jp]
                if not (valid_l or valid_r):
                    tile = jnp.full((GRID_W, LANES), NEG, F32)
                else:
                    ok = col_ok
                    if not valid_l:
                        ok = ok & (~left)
                    if not valid_r:
                        ok = ok & left
                    tile = jnp.where(ok, t2[d_left + 1], NEG)
                bias[variant, a * GRID_W:(a + 1) * GRID_W, jp * LANES:(jp + 1) * LANES] = tile


def _attn_kernel(rpb_ref, q_ref, k_ref, v_ref, o_ref, t2, bias, s_buf, p_buf, rden_buf, *, seq_len):
    rows = seq_len // GRID_W
    nb = rows // ATT_R

    @pl.when(pl.program_id(1) == 0)
    def _():
        _attn_build_bias(rpb_ref, t2, bias, pl.program_id(0), rows)

    def q_rows(rb):
        if isinstance(rb, int):
            return pl.ds(rb * (ATT_R * GRID_W), ATT_R * GRID_W)
        return pl.ds(pl.multiple_of(rb * (ATT_R * GRID_W), ATT_R * GRID_W), ATT_R * GRID_W)

    def k_rows(rb):
        if isinstance(rb, int):
            return pl.ds(min(max(rb * ATT_R - NA_KH // 2, 0), rows - ATT_KW) * GRID_W, ATT_KW * GRID_W)
        ws = jnp.clip(rb * ATT_R - NA_KH // 2, 0, rows - ATT_KW)
        return pl.ds(pl.multiple_of(ws * GRID_W, GRID_W), ATT_KW * GRID_W)

    def scores(rb, slot):
        if isinstance(rb, int):
            variant = 0 if rb == 0 else (2 if rb == nb - 1 else 1)
        else:
            variant = jnp.where(rb == 0, 0, jnp.where(rb == nb - 1, 2, 1))
        s = lax.dot_general(q_ref[q_rows(rb), :], k_ref[k_rows(rb), :], (((1,), (1,)), ((), ())),
                            preferred_element_type=F32)
        s_buf[slot] = s + bias[variant]

    def softmax(slot):
        s = s_buf[slot]
        m = jnp.max(s, axis=-1, keepdims=True)
        p = jnp.exp2(s - m)
        rden_buf[slot] = 1.0 / jnp.sum(p, axis=-1, keepdims=True)
        p_buf[slot] = p.astype(BF16)

    def values(rb, slot):
        o = jnp.dot(p_buf[slot], v_ref[k_rows(rb), :], preferred_element_type=F32)
        o_ref[q_rows(rb), :] = o * rden_buf[slot]

    scores(0, 0)
    scores(1, 1)
    softmax(0)

    def body(v, carry):
        t = 2 * v
        values(t, 0)
        softmax(1)
        scores(t + 2, 0)
        values(t + 1, 1)
        softmax(0)
        scores(t + 3, 1)
        return carry

    lax.fori_loop(0, nb // 2 - 1, body, 0)
    values(nb - 2, 0)
    softmax(1)
    values(nb - 1, 1)


def _attn(qk, v, rpb_flat, bsz, seq_len):
    t = qk.shape[0]
    kernel = functools.partial(_attn_kernel, seq_len=seq_len)
    return pl.pallas_call(
        kernel,
        grid=(NA_HEADS, bsz),
        in_specs=[
            pl.BlockSpec(memory_space=pltpu.SMEM),
            pl.BlockSpec((seq_len, HEAD_DIM), lambda h, b: (b, h)),
            pl.BlockSpec((seq_len, HEAD_DIM), lambda h, b: (b, NA_HEADS + h)),
            pl.BlockSpec((seq_len, HEAD_DIM), lambda h, b: (b, h)),
        ],
        out_specs=pl.BlockSpec((seq_len, HEAD_DIM), lambda h, b: (b, h)),
        out_shape=jax.ShapeDtypeStruct((t, NA_WIDTH), F32),
        scratch_shapes=[
            pltpu.VMEM((2 * NA_KH, GRID_W, LANES), F32),
            pltpu.VMEM((3, ATT_R * GRID_W, ATT_KW * GRID_W), F32),
            pltpu.VMEM((2, ATT_R * GRID_W, ATT_KW * GRID_W), F32),
            pltpu.VMEM((2, ATT_R * GRID_W, ATT_KW * GRID_W), BF16),
            pltpu.VMEM((2, ATT_R * GRID_W, 1), F32),
        ],
        compiler_params=_cparams(("arbitrary", "arbitrary")),
        name="attn",
    )(rpb_flat, qk, qk, v)


DFT_TW = 8


def _kron_mat(m):
    eye = np.eye(DFT_TW, dtype=ml_dtypes.bfloat16)
    return jnp.asarray(np.kron(m.astype(ml_dtypes.bfloat16), eye))


def _dft1_kernel(m_ref, x_ref, *rest):
    o_ref = rest[-1]
    kd, rows, c = x_ref.shape
    mo = o_ref.shape[0]
    for s in range(rows // DFT_TW):
        sl = slice(s * DFT_TW, (s + 1) * DFT_TW)
        x = x_ref[:, sl, :].reshape(kd * DFT_TW, c)
        y = jnp.dot(m_ref[...], x.astype(BF16), preferred_element_type=F32).reshape(mo, DFT_TW, c)
        if len(rest) == 3:
            kh = mo // 2
            twr = _lane_tile(rest[0][:, sl, :], c)
            twi = _lane_tile(rest[1][:, sl, :], c)
            yr = y[:kh]
            yi = y[kh:]
            o_ref[:kh, sl, :] = yr * twr - yi * twi
            o_ref[kh:, sl, :] = yr * twi + yi * twr
        else:
            o_ref[:, sl, :] = y


def _dft_block_rows(mo, c):
    rows = DFT_TW
    while rows < 32 and mo * 2 * rows * c * 4 <= (8 << 20):
        rows *= 2
    return rows


def _dft1(mk, x, tw=None):
    bsz, kd, n2, c = x.shape
    mo = mk.shape[0] // DFT_TW
    rows = _dft_block_rows(mo, c)
    in_specs = [
        pl.BlockSpec(mk.shape, lambda b, j: (0, 0)),
        pl.BlockSpec((None, kd, rows, c), lambda b, j: (b, 0, j, 0)),
    ]
    args = [mk, x]
    if tw is not None:
        in_specs += [pl.BlockSpec((mo // 2, rows, LANES), lambda b, j: (0, j, 0))] * 2
        args += list(tw)
    return pl.pallas_call(
        _dft1_kernel,
        grid=(bsz, n2 // rows),
        in_specs=in_specs,
        out_specs=pl.BlockSpec((None, mo, rows, c), lambda b, j: (b, 0, j, 0)),
        out_shape=jax.ShapeDtypeStruct((bsz, mo, n2, c), F32),
        compiler_params=_cparams(("arbitrary", "arbitrary")),
        name="dft1",
    )(*args)


def _stage1_fwd_mat(n1, kd, scale=1.0, rows=None):
    c, s = _cos_sin(n1 if rows is None else rows, kd, n1)
    return _kron_mat(np.concatenate([c, -s], axis=0) * scale)


def _stage2_mats(n2):
    c, s = _cos_sin(n2, n2, n2)
    fwd = np.block([[c, s], [-s, c]])
    inv = np.block([[c, -s], [s, c]])
    return jnp.asarray(fwd.astype(ml_dtypes.bfloat16)), jnp.asarray(inv.astype(ml_dtypes.bfloat16))


def _half_rows(n1):
    return n1 // 2 + DFT_TW


def _twiddle(n1, n2, rows=None):
    rows = n1 if rows is None else rows
    ang = 2.0 * np.pi * np.outer(np.arange(rows), np.arange(n2)) / (n1 * n2)
    twr = jnp.asarray(np.cos(ang).astype(np.float32))
    twi = jnp.asarray((-np.sin(ang)).astype(np.float32))
    shape = (rows, n2, LANES)
    return jnp.broadcast_to(twr[:, :, None], shape), jnp.broadcast_to(twi[:, :, None], shape)


def _lane_tile(x, width):
    reps = width // x.shape[-1]
    return x if reps == 1 else jnp.concatenate([x] * reps, axis=-1)


FFT_KC = 8


def _fnw_kernel(cs_ref, w_ref, o_ref):
    o_ref[...] = jnp.dot(cs_ref[...], w_ref[...], precision=HIGHEST, preferred_element_type=F32)


def _fnw(fn_w):
    c, s = _cos_sin(FN_GROUP_DIM, FN_GROUP_DIM, FN_GROUP_DIM)
    cs = jnp.asarray(np.concatenate([c, s], axis=0).astype(np.float32))
    return pl.pallas_call(
        _fnw_kernel,
        grid=(FN_GROUPS,),
        in_specs=[
            pl.BlockSpec((2 * FN_GROUP_DIM, FN_GROUP_DIM), lambda g: (0, 0)),
            pl.BlockSpec((None, FN_GROUP_DIM, FN_GROUP_DIM), lambda g: (g, 0, 0)),
        ],
        out_specs=pl.BlockSpec((None, 2 * FN_GROUP_DIM, FN_GROUP_DIM), lambda g: (g, 0, 0)),
        out_shape=jax.ShapeDtypeStruct((FN_GROUPS, 2 * FN_GROUP_DIM, FN_GROUP_DIM), F32),
        compiler_params=_cparams(("arbitrary",)),
        name="fnw",
    )(cs, fn_w)


def _fn2_kernel(y_ref, f2_ref, g_ref, b_ref, o_ref):
    n2 = DFT_N2
    for t in range(FFT_KC):
        rhs = jnp.concatenate([y_ref[0, t], y_ref[1, t]], axis=0).astype(BF16)
        z = jnp.dot(f2_ref[...], rhs, preferred_element_type=F32)
        zr = z[:n2]
        zi = z[n2:]
        outs = []
        for g in range(FN_GROUPS):
            sl = slice(g * FN_GROUP_DIM, (g + 1) * FN_GROUP_DIM)
            zc = jnp.concatenate([zr[:, sl], zi[:, sl]], axis=1).astype(BF16)
            outs.append(jnp.dot(zc, g_ref[g].astype(BF16), preferred_element_type=F32))
        o_ref[:, t, :] = jnp.concatenate(outs, axis=1) + b_ref[...]


def _fourier(u, gcat, fn_b, bsz, seq_len):
    n2 = DFT_N2
    n1 = seq_len // n2
    m1 = _stage1_fwd_mat(n1, n1, scale=1.0 / math.sqrt(seq_len * FN_GROUP_DIM))
    y = _dft1(m1, u.reshape(bsz, n1, n2, FN_WIDTH), _twiddle(n1, n2))
    y = y.reshape(bsz, 2, n1, n2, FN_WIDTH)
    f2, _ = _stage2_mats(n2)
    out = pl.pallas_call(
        _fn2_kernel,
        grid=(n1 // FFT_KC, bsz),
        in_specs=[
            pl.BlockSpec((None, 2, FFT_KC, n2, FN_WIDTH), lambda k, b: (b, 0, k, 0, 0)),
            pl.BlockSpec(f2.shape, lambda k, b: (0, 0)),
            pl.BlockSpec((FN_GROUPS, 2 * FN_GROUP_DIM, FN_GROUP_DIM), lambda k, b: (0, 0, 0)),
            pl.BlockSpec((1, FN_WIDTH), lambda k, b: (0, 0)),
        ],
        out_specs=pl.BlockSpec((None, n2, FFT_KC, FN_WIDTH), lambda k, b: (b, 0, k, 0)),
        out_shape=jax.ShapeDtypeStruct((bsz, n2, n1, FN_WIDTH), F32),
        compiler_params=_cparams(("arbitrary", "arbitrary")),
        name="fn2",
    )(y, f2, gcat, fn_b)
    return out.reshape(bsz * seq_len, FN_WIDTH)


HPRE_TM = 512


def _hpre_kernel(up_ref, um_ref, un_ref, cw_ref, cb_ref, z_ref, x0_ref, *, seq_len):
    tm = HPRE_TM
    i = pl.program_id(0)
    at_start = (i * tm) % seq_len == 0
    at_end = ((i + 1) * tm) % seq_len == 0
    u = um_ref[...]
    r = lax.broadcasted_iota(jnp.int32, (tm, 1), 0)
    prev_row = jnp.where(at_start, 0.0, up_ref[7:8, :])
    next_row = jnp.where(at_end, 0.0, un_ref[0:1, :])
    upv = jnp.where(r == 0, prev_row, pltpu.roll(u, 1, 0))
    dnv = jnp.where(r == tm - 1, next_row, pltpu.roll(u, tm - 1, 0))
    y = upv * cw_ref[0:1, :] + u * cw_ref[1:2, :] + dnv * cw_ref[2:3, :] + cb_ref[...]
    z_ref[...] = y[:, 2 * HY_WIDTH:] * y[:, HY_WIDTH:2 * HY_WIDTH]
    x0_ref[...] = y[:, :HY_WIDTH]


def _hpre(u, cw, cb, seq_len):
    t = u.shape[0]
    tm = HPRE_TM
    w = 3 * HY_WIDTH
    nb8 = t // 8
    return pl.pallas_call(
        functools.partial(_hpre_kernel, seq_len=seq_len),
        grid=(t // tm,),
        in_specs=[
            pl.BlockSpec((8, w), lambda i: (jnp.maximum(i * (tm // 8) - 1, 0), 0)),
            pl.BlockSpec((tm, w), lambda i: (i, 0)),
            pl.BlockSpec((8, w), lambda i: (jnp.minimum((i + 1) * (tm // 8), nb8 - 1), 0)),
            pl.BlockSpec((3, w), lambda i: (0, 0)),
            pl.BlockSpec((1, w), lambda i: (0, 0)),
        ],
        out_specs=[
            pl.BlockSpec((tm, HY_WIDTH), lambda i: (i, 0)),
            pl.BlockSpec((tm, HY_WIDTH), lambda i: (i, 0)),
        ],
        out_shape=[jax.ShapeDtypeStruct((t, HY_WIDTH), F32), jax.ShapeDtypeStruct((t, HY_WIDTH), F32)],
        compiler_params=_cparams(("arbitrary",)),
        name="hpre",
    )(u, u, u, cw, cb)


HFILT_TL = 512
HY_EMB_PAD = 128


def _hfilt_kernel(z_ref, w1_ref, b1_ref, f1_ref, w2_ref, b2_ref, f2_ref, w3_ref, b3_ref, dl_ref, o_ref):
    z = z_ref[...]
    h = jnp.sin(f1_ref[...] * (jnp.dot(z, w1_ref[...], precision=HIGHEST, preferred_element_type=F32) + b1_ref[...]))
    h = jnp.sin(f2_ref[...] * (jnp.dot(h, w2_ref[...], precision=HIGHEST, preferred_element_type=F32) + b2_ref[...]))
    h = jnp.dot(h, w3_ref[...], precision=HIGHEST, preferred_element_type=F32) + b3_ref[...]
    t = z[:, 0:1]
    h = h * jnp.exp(-t * dl_ref[...])
    row = lax.broadcasted_iota(jnp.int32, h.shape, 0) + pl.program_id(0) * HFILT_TL
    col = lax.broadcasted_iota(jnp.int32, h.shape, 1)
    o_ref[...] = jnp.where((row == 0) & (col >= HY_WIDTH), 0.0, h)


def _hyena_emb_np(seq_len):
    t = np.linspace(0.0, 1.0, seq_len)[:, None]
    bands = (HY_EMB_DIM - 1) // 2
    w = 2.0 * np.pi * np.arange(seq_len)[:, None] / seq_len
    fr = np.linspace(1e-4, bands - 1, bands)[None, :]
    z = np.concatenate([t, np.cos(fr * w), -np.sin(fr * w)], axis=-1)
    out = np.zeros((seq_len, HY_EMB_PAD), np.float32)
    out[:, :HY_EMB_DIM] = z
    return out


def _hfilt(seq_len, w1, b1, f1, w2, b2, f2, w3, b3):
    zemb = jnp.asarray(_hyena_emb_np(seq_len))
    max_decay = math.log(HY_TARGET) / HY_FAST_DECAY
    min_decay = math.log(HY_TARGET) / HY_SLOW_DECAY
    deltas = np.abs(np.linspace(min_decay, max_decay, HY_WIDTH))
    dl = jnp.asarray(np.tile(deltas, 2)[None, :].astype(np.float32))
    w1p = jnp.pad(w1, ((0, HY_EMB_PAD - HY_EMB_DIM), (0, 0)))
    tl = HFILT_TL
    full = lambda shape: pl.BlockSpec(shape, lambda i: (0,) * len(shape))
    return pl.pallas_call(
        _hfilt_kernel,
        grid=(seq_len // tl,),
        in_specs=[
            pl.BlockSpec((tl, HY_EMB_PAD), lambda i: (i, 0)),
            full((HY_EMB_PAD, HY_HIDDEN)), full((1, HY_HIDDEN)), full((1, HY_HIDDEN)),
            full((HY_HIDDEN, HY_HIDDEN)), full((1, HY_HIDDEN)), full((1, HY_HIDDEN)),
            full((HY_HIDDEN, 2 * HY_WIDTH)), full((1, 2 * HY_WIDTH)), full((1, 2 * HY_WIDTH)),
        ],
        out_specs=pl.BlockSpec((tl, 2 * HY_WIDTH), lambda i: (i, 0)),
        out_shape=jax.ShapeDtypeStruct((seq_len, 2 * HY_WIDTH), F32),
        compiler_params=_cparams(("arbitrary",)),
        name="hfilt",
    )(zemb, w1p, b1, f1, w2, b2, f2, w3, b3, dl)


def _hk2_kernel(y_ref, f2_ref, o_ref):
    n2 = DFT_N2
    for t in range(FFT_KC):
        rhs = jnp.concatenate([y_ref[0, t], y_ref[1, t]], axis=0).astype(BF16)
        z = jnp.dot(f2_ref[...], rhs, preferred_element_type=F32)
        zr = z[:n2]
        zi = z[n2:]
        o_ref[0, t] = zr[:, :HY_WIDTH] + zr[:, HY_WIDTH:]
        o_ref[1, t] = zi[:, :HY_WIDTH] - zi[:, HY_WIDTH:]


def _hy2_kernel(y_ref, kf_ref, twr_ref, twi_ref, f2_ref, f2i_ref, o_ref):
    n2 = DFT_N2
    for t in range(FFT_KC):
        rhs = jnp.concatenate([y_ref[0, t], y_ref[1, t]], axis=0).astype(BF16)
        z = jnp.dot(f2_ref[...], rhs, preferred_element_type=F32)
        zr = z[:n2]
        zi = z[n2:]
        kr = kf_ref[0, t]
        ki = kf_ref[1, t]
        pr = zr * kr - zi * ki
        pi = zr * ki + zi * kr
        rhs = jnp.concatenate([pr, pi], axis=0).astype(BF16)
        g = jnp.dot(f2i_ref[...], rhs, preferred_element_type=F32)
        gr = g[:n2]
        gi = g[n2:]
        twr = _lane_tile(twr_ref[t], HY_WIDTH)
        twi = _lane_tile(twi_ref[t], HY_WIDTH)
        o_ref[0, t] = gr * twr + gi * twi
        o_ref[1, t] = gi * twr - gr * twi


def _hy3_kernel(m_ref, g_ref, z_ref, x0_ref, d_ref, o_ref):
    kd, rows, c = g_ref.shape
    mo = o_ref.shape[0]
    for s in range(rows // DFT_TW):
        sl = slice(s * DFT_TW, (s + 1) * DFT_TW)
        rhs = g_ref[:, sl, :].reshape(kd * DFT_TW, c).astype(BF16)
        conv = jnp.dot(m_ref[...], rhs, preferred_element_type=F32).reshape(mo, DFT_TW, c)
        o_ref[:, sl, :] = (conv + z_ref[:, sl, :] * d_ref[...]) * x0_ref[:, sl, :]


def _hyena_filter_spectrum(seq_len, w1, b1, f1, w2, b2, f2, w3, b3):
    n = 2 * seq_len
    n2 = DFT_N2
    n1 = n // n2
    kh = _half_rows(n1)
    h = _hfilt(seq_len, w1, b1, f1, w2, b2, f2, w3, b3)
    m1 = _stage1_fwd_mat(n1, n1 // 2, rows=kh)
    y = _dft1(m1, h.reshape(1, n1 // 2, n2, 2 * HY_WIDTH), _twiddle(n1, n2, kh))
    y = y.reshape(2, kh, n2, 2 * HY_WIDTH)
    f2m, _ = _stage2_mats(n2)
    return pl.pallas_call(
        _hk2_kernel,
        grid=(kh // FFT_KC,),
        in_specs=[
            pl.BlockSpec((2, FFT_KC, n2, 2 * HY_WIDTH), lambda k: (0, k, 0, 0)),
            pl.BlockSpec(f2m.shape, lambda k: (0, 0)),
        ],
        out_specs=pl.BlockSpec((2, FFT_KC, n2, HY_WIDTH), lambda k: (0, k, 0, 0)),
        out_shape=jax.ShapeDtypeStruct((2, kh, n2, HY_WIDTH), F32),
        compiler_params=_cparams(("arbitrary",)),
        name="hk2",
    )(y, f2m)


def _hyena_conv(z, x0, kf, d_skip, bsz, seq_len):
    n = 2 * seq_len
    n2 = DFT_N2
    n1 = n // n2
    kh = _half_rows(n1)
    zv = z.reshape(bsz, n1 // 2, n2, HY_WIDTH)
    tw = _twiddle(n1, n2, kh)
    y = _dft1(_stage1_fwd_mat(n1, n1 // 2, rows=kh), zv, tw).reshape(bsz, 2, kh, n2, HY_WIDTH)
    f2m, f2i = _stage2_mats(n2)
    g = pl.pallas_call(
        _hy2_kernel,
        grid=(kh // FFT_KC, bsz),
        in_specs=[
            pl.BlockSpec((None, 2, FFT_KC, n2, HY_WIDTH), lambda k, b: (b, 0, k, 0, 0)),
            pl.BlockSpec((2, FFT_KC, n2, HY_WIDTH), lambda k, b: (0, k, 0, 0)),
            pl.BlockSpec((FFT_KC, n2, LANES), lambda k, b: (k, 0, 0)),
            pl.BlockSpec((FFT_KC, n2, LANES), lambda k, b: (k, 0, 0)),
            pl.BlockSpec(f2m.shape, lambda k, b: (0, 0)),
            pl.BlockSpec(f2i.shape, lambda k, b: (0, 0)),
        ],
        out_specs=pl.BlockSpec((None, 2, FFT_KC, n2, HY_WIDTH), lambda k, b: (b, 0, k, 0, 0)),
        out_shape=jax.ShapeDtypeStruct((bsz, 2, kh, n2, HY_WIDTH), F32),
        compiler_params=_cparams(("arbitrary", "arbitrary")),
        name="hy2",
    )(y, kf, tw[0], tw[1], f2m, f2i)
    c, s = _cos_sin(n1 // 2, kh, n1)
    wt = np.where(np.arange(kh) < n1 // 2, 2.0, 0.0)
    wt[0] = 1.0
    wt[n1 // 2] = 1.0
    m3 = _kron_mat(np.concatenate([c * wt, -s * wt], axis=1) / n)
    rows = 2 * DFT_TW
    half = pl.BlockSpec((None, n1 // 2, rows, HY_WIDTH), lambda b, j: (b, 0, j, 0))
    out = pl.pallas_call(
        _hy3_kernel,
        grid=(bsz, n2 // rows),
        in_specs=[
            pl.BlockSpec(m3.shape, lambda b, j: (0, 0)),
            pl.BlockSpec((None, 2 * kh, rows, HY_WIDTH), lambda b, j: (b, 0, j, 0)),
            half,
            half,
            pl.BlockSpec((1, HY_WIDTH), lambda b, j: (0, 0)),
        ],
        out_specs=half,
        out_shape=jax.ShapeDtypeStruct((bsz, n1 // 2, n2, HY_WIDTH), F32),
        compiler_params=_cparams(("arbitrary", "arbitrary")),
        name="hy3",
    )(m3, g.reshape(bsz, 2 * kh, n2, HY_WIDTH), zv, x0.reshape(bsz, n1 // 2, n2, HY_WIDTH), d_skip)
    return out.reshape(bsz * seq_len, HY_WIDTH)


KOUT_TM = 512
KOUT_SUB = 256


def _rms(y):
    return y * lax.rsqrt(jnp.mean(y * y, axis=-1, keepdims=True) + EPS)


def _kout_kernel(ya_ref, yb_ref, yc_ref, x_ref, mod_ref, og_ref, n2g_ref, w_ref, xo_ref, h2_ref):
    for s in range(KOUT_TM // KOUT_SUB):
        sl = slice(s * KOUT_SUB, (s + 1) * KOUT_SUB)
        n = jnp.concatenate([_rms(ya_ref[sl, :]), _rms(yb_ref[sl, :]), _rms(yc_ref[sl, :])], axis=-1) * og_ref[...]
        y = jnp.dot(n.astype(BF16), w_ref[...], preferred_element_type=F32)
        xn = x_ref[sl, :] + mod_ref[2:3, :] * y
        xo_ref[sl, :] = xn
        h2 = _rms(xn) * n2g_ref[...]
        h2_ref[sl, :] = (h2 * (1.0 + mod_ref[4:5, :]) + mod_ref[3:4, :]).astype(BF16)


def _kout(ya, yb, yc, x, mod, og, n2g, w, l, seq_len):
    t = x.shape[0]
    tm = KOUT_TM
    row = lambda width: pl.BlockSpec((tm, width), lambda i: (i, 0))
    return pl.pallas_call(
        _kout_kernel,
        grid=(t // tm,),
        in_specs=[
            row(NA_WIDTH), row(FN_WIDTH), row(HY_WIDTH), row(D_MODEL),
            pl.BlockSpec((None, N_MOD, D_MODEL), lambda i: ((i * tm) // seq_len, 0, 0)),
            pl.BlockSpec((1, D_MODEL), lambda i: (0, 0)),
            pl.BlockSpec((1, D_MODEL), lambda i: (0, 0)),
            pl.BlockSpec((None, D_MODEL, D_MODEL), lambda i: (l, 0, 0)),
        ],
        out_specs=[row(D_MODEL), row(D_MODEL)],
        out_shape=[jax.ShapeDtypeStruct((t, D_MODEL), F32), jax.ShapeDtypeStruct((t, D_MODEL), BF16)],
        compiler_params=_cparams(("arbitrary",)),
        name="kout",
    )(ya, yb, yc, x, mod, og, n2g, w)


MLP_TM = 512
MLP_TF = 512
MLP_HALO = 16


def _gelu_exact(a):
    return 0.5 * a * (1.0 + lax.erf(a * (1.0 / math.sqrt(2.0))))


def _mlp_kernel(hp_ref, hm_ref, hn_ref, x_ref, mod_ref, wa_ref, wg_ref, cw_ref, cb_ref, wd_ref, *rest,
                seq_len, emit_next):
    if emit_next:
        modn_ref, gn_ref, o_ref, hnext_ref, hext, acc = rest
    else:
        o_ref, hext, acc = rest
    tm = MLP_TM
    halo = MLP_HALO
    i = pl.program_id(0)
    j = pl.program_id(1)

    @pl.when(j == 0)
    def _():
        hext[0:halo, :] = hp_ref[...]
        hext[halo:halo + tm, :] = hm_ref[...]
        hext[halo + tm:, :] = hn_ref[...]
        acc[...] = jnp.zeros_like(acc)

    at_start = (i * tm) % seq_len == 0
    at_end = ((i + 1) * tm) % seq_len == 0
    a_ext = jnp.dot(hext[...], wa_ref[...], preferred_element_type=F32)
    up = pltpu.roll(a_ext, 1, 0)[halo:halo + tm]
    dn = pltpu.roll(a_ext, tm + 2 * halo - 1, 0)[halo:halo + tm]
    mid = a_ext[halo:halo + tm]
    r = lax.broadcasted_iota(jnp.int32, (tm, 1), 0)
    up = jnp.where((r == 0) & at_start, 0.0, up)
    dn = jnp.where((r == tm - 1) & at_end, 0.0, dn)
    a = up * cw_ref[0:1, :] + mid * cw_ref[1:2, :] + dn * cw_ref[2:3, :] + cb_ref[...]
    gate = jnp.dot(hext[halo:halo + tm, :], wg_ref[...], preferred_element_type=F32)
    act = (_gelu_exact(a) * gate).astype(BF16)
    acc[...] += jnp.dot(act, wd_ref[...], preferred_element_type=F32)

    @pl.when(j == pl.num_programs(1) - 1)
    def _():
        xo = x_ref[...] + mod_ref[5:6, :] * acc[...]
        o_ref[...] = xo
        if emit_next:
            hnext_ref[...] = _modnorm(xo, gn_ref[...], modn_ref[0:1, :], modn_ref[1:2, :]).astype(BF16)


def _mlp(h2, x, mod, w_up, cw, cb, w_down, l, seq_len, next_norm=None):
    t = x.shape[0]
    tm, tf, halo = MLP_TM, MLP_TF, MLP_HALO
    nf = D_FF // tf
    nbh = t // halo
    emit_next = next_norm is not None
    row = pl.BlockSpec((tm, D_MODEL), lambda i, j: (i, 0))
    modspec = pl.BlockSpec((None, N_MOD, D_MODEL), lambda i, j: ((i * tm) // seq_len, 0, 0))
    in_specs = [
        pl.BlockSpec((halo, D_MODEL), lambda i, j: (jnp.maximum(i * (tm // halo) - 1, 0), 0)),
        row,
        pl.BlockSpec((halo, D_MODEL), lambda i, j: (jnp.minimum((i + 1) * (tm // halo), nbh - 1), 0)),
        row,
        modspec,
        pl.BlockSpec((None, D_MODEL, tf), lambda i, j: (l, 0, j)),
        pl.BlockSpec((None, D_MODEL, tf), lambda i, j: (l, 0, nf + j)),
        pl.BlockSpec((3, tf), lambda i, j: (0, j)),
        pl.BlockSpec((1, tf), lambda i, j: (0, j)),
        pl.BlockSpec((None, tf, D_MODEL), lambda i, j: (l, j, 0)),
    ]
    args = [h2, h2, h2, x, mod, w_up, w_up, cw, cb, w_down]
    out_specs = [row]
    out_shape = [jax.ShapeDtypeStruct((t, D_MODEL), F32)]
    if emit_next:
        in_specs += [modspec, pl.BlockSpec((1, D_MODEL), lambda i, j: (0, 0))]
        args += list(next_norm)
        out_specs.append(row)
        out_shape.append(jax.ShapeDtypeStruct((t, D_MODEL), BF16))
    return pl.pallas_call(
        functools.partial(_mlp_kernel, seq_len=seq_len, emit_next=emit_next),
        grid=(t // tm, nf),
        in_specs=in_specs,
        out_specs=out_specs,
        out_shape=out_shape,
        scratch_shapes=[pltpu.VMEM((tm + 2 * halo, D_MODEL), BF16), pltpu.VMEM((tm, D_MODEL), F32)],
        compiler_params=_cparams(("arbitrary", "arbitrary")),
        name="mlp",
    )(*args)


def _layer(x, h, mod, p, l, bsz, seq_len, kf, next_norm):
    qk, v, u_fn, u_hy = _kin(h, p["w_in"], l, p["q_norm_g"][l][None], p["k_norm_g"][l][None])
    ya = _attn(qk, v, p["na_rpb"][l].reshape(-1), bsz, seq_len)
    yb = _fourier(u_fn, p["gcat"][l], p["fn_b"][l][None], bsz, seq_len)
    z, x0 = _hpre(u_hy, p["hy_conv_w"][l], p["hy_conv_b"][l][None], seq_len)
    yc = _hyena_conv(z, x0, kf, p["hy_d"][l][None], bsz, seq_len)
    x1, h2 = _kout(ya, yb, yc, x, mod, p["out_norm_g"][l][None], p["norm2_g"][l][None], p["w_out"], l, seq_len)
    out = _mlp(h2, x1, mod, p["mlp_w_up"], p["mlp_conv_w"][l], p["mlp_conv_b"][l][None], p["mlp_w_down"], l,
               seq_len, next_norm)
    return (out[0], out[1]) if next_norm is not None else (out[0], None)


def kernel(x_prompt, x_sample, c_prompt, c_sample, ada_w, ada_b, norm1_g, w_in, q_norm_g, k_norm_g, na_rpb, fn_w, fn_b, hy_conv_w, hy_conv_b, hy_w1, hy_b1, hy_f1, hy_w2, hy_b2, hy_f2, hy_w3, hy_b3, hy_d, out_norm_g, w_out, norm2_g, mlp_w_up, mlp_conv_w, mlp_conv_b, mlp_w_down):
    depth = ada_w.shape[0]
    groups = [(x_prompt, c_prompt), (x_sample, c_sample)]
    nseq = sum(c.shape[0] for _, c in groups)
    nrow = -(-nseq // 16) * 16
    c_all = jnp.concatenate([c for _, c in groups] + [jnp.zeros((nrow - nseq, D_MODEL), F32)], axis=0)
    mod_all = _ada(c_all, ada_w, ada_b[:, None, :]).reshape(depth, nrow, N_MOD, D_MODEL)

    p = dict(
        w_in=w_in.astype(BF16), q_norm_g=q_norm_g, k_norm_g=k_norm_g, na_rpb=na_rpb,
        fn_b=fn_b, hy_conv_w=hy_conv_w, hy_conv_b=hy_conv_b, hy_d=hy_d, out_norm_g=out_norm_g,
        w_out=w_out.astype(BF16), norm2_g=norm2_g, mlp_w_up=mlp_w_up.astype(BF16), mlp_conv_w=mlp_conv_w,
        mlp_conv_b=mlp_conv_b, mlp_w_down=mlp_w_down.astype(BF16),
        gcat=[_fnw(fn_w[l]) for l in range(depth)],
    )

    outs = []
    seq_off = 0
    kf_cache = {}
    for x, c in groups:
        bsz, seq_len, _ = x.shape
        xt = x.reshape(bsz * seq_len, D_MODEL)
        mods = [mod_all[l, seq_off:seq_off + bsz] for l in range(depth)]
        ht = _knorm(xt, mods[0], norm1_g[0][None], seq_len)
        for l in range(depth):
            if (l, seq_len) not in kf_cache:
                kf_cache[(l, seq_len)] = _hyena_filter_spectrum(
                    seq_len, hy_w1[l], hy_b1[l][None], hy_f1[l][None], hy_w2[l], hy_b2[l][None], hy_f2[l][None],
                    hy_w3[l], hy_b3[l][None])
            next_norm = (mods[l + 1], norm1_g[l + 1][None]) if l + 1 < depth else None
            xt, ht = _layer(xt, ht, mods[l], p, l, bsz, seq_len, kf_cache[(l, seq_len)], next_norm)
        outs.append(xt.reshape(bsz, seq_len, D_MODEL))
        seq_off += bsz
    return tuple(outs)
```

```python
import functools
import math

import ml_dtypes
import numpy as np
import jax
import jax.numpy as jnp
from jax import lax
from jax.experimental import pallas as pl
from jax.experimental.pallas import tpu as pltpu

F32 = jnp.float32
BF16 = jnp.bfloat16

D_MODEL = 2048
GRID_W = 64
HEAD_DIM = 128
NA_HEADS = 8
NA_WIDTH = NA_HEADS * HEAD_DIM
NA_KH = 8
NA_KW = 16
FN_GROUPS = 4
FN_GROUP_DIM = 128
FN_WIDTH = 512
HY_WIDTH = 512
HY_EMB_DIM = 33
HY_HIDDEN = 64
HY_FAST_DECAY = 0.3
HY_SLOW_DECAY = 1.5
HY_TARGET = 1e-2
IN_PROJ = 3 * NA_WIDTH + FN_WIDTH + 3 * HY_WIDTH
D_FF = 5632
N_MOD = 6
EPS = 1e-6
NEG = -1e30
LOG2E = math.log2(math.e)

LANES = 128
DFT_N2 = 128
VMEM_LIMIT = 56 << 20

HIGHEST = lax.Precision.HIGHEST


def _cparams(sem):
    return pltpu.CompilerParams(dimension_semantics=sem, vmem_limit_bytes=VMEM_LIMIT)


def _cos_sin(n_out, n_in, period):
    ang = 2.0 * np.pi * np.outer(np.arange(n_out), np.arange(n_in)) / period
    return np.cos(ang), np.sin(ang)


def _ada_kernel(c_ref, w_ref, b_ref, o_ref):
    c = c_ref[...]
    s = c * (1.0 / (1.0 + jnp.exp(-c)))
    sh = s.astype(BF16)
    sl = (s - sh.astype(F32)).astype(BF16)
    w = w_ref[...]
    wh = w.astype(BF16)
    wl = (w - wh.astype(F32)).astype(BF16)
    nrow = c.shape[0]
    both = jnp.dot(jnp.concatenate([sh, sl], axis=0), wh, preferred_element_type=F32)
    o_ref[...] = both[:nrow] + both[nrow:] + jnp.dot(sh, wl, preferred_element_type=F32) + b_ref[...]


def _ada(c_all, ada_w, ada_b):
    depth = ada_w.shape[0]
    nrow = c_all.shape[0]
    tn = 1024
    ncol = N_MOD * D_MODEL
    return pl.pallas_call(
        _ada_kernel,
        grid=(depth, ncol // tn),
        in_specs=[
            pl.BlockSpec((nrow, D_MODEL), lambda l, j: (0, 0)),
            pl.BlockSpec((None, D_MODEL, tn), lambda l, j: (l, 0, j)),
            pl.BlockSpec((None, 1, tn), lambda l, j: (l, 0, j)),
        ],
        out_specs=pl.BlockSpec((None, nrow, tn), lambda l, j: (l, 0, j)),
        out_shape=jax.ShapeDtypeStruct((depth, nrow, ncol), F32),
        compiler_params=_cparams(("arbitrary", "arbitrary")),
        name="ada",
    )(c_all, ada_w, ada_b)


KIN_TN = 512
KIN_Q_TILES = NA_WIDTH // KIN_TN
KIN_QK_TILES = 2 * KIN_Q_TILES
KIN_QKV_TILES = 3 * NA_WIDTH // KIN_TN
KIN_FN_TILE = KIN_QKV_TILES
KIN_HY_TILES = 3 * HY_WIDTH // KIN_TN


KNORM_TM = 512


def _modnorm(x, g, shift, scale):
    ms = jnp.mean(x * x, axis=-1, keepdims=True)
    return x * lax.rsqrt(ms + EPS) * g * (1.0 + scale) + shift


def _knorm_kernel(x_ref, mod_ref, g_ref, o_ref):
    o_ref[...] = _modnorm(x_ref[...], g_ref[...], mod_ref[0:1, :], mod_ref[1:2, :]).astype(BF16)


def _knorm(x, mod, g, seq_len):
    t = x.shape[0]
    tm = KNORM_TM
    return pl.pallas_call(
        _knorm_kernel,
        grid=(t // tm,),
        in_specs=[
            pl.BlockSpec((tm, D_MODEL), lambda i: (i, 0)),
            pl.BlockSpec((None, N_MOD, D_MODEL), lambda i: ((i * tm) // seq_len, 0, 0)),
            pl.BlockSpec((1, D_MODEL), lambda i: (0, 0)),
        ],
        out_specs=pl.BlockSpec((tm, D_MODEL), lambda i: (i, 0)),
        out_shape=jax.ShapeDtypeStruct((t, D_MODEL), BF16),
        compiler_params=_cparams(("arbitrary",)),
        name="knorm",
    )(x, mod, g)


def _head_rms(acc, gain):
    outs = []
    for hh in range(KIN_TN // HEAD_DIM):
        a = acc[:, hh * HEAD_DIM:(hh + 1) * HEAD_DIM]
        outs.append(a * lax.rsqrt(jnp.mean(a * a, axis=-1, keepdims=True) + EPS) * gain)
    return jnp.concatenate(outs, axis=-1)


def _kin_kernel(h_ref, w_ref, qg_ref, kg_ref, oqk_ref, ov_ref, ofn_ref, ohy_ref, raw):
    i = pl.program_id(0)
    j = pl.program_id(1)

    @pl.when((i == 0) & (j == 0))
    def _():
        raw[1] = jnp.zeros(raw.shape[1:], F32)

    gain = jnp.where(j - 1 < KIN_Q_TILES, qg_ref[...] * (HEAD_DIM ** -0.5 * LOG2E), kg_ref[...])

    def step(wslot, rslot):
        acc = jnp.dot(h_ref[...], w_ref[...], preferred_element_type=F32)
        oqk_ref[...] = _head_rms(raw[rslot], gain).astype(BF16)
        raw[wslot] = acc

        @pl.when((j >= KIN_QK_TILES) & (j < KIN_QKV_TILES))
        def _():
            ov_ref[...] = acc.astype(BF16)

        @pl.when(j == KIN_FN_TILE)
        def _():
            ofn_ref[...] = acc

        @pl.when(j > KIN_FN_TILE)
        def _():
            ohy_ref[...] = acc

    @pl.when(j % 2 == 0)
    def _():
        step(0, 1)

    @pl.when(j % 2 == 1)
    def _():
        step(1, 0)


def _kin(h, w, l, qg, kg):
    t = h.shape[0]
    tm = 1024
    nj = IN_PROJ // KIN_TN
    return pl.pallas_call(
        _kin_kernel,
        grid=(t // tm, nj),
        in_specs=[
            pl.BlockSpec((tm, D_MODEL), lambda i, j: (i, 0)),
            pl.BlockSpec((None, D_MODEL, KIN_TN), lambda i, j: (l, 0, j)),
            pl.BlockSpec((1, HEAD_DIM), lambda i, j: (0, 0)),
            pl.BlockSpec((1, HEAD_DIM), lambda i, j: (0, 0)),
        ],
        out_specs=[
            pl.BlockSpec((tm, KIN_TN), lambda i, j: (i, jnp.clip(j - 1, 0, KIN_QK_TILES))),
            pl.BlockSpec((tm, KIN_TN), lambda i, j: (i, jnp.clip(j - KIN_QK_TILES, 0, KIN_Q_TILES - 1))),
            pl.BlockSpec((tm, KIN_TN), lambda i, j: (i, 0)),
            pl.BlockSpec((tm, KIN_TN), lambda i, j: (i, jnp.clip(j - KIN_FN_TILE - 1, 0, KIN_HY_TILES - 1))),
        ],
        out_shape=[
            jax.ShapeDtypeStruct((t, (KIN_QK_TILES + 1) * KIN_TN), BF16),
            jax.ShapeDtypeStruct((t, NA_WIDTH), BF16),
            jax.ShapeDtypeStruct((t, FN_WIDTH), F32),
            jax.ShapeDtypeStruct((t, 3 * HY_WIDTH), F32),
        ],
        scratch_shapes=[pltpu.VMEM((2, tm, KIN_TN), F32)],
        compiler_params=_cparams(("arbitrary", "arbitrary")),
        name="kin",
    )(h, w, qg, kg)


ATT_R = 4
ATT_KW = ATT_R + NA_KH


def _attn_variant_tiles(rows, variant):
    if variant == 0:
        r0, ws = 0, 0
    elif variant == 1:
        r0 = ATT_R
        ws = r0 - NA_KH // 2
    else:
        r0, ws = rows - ATT_R, rows - ATT_KW
    tiles = []
    for a in range(ATT_R):
        r = r0 + a
        rs = min(max(r - NA_KH // 2, 0), rows - NA_KH)
        row = []
        for jp in range(ATT_KW // 2):
            kr = ws + 2 * jp
            valid_l = rs <= kr < rs + NA_KH
            valid_r = rs <= kr + 1 < rs + NA_KH
            d_left = kr - r + (NA_KH - 1)
            row.append((valid_l, valid_r, d_left))
        tiles.append(row)
    return tiles


def _attn_build_bias(rpb_ref, t2, bias, h, rows):
    n_rd = 2 * NA_KH - 1
    n_cd = 2 * NA_KW - 1
    cidx = lax.broadcasted_iota(jnp.int32, (GRID_W, LANES), 0)
    lane = lax.broadcasted_iota(jnp.int32, (GRID_W, LANES), 1)
    widx = lane & (GRID_W - 1)
    left = lane < GRID_W
    dcidx = jnp.clip(widx - cidx, -(NA_KW - 1), NA_KW - 1) + (NA_KW - 1)
    base = h * (n_rd * n_cd)
    for dp in range(n_rd + 1):
        acc = jnp.zeros((GRID_W, LANES), F32)
        for e in range(n_cd):
            vl = rpb_ref[base + (dp - 1) * n_cd + e] if dp - 1 >= 0 else 0.0
            vr = rpb_ref[base + dp * n_cd + e] if dp < n_rd else 0.0
            acc = jnp.where(dcidx == e, jnp.where(left, vl, vr), acc)
        t2[dp] = acc * LOG2E

    c_lo = jnp.clip(cidx - NA_KW // 2, 0, GRID_W - NA_KW)
    col_ok = (widx >= c_lo) & (widx < c_lo + NA_KW)
    for variant in range(3):
        tiles = _attn_variant_tiles(rows, variant)
        for a in range(ATT_R):
            for jp in range(ATT_KW // 2):
                valid_l, valid_r, d_left = tiles[a][jp]
                if not (valid_l or valid_r):
                    tile = jnp.full((GRID_W, LANES), NEG, F32)
                else:
                    ok = col_ok
                    if not valid_l:
                        ok = ok & (~left)
                    if not valid_r:
                        ok = ok & left
                    tile = jnp.where(ok, t2[d_left + 1], NEG)
                bias[variant, a * GRID_W:(a + 1) * GRID_W, jp * LANES:(jp + 1) * LANES] = tile


def _attn_kernel(rpb_ref, q_ref, k_ref, v_ref, o_ref, t2, bias, s_buf, p_buf, rden_buf, *, seq_len):
    rows = seq_len // GRID_W
    nb = rows // ATT_R

    @pl.when(pl.program_id(1) == 0)
    def _():
        _attn_build_bias(rpb_ref, t2, bias, pl.program_id(0), rows)

    def q_rows(rb):
        if isinstance(rb, int):
            return pl.ds(rb * (ATT_R * GRID_W), ATT_R * GRID_W)
        return pl.ds(pl.multiple_of(rb * (ATT_R * GRID_W), ATT_R * GRID_W), ATT_R * GRID_W)

    def k_rows(rb):
        if isinstance(rb, int):
            return pl.ds(min(max(rb * ATT_R - NA_KH // 2, 0), rows - ATT_KW) * GRID_W, ATT_KW * GRID_W)
        ws = jnp.clip(rb * ATT_R - NA_KH // 2, 0, rows - ATT_KW)
        return pl.ds(pl.multiple_of(ws * GRID_W, GRID_W), ATT_KW * GRID_W)

    def scores(rb, slot):
        if isinstance(rb, int):
            variant = 0 if rb == 0 else (2 if rb == nb - 1 else 1)
        else:
            variant = jnp.where(rb == 0, 0, jnp.where(rb == nb - 1, 2, 1))
        s = lax.dot_general(q_ref[q_rows(rb), :], k_ref[k_rows(rb), :], (((1,), (1,)), ((), ())),
                            preferred_element_type=F32)
        s_buf[slot] = s + bias[variant]

    def softmax(slot):
        s = s_buf[slot]
        m = jnp.max(s, axis=-1, keepdims=True)
        p = jnp.exp2(s - m)
        rden_buf[slot] = 1.0 / jnp.sum(p, axis=-1, keepdims=True)
        p_buf[slot] = p.astype(BF16)

    def values(rb, slot):
        o = jnp.dot(p_buf[slot], v_ref[k_rows(rb), :], preferred_element_type=F32)
        o_ref[q_rows(rb), :] = o * rden_buf[slot]

    scores(0, 0)
    scores(1, 1)
    softmax(0)

    def body(v, carry):
        t = 2 * v
        values(t, 0)
        softmax(1)
        scores(t + 2, 0)
        values(t + 1, 1)
        softmax(0)
        scores(t + 3, 1)
        return carry

    lax.fori_loop(0, nb // 2 - 1, body, 0)
    values(nb - 2, 0)
    softmax(1)
    values(nb - 1, 1)


def _attn(qk, v, rpb_flat, bsz, seq_len):
    t = qk.shape[0]
    kernel = functools.partial(_attn_kernel, seq_len=seq_len)
    return pl.pallas_call(
        kernel,
        grid=(NA_HEADS, bsz),
        in_specs=[
            pl.BlockSpec(memory_space=pltpu.SMEM),
            pl.BlockSpec((seq_len, HEAD_DIM), lambda h, b: (b, h)),
            pl.BlockSpec((seq_len, HEAD_DIM), lambda h, b: (b, NA_HEADS + h)),
            pl.BlockSpec((seq_len, HEAD_DIM), lambda h, b: (b, h)),
        ],
        out_specs=pl.BlockSpec((seq_len, HEAD_DIM), lambda h, b: (b, h)),
        out_shape=jax.ShapeDtypeStruct((t, NA_WIDTH), F32),
        scratch_shapes=[
            pltpu.VMEM((2 * NA_KH, GRID_W, LANES), F32),
            pltpu.VMEM((3, ATT_R * GRID_W, ATT_KW * GRID_W), F32),
            pltpu.VMEM((2, ATT_R * GRID_W, ATT_KW * GRID_W), F32),
            pltpu.VMEM((2, ATT_R * GRID_W, ATT_KW * GRID_W), BF16),
            pltpu.VMEM((2, ATT_R * GRID_W, 1), F32),
        ],
        compiler_params=_cparams(("arbitrary", "arbitrary")),
        name="attn",
    )(rpb_flat, qk, qk, v)


DFT_TW = 8


def _kron_mat(m):
    eye = np.eye(DFT_TW, dtype=ml_dtypes.bfloat16)
    return jnp.asarray(np.kron(m.astype(ml_dtypes.bfloat16), eye))


BF16_ROWS = 16


def _dft1_kernel(m_ref, x_ref, twr_ref, twi_ref, o_ref):
    kd, rows, c = x_ref.shape
    mo = o_ref.shape[0]
    kh = mo // 2
    for g in range(rows // BF16_ROWS):
        re, im = [], []
        for s in range(g * BF16_ROWS // DFT_TW, (g + 1) * BF16_ROWS // DFT_TW):
            sl = slice(s * DFT_TW, (s + 1) * DFT_TW)
            x = x_ref[:, sl, :].reshape(kd * DFT_TW, c)
            y = jnp.dot(m_ref[...], x.astype(BF16), preferred_element_type=F32).reshape(mo, DFT_TW, c)
            twr = _lane_tile(twr_ref[:, sl, :], c)
            twi = _lane_tile(twi_ref[:, sl, :], c)
            yr = y[:kh]
            yi = y[kh:]
            re.append(yr * twr - yi * twi)
            im.append(yr * twi + yi * twr)
        gl = slice(g * BF16_ROWS, (g + 1) * BF16_ROWS)
        o_ref[:kh, gl, :] = jnp.concatenate(re, axis=1).astype(BF16)
        o_ref[kh:, gl, :] = jnp.concatenate(im, axis=1).astype(BF16)


def _dft1(mk, x, tw):
    bsz, kd, n2, c = x.shape
    mo = mk.shape[0] // DFT_TW
    rows = 2 * BF16_ROWS if c <= FN_WIDTH else BF16_ROWS
    return pl.pallas_call(
        _dft1_kernel,
        grid=(bsz, n2 // rows),
        in_specs=[
            pl.BlockSpec(mk.shape, lambda b, j: (0, 0)),
            pl.BlockSpec((None, kd, rows, c), lambda b, j: (b, 0, j, 0)),
            pl.BlockSpec((mo // 2, rows, LANES), lambda b, j: (0, j, 0)),
            pl.BlockSpec((mo // 2, rows, LANES), lambda b, j: (0, j, 0)),
        ],
        out_specs=pl.BlockSpec((None, mo, rows, c), lambda b, j: (b, 0, j, 0)),
        out_shape=jax.ShapeDtypeStruct((bsz, mo, n2, c), BF16),
        compiler_params=_cparams(("arbitrary", "arbitrary")),
        name="dft1",
    )(mk, x, *tw)


def _stage1_fwd_mat(n1, kd, scale=1.0, rows=None):
    c, s = _cos_sin(n1 if rows is None else rows, kd, n1)
    return _kron_mat(np.concatenate([c, -s], axis=0) * scale)


def _stage2_mats(n2):
    c, s = _cos_sin(n2, n2, n2)
    fwd = np.block([[c, s], [-s, c]])
    inv = np.block([[c, -s], [s, c]])
    return jnp.asarray(fwd.astype(ml_dtypes.bfloat16)), jnp.asarray(inv.astype(ml_dtypes.bfloat16))


def _half_rows(n1):
    return n1 // 2 + DFT_TW


def _twiddle(n1, n2, rows=None):
    rows = n1 if rows is None else rows
    ang = 2.0 * np.pi * np.outer(np.arange(rows), np.arange(n2)) / (n1 * n2)
    twr = jnp.asarray(np.cos(ang).astype(np.float32))
    twi = jnp.asarray((-np.sin(ang)).astype(np.float32))
    shape = (rows, n2, LANES)
    return jnp.broadcast_to(twr[:, :, None], shape), jnp.broadcast_to(twi[:, :, None], shape)


def _lane_tile(x, width):
    reps = width // x.shape[-1]
    return x if reps == 1 else jnp.concatenate([x] * reps, axis=-1)


FFT_KC = 8


def _fnw_kernel(cs_ref, w_ref, o_ref):
    o_ref[...] = jnp.dot(cs_ref[...], w_ref[...], precision=HIGHEST, preferred_element_type=F32)


def _fnw(fn_w):
    c, s = _cos_sin(FN_GROUP_DIM, FN_GROUP_DIM, FN_GROUP_DIM)
    cs = jnp.asarray(np.concatenate([c, s], axis=0).astype(np.float32))
    return pl.pallas_call(
        _fnw_kernel,
        grid=(FN_GROUPS,),
        in_specs=[
            pl.BlockSpec((2 * FN_GROUP_DIM, FN_GROUP_DIM), lambda g: (0, 0)),
            pl.BlockSpec((None, FN_GROUP_DIM, FN_GROUP_DIM), lambda g: (g, 0, 0)),
        ],
        out_specs=pl.BlockSpec((None, 2 * FN_GROUP_DIM, FN_GROUP_DIM), lambda g: (g, 0, 0)),
        out_shape=jax.ShapeDtypeStruct((FN_GROUPS, 2 * FN_GROUP_DIM, FN_GROUP_DIM), F32),
        compiler_params=_cparams(("arbitrary",)),
        name="fnw",
    )(cs, fn_w)


def _fn2_kernel(y_ref, f2_ref, g_ref, b_ref, o_ref):
    n2 = DFT_N2
    for t in range(FFT_KC):
        rhs = jnp.concatenate([y_ref[0, t], y_ref[1, t]], axis=0)
        z = jnp.dot(f2_ref[...], rhs, preferred_element_type=F32)
        zr = z[:n2]
        zi = z[n2:]
        outs = []
        for g in range(FN_GROUPS):
            sl = slice(g * FN_GROUP_DIM, (g + 1) * FN_GROUP_DIM)
            zc = jnp.concatenate([zr[:, sl], zi[:, sl]], axis=1).astype(BF16)
            outs.append(jnp.dot(zc, g_ref[g].astype(BF16), preferred_element_type=F32))
        o_ref[:, t, :] = jnp.concatenate(outs, axis=1) + b_ref[...]


def _fourier(u, gcat, fn_b, bsz, seq_len):
    n2 = DFT_N2
    n1 = seq_len // n2
    m1 = _stage1_fwd_mat(n1, n1, scale=1.0 / math.sqrt(seq_len * FN_GROUP_DIM))
    y = _dft1(m1, u.reshape(bsz, n1, n2, FN_WIDTH), _twiddle(n1, n2))
    y = y.reshape(bsz, 2, n1, n2, FN_WIDTH)
    f2, _ = _stage2_mats(n2)
    out = pl.pallas_call(
        _fn2_kernel,
        grid=(n1 // FFT_KC, bsz),
        in_specs=[
            pl.BlockSpec((None, 2, FFT_KC, n2, FN_WIDTH), lambda k, b: (b, 0, k, 0, 0)),
            pl.BlockSpec(f2.shape, lambda k, b: (0, 0)),
            pl.BlockSpec((FN_GROUPS, 2 * FN_GROUP_DIM, FN_GROUP_DIM), lambda k, b: (0, 0, 0)),
            pl.BlockSpec((1, FN_WIDTH), lambda k, b: (0, 0)),
        ],
        out_specs=pl.BlockSpec((None, n2, FFT_KC, FN_WIDTH), lambda k, b: (b, 0, k, 0)),
        out_shape=jax.ShapeDtypeStruct((bsz, n2, n1, FN_WIDTH), F32),
        compiler_params=_cparams(("arbitrary", "arbitrary")),
        name="fn2",
    )(y, f2, gcat, fn_b)
    return out.reshape(bsz * seq_len, FN_WIDTH)


HPRE_TM = 512


def _hpre_kernel(up_ref, um_ref, un_ref, cw_ref, cb_ref, z_ref, x0_ref, *, seq_len):
    tm = HPRE_TM
    i = pl.program_id(0)
    at_start = (i * tm) % seq_len == 0
    at_end = ((i + 1) * tm) % seq_len == 0
    u = um_ref[...]
    r = lax.broadcasted_iota(jnp.int32, (tm, 1), 0)
    prev_row = jnp.where(at_start, 0.0, up_ref[7:8, :])
    next_row = jnp.where(at_end, 0.0, un_ref[0:1, :])
    upv = jnp.where(r == 0, prev_row, pltpu.roll(u, 1, 0))
    dnv = jnp.where(r == tm - 1, next_row, pltpu.roll(u, tm - 1, 0))
    y = upv * cw_ref[0:1, :] + u * cw_ref[1:2, :] + dnv * cw_ref[2:3, :] + cb_ref[...]
    z_ref[...] = y[:, 2 * HY_WIDTH:] * y[:, HY_WIDTH:2 * HY_WIDTH]
    x0_ref[...] = y[:, :HY_WIDTH]


def _hpre(u, cw, cb, seq_len):
    t = u.shape[0]
    tm = HPRE_TM
    w = 3 * HY_WIDTH
    nb8 = t // 8
    return pl.pallas_call(
        functools.partial(_hpre_kernel, seq_len=seq_len),
        grid=(t // tm,),
        in_specs=[
            pl.BlockSpec((8, w), lambda i: (jnp.maximum(i * (tm // 8) - 1, 0), 0)),
            pl.BlockSpec((tm, w), lambda i: (i, 0)),
            pl.BlockSpec((8, w), lambda i: (jnp.minimum((i + 1) * (tm // 8), nb8 - 1), 0)),
            pl.BlockSpec((3, w), lambda i: (0, 0)),
            pl.BlockSpec((1, w), lambda i: (0, 0)),
        ],
        out_specs=[
            pl.BlockSpec((tm, HY_WIDTH), lambda i: (i, 0)),
            pl.BlockSpec((tm, HY_WIDTH), lambda i: (i, 0)),
        ],
        out_shape=[jax.ShapeDtypeStruct((t, HY_WIDTH), F32), jax.ShapeDtypeStruct((t, HY_WIDTH), F32)],
        compiler_params=_cparams(("arbitrary",)),
        name="hpre",
    )(u, u, u, cw, cb)


HFILT_TL = 512
HY_EMB_PAD = 128


def _hfilt_kernel(z_ref, w1_ref, b1_ref, f1_ref, w2_ref, b2_ref, f2_ref, w3_ref, b3_ref, dl_ref, o_ref):
    z = z_ref[...]
    h = jnp.sin(f1_ref[...] * (jnp.dot(z, w1_ref[...], precision=HIGHEST, preferred_element_type=F32) + b1_ref[...]))
    h = jnp.sin(f2_ref[...] * (jnp.dot(h, w2_ref[...], precision=HIGHEST, preferred_element_type=F32) + b2_ref[...]))
    h = jnp.dot(h, w3_ref[...], precision=HIGHEST, preferred_element_type=F32) + b3_ref[...]
    t = z[:, 0:1]
    h = h * jnp.exp(-t * dl_ref[...])
    row = lax.broadcasted_iota(jnp.int32, h.shape, 0) + pl.program_id(0) * HFILT_TL
    col = lax.broadcasted_iota(jnp.int32, h.shape, 1)
    o_ref[...] = jnp.where((row == 0) & (col >= HY_WIDTH), 0.0, h)


def _hyena_emb_np(seq_len):
    t = np.linspace(0.0, 1.0, seq_len)[:, None]
    bands = (HY_EMB_DIM - 1) // 2
    w = 2.0 * np.pi * np.arange(seq_len)[:, None] / seq_len
    fr = np.linspace(1e-4, bands - 1, bands)[None, :]
    z = np.concatenate([t, np.cos(fr * w), -np.sin(fr * w)], axis=-1)
    out = np.zeros((seq_len, HY_EMB_PAD), np.float32)
    out[:, :HY_EMB_DIM] = z
    return out


def _hfilt(seq_len, w1, b1, f1, w2, b2, f2, w3, b3):
    zemb = jnp.asarray(_hyena_emb_np(seq_len))
    max_decay = math.log(HY_TARGET) / HY_FAST_DECAY
    min_decay = math.log(HY_TARGET) / HY_SLOW_DECAY
    deltas = np.abs(np.linspace(min_decay, max_decay, HY_WIDTH))
    dl = jnp.asarray(np.tile(deltas, 2)[None, :].astype(np.float32))
    w1p = jnp.pad(w1, ((0, HY_EMB_PAD - HY_EMB_DIM), (0, 0)))
    tl = HFILT_TL
    full = lambda shape: pl.BlockSpec(shape, lambda i: (0,) * len(shape))
    return pl.pallas_call(
        _hfilt_kernel,
        grid=(seq_len // tl,),
        in_specs=[
            pl.BlockSpec((tl, HY_EMB_PAD), lambda i: (i, 0)),
            full((HY_EMB_PAD, HY_HIDDEN)), full((1, HY_HIDDEN)), full((1, HY_HIDDEN)),
            full((HY_HIDDEN, HY_HIDDEN)), full((1, HY_HIDDEN)), full((1, HY_HIDDEN)),
            full((HY_HIDDEN, 2 * HY_WIDTH)), full((1, 2 * HY_WIDTH)), full((1, 2 * HY_WIDTH)),
        ],
        out_specs=pl.BlockSpec((tl, 2 * HY_WIDTH), lambda i: (i, 0)),
        out_shape=jax.ShapeDtypeStruct((seq_len, 2 * HY_WIDTH), F32),
        compiler_params=_cparams(("arbitrary",)),
        name="hfilt",
    )(zemb, w1p, b1, f1, w2, b2, f2, w3, b3, dl)


def _hk2_kernel(y_ref, f2_ref, o_ref):
    n2 = DFT_N2
    for t in range(FFT_KC):
        rhs = jnp.concatenate([y_ref[0, t], y_ref[1, t]], axis=0)
        z = jnp.dot(f2_ref[...], rhs, preferred_element_type=F32)
        zr = z[:n2]
        zi = z[n2:]
        o_ref[0, t] = zr[:, :HY_WIDTH] + zr[:, HY_WIDTH:]
        o_ref[1, t] = zi[:, :HY_WIDTH] - zi[:, HY_WIDTH:]


def _hy2_kernel(y_ref, kf_ref, twr_ref, twi_ref, f2_ref, f2i_ref, o_ref):
    n2 = DFT_N2
    for t in range(FFT_KC):
        rhs = jnp.concatenate([y_ref[0, t], y_ref[1, t]], axis=0)
        z = jnp.dot(f2_ref[...], rhs, preferred_element_type=F32)
        zr = z[:n2]
        zi = z[n2:]
        kr = kf_ref[0, t]
        ki = kf_ref[1, t]
        pr = zr * kr - zi * ki
        pi = zr * ki + zi * kr
        rhs = jnp.concatenate([pr, pi], axis=0).astype(BF16)
        g = jnp.dot(f2i_ref[...], rhs, preferred_element_type=F32)
        gr = g[:n2]
        gi = g[n2:]
        twr = _lane_tile(twr_ref[t], HY_WIDTH)
        twi = _lane_tile(twi_ref[t], HY_WIDTH)
        o_ref[0, t] = (gr * twr + gi * twi).astype(BF16)
        o_ref[1, t] = (gi * twr - gr * twi).astype(BF16)


def _hy3_kernel(m_ref, g_ref, z_ref, x0_ref, d_ref, o_ref):
    kd, rows, c = g_ref.shape
    mo = o_ref.shape[0]
    g = g_ref[...].astype(F32)
    for s in range(rows // DFT_TW):
        sl = slice(s * DFT_TW, (s + 1) * DFT_TW)
        rhs = g[:, sl, :].reshape(kd * DFT_TW, c).astype(BF16)
        conv = jnp.dot(m_ref[...], rhs, preferred_element_type=F32).reshape(mo, DFT_TW, c)
        o_ref[:, sl, :] = (conv + z_ref[:, sl, :] * d_ref[...]) * x0_ref[:, sl, :]


def _hyena_filter_spectrum(seq_len, w1, b1, f1, w2, b2, f2, w3, b3):
    n = 2 * seq_len
    n2 = DFT_N2
    n1 = n // n2
    kh = _half_rows(n1)
    h = _hfilt(seq_len, w1, b1, f1, w2, b2, f2, w3, b3)
    m1 = _stage1_fwd_mat(n1, n1 // 2, rows=kh)
    y = _dft1(m1, h.reshape(1, n1 // 2, n2, 2 * HY_WIDTH), _twiddle(n1, n2, kh))
    y = y.reshape(2, kh, n2, 2 * HY_WIDTH)
    f2m, _ = _stage2_mats(n2)
    return pl.pallas_call(
        _hk2_kernel,
        grid=(kh // FFT_KC,),
        in_specs=[
            pl.BlockSpec((2, FFT_KC, n2, 2 * HY_WIDTH), lambda k: (0, k, 0, 0)),
            pl.BlockSpec(f2m.shape, lambda k: (0, 0)),
        ],
        out_specs=pl.BlockSpec((2, FFT_KC, n2, HY_WIDTH), lambda k: (0, k, 0, 0)),
        out_shape=jax.ShapeDtypeStruct((2, kh, n2, HY_WIDTH), F32),
        compiler_params=_cparams(("arbitrary",)),
        name="hk2",
    )(y, f2m)


def _hyena_conv(z, x0, kf, d_skip, bsz, seq_len):
    n = 2 * seq_len
    n2 = DFT_N2
    n1 = n // n2
    kh = _half_rows(n1)
    zv = z.reshape(bsz, n1 // 2, n2, HY_WIDTH)
    tw = _twiddle(n1, n2, kh)
    y = _dft1(_stage1_fwd_mat(n1, n1 // 2, rows=kh), zv, tw).reshape(bsz, 2, kh, n2, HY_WIDTH)
    f2m, f2i = _stage2_mats(n2)
    g = pl.pallas_call(
        _hy2_kernel,
        grid=(kh // FFT_KC, bsz),
        in_specs=[
            pl.BlockSpec((None, 2, FFT_KC, n2, HY_WIDTH), lambda k, b: (b, 0, k, 0, 0)),
            pl.BlockSpec((2, FFT_KC, n2, HY_WIDTH), lambda k, b: (0, k, 0, 0)),
            pl.BlockSpec((FFT_KC, n2, LANES), lambda k, b: (k, 0, 0)),
            pl.BlockSpec((FFT_KC, n2, LANES), lambda k, b: (k, 0, 0)),
            pl.BlockSpec(f2m.shape, lambda k, b: (0, 0)),
            pl.BlockSpec(f2i.shape, lambda k, b: (0, 0)),
        ],
        out_specs=pl.BlockSpec((None, 2, FFT_KC, n2, HY_WIDTH), lambda k, b: (b, 0, k, 0, 0)),
        out_shape=jax.ShapeDtypeStruct((bsz, 2, kh, n2, HY_WIDTH), BF16),
        compiler_params=_cparams(("arbitrary", "arbitrary")),
        name="hy2",
    )(y, kf, tw[0], tw[1], f2m, f2i)
    c, s = _cos_sin(n1 // 2, kh, n1)
    wt = np.where(np.arange(kh) < n1 // 2, 2.0, 0.0)
    wt[0] = 1.0
    wt[n1 // 2] = 1.0
    m3 = _kron_mat(np.concatenate([c * wt, -s * wt], axis=1) / n)
    rows = BF16_ROWS
    half =pl.BlockSpec((None, n1 // 2, rows, HY_WIDTH), lambda b, j: (b, 0, j, 0))
    out = pl.pallas_call(
        _hy3_kernel,
        grid=(bsz, n2 // rows),
        in_specs=[
            pl.BlockSpec(m3.shape, lambda b, j: (0, 0)),
            pl.BlockSpec((None, 2 * kh, rows, HY_WIDTH), lambda b, j: (b, 0, j, 0)),
            half,
            half,
            pl.BlockSpec((1, HY_WIDTH), lambda b, j: (0, 0)),
        ],
        out_specs=half,
        out_shape=jax.ShapeDtypeStruct((bsz, n1 // 2, n2, HY_WIDTH), F32),
        compiler_params=_cparams(("arbitrary", "arbitrary")),
        name="hy3",
    )(m3, g.reshape(bsz, 2 * kh, n2, HY_WIDTH), zv, x0.reshape(bsz, n1 // 2, n2, HY_WIDTH), d_skip)
    return out.reshape(bsz * seq_len, HY_WIDTH)


KOUT_TM = 512
KOUT_SUB = 256


def _rms(y):
    return y * lax.rsqrt(jnp.mean(y * y, axis=-1, keepdims=True) + EPS)


def _kout_kernel(ya_ref, yb_ref, yc_ref, x_ref, mod_ref, og_ref, n2g_ref, w_ref, xo_ref, h2_ref):
    for s in range(KOUT_TM // KOUT_SUB):
        sl = slice(s * KOUT_SUB, (s + 1) * KOUT_SUB)
        n = jnp.concatenate([_rms(ya_ref[sl, :]), _rms(yb_ref[sl, :]), _rms(yc_ref[sl, :])], axis=-1) * og_ref[...]
        y = jnp.dot(n.astype(BF16), w_ref[...], preferred_element_type=F32)
        xn = x_ref[sl, :] + mod_ref[2:3, :] * y
        xo_ref[sl, :] = xn
        h2 = _rms(xn) * n2g_ref[...]
        h2_ref[sl, :] = (h2 * (1.0 + mod_ref[4:5, :]) + mod_ref[3:4, :]).astype(BF16)


def _kout(ya, yb, yc, x, mod, og, n2g, w, l, seq_len):
    t = x.shape[0]
    tm = KOUT_TM
    row = lambda width: pl.BlockSpec((tm, width), lambda i: (i, 0))
    return pl.pallas_call(
        _kout_kernel,
        grid=(t // tm,),
        in_specs=[
            row(NA_WIDTH), row(FN_WIDTH), row(HY_WIDTH), row(D_MODEL),
            pl.BlockSpec((None, N_MOD, D_MODEL), lambda i: ((i * tm) // seq_len, 0, 0)),
            pl.BlockSpec((1, D_MODEL), lambda i: (0, 0)),
            pl.BlockSpec((1, D_MODEL), lambda i: (0, 0)),
            pl.BlockSpec((None, D_MODEL, D_MODEL), lambda i: (l, 0, 0)),
        ],
        out_specs=[row(D_MODEL), row(D_MODEL)],
        out_shape=[jax.ShapeDtypeStruct((t, D_MODEL), F32), jax.ShapeDtypeStruct((t, D_MODEL), BF16)],
        compiler_params=_cparams(("arbitrary",)),
        name="kout",
    )(ya, yb, yc, x, mod, og, n2g, w)


MLP_TM = 512
MLP_TF = 512
MLP_HALO = 16


def _gelu_exact(a):
    return 0.5 * a * (1.0 + lax.erf(a * (1.0 / math.sqrt(2.0))))


def _mlp_kernel(hp_ref, hm_ref, hn_ref, x_ref, mod_ref, wa_ref, wg_ref, cw_ref, cb_ref, wd_ref, *rest,
                seq_len, emit_next):
    if emit_next:
        modn_ref, gn_ref, o_ref, hnext_ref, hext, acc = rest
    else:
        o_ref, hext, acc = rest
    tm = MLP_TM
    halo = MLP_HALO
    i = pl.program_id(0)
    j = pl.program_id(1)

    @pl.when(j == 0)
    def _():
        hext[0:halo, :] = hp_ref[...]
        hext[halo:halo + tm, :] = hm_ref[...]
        hext[halo + tm:, :] = hn_ref[...]
        acc[...] = jnp.zeros_like(acc)

    at_start = (i * tm) % seq_len == 0
    at_end = ((i + 1) * tm) % seq_len == 0
    a_ext = jnp.dot(hext[...], wa_ref[...], preferred_element_type=F32)
    up = pltpu.roll(a_ext, 1, 0)[halo:halo + tm]
    dn = pltpu.roll(a_ext, tm + 2 * halo - 1, 0)[halo:halo + tm]
    mid = a_ext[halo:halo + tm]
    r = lax.broadcasted_iota(jnp.int32, (tm, 1), 0)
    up = jnp.where((r == 0) & at_start, 0.0, up)
    dn = jnp.where((r == tm - 1) & at_end, 0.0, dn)
    a = up * cw_ref[0:1, :] + mid * cw_ref[1:2, :] + dn * cw_ref[2:3, :] + cb_ref[...]
    gate = jnp.dot(hext[halo:halo + tm, :], wg_ref[...], preferred_element_type=F32)
    act = (_gelu_exact(a) * gate).astype(BF16)
    acc[...] += jnp.dot(act, wd_ref[...], preferred_element_type=F32)

    @pl.when(j == pl.num_programs(1) - 1)
    def _():
        xo = x_ref[...] + mod_ref[5:6, :] * acc[...]
        o_ref[...] = xo
        if emit_next:
            hnext_ref[...] = _modnorm(xo, gn_ref[...], modn_ref[0:1, :], modn_ref[1:2, :]).astype(BF16)


def _mlp(h2, x, mod, w_up, cw, cb, w_down, l, seq_len, next_norm=None):
    t = x.shape[0]
    tm, tf, halo = MLP_TM, MLP_TF, MLP_HALO
    nf = D_FF // tf
    nbh = t // halo
    emit_next = next_norm is not None
    row = pl.BlockSpec((tm, D_MODEL), lambda i, j: (i, 0))
    modspec = pl.BlockSpec((None, N_MOD, D_MODEL), lambda i, j: ((i * tm) // seq_len, 0, 0))
    in_specs = [
        pl.BlockSpec((halo, D_MODEL), lambda i, j: (jnp.maximum(i * (tm // halo) - 1, 0), 0)),
        row,
        pl.BlockSpec((halo, D_MODEL), lambda i, j: (jnp.minimum((i + 1) * (tm // halo), nbh - 1), 0)),
        row,
        modspec,
        pl.BlockSpec((None, D_MODEL, tf), lambda i, j: (l, 0, j)),
        pl.BlockSpec((None, D_MODEL, tf), lambda i, j: (l, 0, nf + j)),
        pl.BlockSpec((3, tf), lambda i, j: (0, j)),
        pl.BlockSpec((1, tf), lambda i, j: (0, j)),
        pl.BlockSpec((None, tf, D_MODEL), lambda i, j: (l, j, 0)),
    ]
    args = [h2, h2, h2, x, mod, w_up, w_up, cw, cb, w_down]
    out_specs = [row]
    out_shape = [jax.ShapeDtypeStruct((t, D_MODEL), F32)]
    if emit_next:
        in_specs += [modspec, pl.BlockSpec((1, D_MODEL), lambda i, j: (0, 0))]
        args += list(next_norm)
        out_specs.append(row)
        out_shape.append(jax.ShapeDtypeStruct((t, D_MODEL), BF16))
    return pl.pallas_call(
        functools.partial(_mlp_kernel, seq_len=seq_len, emit_next=emit_next),
        grid=(t // tm, nf),
        in_specs=in_specs,
        out_specs=out_specs,
        out_shape=out_shape,
        scratch_shapes=[pltpu.VMEM((tm + 2 * halo, D_MODEL), BF16), pltpu.VMEM((tm, D_MODEL), F32)],
        compiler_params=_cparams(("arbitrary", "arbitrary")),
        name="mlp",
    )(*args)


def _layer(x, h, mod, p, l, bsz, seq_len, kf, next_norm):
    qk, v, u_fn, u_hy = _kin(h, p["w_in"], l, p["q_norm_g"][l][None], p["k_norm_g"][l][None])
    ya = _attn(qk, v, p["na_rpb"][l].reshape(-1), bsz, seq_len)
    yb = _fourier(u_fn, p["gcat"][l], p["fn_b"][l][None], bsz, seq_len)
    z, x0 = _hpre(u_hy, p["hy_conv_w"][l], p["hy_conv_b"][l][None], seq_len)
    yc = _hyena_conv(z, x0, kf, p["hy_d"][l][None], bsz, seq_len)
    x1, h2 = _kout(ya, yb, yc, x, mod, p["out_norm_g"][l][None], p["norm2_g"][l][None], p["w_out"], l, seq_len)
    out = _mlp(h2, x1, mod, p["mlp_w_up"], p["mlp_conv_w"][l], p["mlp_conv_b"][l][None], p["mlp_w_down"], l,
               seq_len, next_norm)
    return (out[0], out[1]) if next_norm is not None else (out[0], None)


def kernel(x_prompt, x_sample, c_prompt, c_sample, ada_w, ada_b, norm1_g, w_in, q_norm_g, k_norm_g, na_rpb, fn_w, fn_b, hy_conv_w, hy_conv_b, hy_w1, hy_b1, hy_f1, hy_w2, hy_b2, hy_f2, hy_w3, hy_b3, hy_d, out_norm_g, w_out, norm2_g, mlp_w_up, mlp_conv_w, mlp_conv_b, mlp_w_down):
    depth = ada_w.shape[0]
    groups = [(x_prompt, c_prompt), (x_sample, c_sample)]
    nseq = sum(c.shape[0] for _, c in groups)
    nrow = -(-nseq // 16) * 16
    c_all = jnp.concatenate([c for _, c in groups] + [jnp.zeros((nrow - nseq, D_MODEL), F32)], axis=0)
    mod_all = _ada(c_all, ada_w, ada_b[:, None, :]).reshape(depth, nrow, N_MOD, D_MODEL)

    p = dict(
        w_in=w_in.astype(BF16), q_norm_g=q_norm_g, k_norm_g=k_norm_g, na_rpb=na_rpb,
        fn_b=fn_b, hy_conv_w=hy_conv_w, hy_conv_b=hy_conv_b, hy_d=hy_d, out_norm_g=out_norm_g,
        w_out=w_out.astype(BF16), norm2_g=norm2_g, mlp_w_up=mlp_w_up.astype(BF16), mlp_conv_w=mlp_conv_w,
        mlp_conv_b=mlp_conv_b, mlp_w_down=mlp_w_down.astype(BF16),
        gcat=[_fnw(fn_w[l]) for l in range(depth)],
    )

    outs = []
    seq_off = 0
    kf_cache = {}
    for x, c in groups:
        bsz, seq_len, _ = x.shape
        xt = x.reshape(bsz * seq_len, D_MODEL)
        mods = [mod_all[l, seq_off:seq_off + bsz] for l in range(depth)]
        ht = _knorm(xt, mods[0], norm1_g[0][None], seq_len)
        for l in range(depth):
            if (l, seq_len) not in kf_cache:
                kf_cache[(l, seq_len)] = _hyena_filter_spectrum(
                    seq_len, hy_w1[l], hy_b1[l][None], hy_f1[l][None], hy_w2[l], hy_b2[l][None], hy_f2[l][None],
                    hy_w3[l], hy_b3[l][None])
            next_norm = (mods[l + 1], norm1_g[l + 1][None]) if l + 1 < depth else None
            xt, ht = _layer(xt, ht, mods[l], p, l, bsz, seq_len, kf_cache[(l, seq_len)], next_norm)
        outs.append(xt.reshape(bsz, seq_len, D_MODEL))
        seq_off += bsz
    return tuple(outs)
```

```python
import functools
import math

import ml_dtypes
import numpy as np
import jax
import jax.numpy as jnp
from jax import lax
from jax.experimental import pallas as pl
from jax.experimental.pallas import tpu as pltpu

F32 = jnp.float32
BF16 = jnp.bfloat16

D_MODEL = 2048
GRID_W = 64
HEAD_DIM = 128
NA_HEADS = 8
NA_WIDTH = NA_HEADS * HEAD_DIM
NA_KH = 8
NA_KW = 16
FN_GROUPS = 4
FN_GROUP_DIM = 128
FN_WIDTH = 512
HY_WIDTH = 512
HY_EMB_DIM = 33
HY_HIDDEN = 64
HY_FAST_DECAY = 0.3
HY_SLOW_DECAY = 1.5
HY_TARGET = 1e-2
IN_PROJ = 3 * NA_WIDTH + FN_WIDTH + 3 * HY_WIDTH
D_FF = 5632
SHORT_CONV = 3
N_MOD = 6
EPS = 1e-6
NEG = -1e30
LOG2E = math.log2(math.e)

LANES = 128
DFT_N2 = 128
VMEM_LIMIT = 56 << 20

HIGHEST = lax.Precision.HIGHEST


def _cparams(sem):
    return pltpu.CompilerParams(dimension_semantics=sem, vmem_limit_bytes=VMEM_LIMIT)


def _cos_sin(n_out, n_in, period):
    ang = 2.0 * np.pi * np.outer(np.arange(n_out), np.arange(n_in)) / period
    return np.cos(ang), np.sin(ang)


def _ada_kernel(c_ref, w_ref, b_ref, o_ref):
    c = c_ref[...]
    s = c * (1.0 / (1.0 + jnp.exp(-c)))
    sh = s.astype(BF16)
    sl = (s - sh.astype(F32)).astype(BF16)
    w = w_ref[...]
    wh = w.astype(BF16)
    wl = (w - wh.astype(F32)).astype(BF16)
    nrow = c.shape[0]
    both = jnp.dot(jnp.concatenate([sh, sl], axis=0), wh, preferred_element_type=F32)
    o_ref[...] = both[:nrow] + both[nrow:] + jnp.dot(sh, wl, preferred_element_type=F32) + b_ref[...]


def _ada(c_all, ada_w, ada_b):
    depth = ada_w.shape[0]
    nrow = c_all.shape[0]
    tn = 1024
    ncol = N_MOD * D_MODEL
    return pl.pallas_call(
        _ada_kernel,
        grid=(depth, ncol // tn),
        in_specs=[
            pl.BlockSpec((nrow, D_MODEL), lambda l, j: (0, 0)),
            pl.BlockSpec((None, D_MODEL, tn), lambda l, j: (l, 0, j)),
            pl.BlockSpec((None, 1, tn), lambda l, j: (l, 0, j)),
        ],
        out_specs=pl.BlockSpec((None, nrow, tn), lambda l, j: (l, 0, j)),
        out_shape=jax.ShapeDtypeStruct((depth, nrow, ncol), F32),
        compiler_params=_cparams(("arbitrary", "arbitrary")),
        name="ada",
    )(c_all, ada_w, ada_b)


KIN_TM = 1024
KIN_TN = 512
KIN_Q_TILES = NA_WIDTH // KIN_TN
KIN_QK_TILES = 2 * KIN_Q_TILES
KIN_QKV_TILES = 3 * NA_WIDTH // KIN_TN
KIN_FN_TILE = KIN_QKV_TILES
KIN_HY_TILES = 3 * HY_WIDTH // KIN_TN


KNORM_TM = 512


def _modnorm(x, g, shift, scale):
    ms = jnp.mean(x * x, axis=-1, keepdims=True)
    return x * lax.rsqrt(ms + EPS) * g * (1.0 + scale) + shift


def _knorm_kernel(x_ref, mod_ref, g_ref, o_ref):
    o_ref[...] = _modnorm(x_ref[...], g_ref[...], mod_ref[0:1, :], mod_ref[1:2, :]).astype(BF16)


def _knorm(x, mod, g, seq_len):
    t = x.shape[0]
    tm = KNORM_TM
    return pl.pallas_call(
        _knorm_kernel,
        grid=(t // tm,),
        in_specs=[
            pl.BlockSpec((tm, D_MODEL), lambda i: (i, 0)),
            pl.BlockSpec((None, N_MOD, D_MODEL), lambda i: ((i * tm) // seq_len, 0, 0)),
            pl.BlockSpec((1, D_MODEL), lambda i: (0, 0)),
        ],
        out_specs=pl.BlockSpec((tm, D_MODEL), lambda i: (i, 0)),
        out_shape=jax.ShapeDtypeStruct((t, D_MODEL), BF16),
        compiler_params=_cparams(("arbitrary",)),
        name="knorm",
    )(x, mod, g)


def _head_rms(acc, gain):
    outs = []
    for hh in range(KIN_TN // HEAD_DIM):
        a = acc[:, hh * HEAD_DIM:(hh + 1) * HEAD_DIM]
        outs.append(a * lax.rsqrt(jnp.mean(a * a, axis=-1, keepdims=True) + EPS) * gain)
    return jnp.concatenate(outs, axis=-1)


def _kin_kernel(h_ref, w_ref, qg_ref, kg_ref, oqk_ref, ov_ref, ou_ref, raw):
    i = pl.program_id(0)
    j = pl.program_id(1)

    @pl.when((i == 0) & (j == 0))
    def _():
        raw[1] = jnp.zeros(raw.shape[1:], F32)

    gain = jnp.where(j - 1 < KIN_Q_TILES, qg_ref[...] * (HEAD_DIM ** -0.5 * LOG2E), kg_ref[...])

    def step(wslot, rslot):
        acc = jnp.dot(h_ref[...], w_ref[...], preferred_element_type=F32)
        oqk_ref[...] = _head_rms(raw[rslot], gain).astype(BF16)
        raw[wslot] = acc
        ov_ref[...] = acc.astype(BF16)
        ou_ref[...] = acc

    @pl.when(j % 2 == 0)
    def _():
        step(0, 1)

    @pl.when(j % 2 == 1)
    def _():
        step(1, 0)


KIN_U_HY_COL = 0
KIN_U_FN_COL = KIN_HY_TILES


def _kin(h, w, l, qg, kg):
    t = h.shape[0]
    tm = KIN_TM
    nj = IN_PROJ // KIN_TN
    v_spare = KIN_Q_TILES
    u_spare = KIN_HY_TILES + 1

    def v_col(j):
        return jnp.where(j < KIN_QK_TILES, v_spare, jnp.where(j < KIN_QKV_TILES, j - KIN_QK_TILES, v_spare + 1))

    def u_col(j):
        return jnp.where(j > KIN_FN_TILE, j - KIN_FN_TILE - 1 + KIN_U_HY_COL,
                         jnp.where(j == KIN_FN_TILE, KIN_U_FN_COL, u_spare))

    return pl.pallas_call(
        _kin_kernel,
        grid=(t // tm, nj),
        in_specs=[
            pl.BlockSpec((tm, D_MODEL), lambda i, j: (i, 0)),
            pl.BlockSpec((None, D_MODEL, KIN_TN), lambda i, j: (l, 0, j)),
            pl.BlockSpec((1, HEAD_DIM), lambda i, j: (0, 0)),
            pl.BlockSpec((1, HEAD_DIM), lambda i, j: (0, 0)),
        ],
        out_specs=[
            pl.BlockSpec((tm, KIN_TN), lambda i, j: (i, jnp.clip(j - 1, 0, KIN_QK_TILES))),
            pl.BlockSpec((tm, KIN_TN), lambda i, j: (i, v_col(j))),
            pl.BlockSpec((tm, KIN_TN), lambda i, j: (i, u_col(j))),
        ],
        out_shape=[
            jax.ShapeDtypeStruct((t, (KIN_QK_TILES + 1) * KIN_TN), BF16),
            jax.ShapeDtypeStruct((t, (v_spare + 2) * KIN_TN), BF16),
            jax.ShapeDtypeStruct((t, (u_spare + 1) * KIN_TN), F32),
        ],
        scratch_shapes=[pltpu.VMEM((2, tm, KIN_TN), F32)],
        compiler_params=_cparams(("arbitrary", "arbitrary")),
        name="kin",
    )(h, w, qg, kg)


ATT_R = 4
ATT_KW = ATT_R + NA_KH


def _attn_variant_tiles(rows, variant):
    if variant == 0:
        r0, ws = 0, 0
    elif variant == 1:
        r0 = ATT_R
        ws = r0 - NA_KH // 2
    else:
        r0, ws = rows - ATT_R, rows - ATT_KW
    tiles = []
    for a in range(ATT_R):
        r = r0 + a
        rs = min(max(r - NA_KH // 2, 0), rows - NA_KH)
        row = []
        for jp in range(ATT_KW // 2):
            kr = ws + 2 * jp
            valid_l = rs <= kr < rs + NA_KH
            valid_r = rs <= kr + 1 < rs + NA_KH
            d_left = kr - r + (NA_KH - 1)
            row.append((valid_l, valid_r, d_left))
        tiles.append(row)
    return tiles


def _attn_build_bias(rpb_ref, t2, bias, h, rows):
    n_rd = 2 * NA_KH - 1
    n_cd = 2 * NA_KW - 1
    cidx = lax.broadcasted_iota(jnp.int32, (GRID_W, LANES), 0)
    lane = lax.broadcasted_iota(jnp.int32, (GRID_W, LANES), 1)
    widx = lane & (GRID_W - 1)
    left = lane < GRID_W
    dcidx = jnp.clip(widx - cidx, -(NA_KW - 1), NA_KW - 1) + (NA_KW - 1)
    base = h * (n_rd * n_cd)
    for dp in range(n_rd + 1):
        acc = jnp.zeros((GRID_W, LANES), F32)
        for e in range(n_cd):
            vl = rpb_ref[base + (dp - 1) * n_cd + e] if dp - 1 >= 0 else 0.0
            vr = rpb_ref[base + dp * n_cd + e] if dp < n_rd else 0.0
            acc = jnp.where(dcidx == e, jnp.where(left, vl, vr), acc)
        t2[dp] = acc * LOG2E

    c_lo = jnp.clip(cidx - NA_KW // 2, 0, GRID_W - NA_KW)
    col_ok = (widx >= c_lo) & (widx < c_lo + NA_KW)
    for variant in range(3):
        tiles = _attn_variant_tiles(rows, variant)
        for a in range(ATT_R):
            for jp in range(ATT_KW // 2):
                valid_l, valid_r, d_left = tiles[a][jp]
                if not (valid_l or valid_r):
                    tile = jnp.full((GRID_W, LANES), NEG, F32)
                else:
                    ok = col_ok
                    if not valid_l:
                        ok = ok & (~left)
                    if not valid_r:
                        ok = ok & left
                    tile = jnp.where(ok, t2[d_left + 1], NEG)
                bias[variant, a * GRID_W:(a + 1) * GRID_W, jp * LANES:(jp + 1) * LANES] = tile


def _attn_kernel(rpb_ref, q_ref, k_ref, v_ref, o_ref, t2, bias, s_buf, p_buf, rden_buf, *, seq_len):
    rows = seq_len // GRID_W
    nb = rows // ATT_R

    @pl.when(pl.program_id(1) == 0)
    def _():
        _attn_build_bias(rpb_ref, t2, bias, pl.program_id(0), rows)

    def q_rows(rb):
        if isinstance(rb, int):
            return pl.ds(rb * (ATT_R * GRID_W), ATT_R * GRID_W)
        return pl.ds(pl.multiple_of(rb * (ATT_R * GRID_W), ATT_R * GRID_W), ATT_R * GRID_W)

    def k_rows(rb):
        if isinstance(rb, int):
            return pl.ds(min(max(rb * ATT_R - NA_KH // 2, 0), rows - ATT_KW) * GRID_W, ATT_KW * GRID_W)
        ws = jnp.clip(rb * ATT_R - NA_KH // 2, 0, rows - ATT_KW)
        return pl.ds(pl.multiple_of(ws * GRID_W, GRID_W), ATT_KW * GRID_W)

    def scores(rb, slot):
        if isinstance(rb, int):
            variant = 0 if rb == 0 else (2 if rb == nb - 1 else 1)
        else:
            variant = jnp.where(rb == 0, 0, jnp.where(rb == nb - 1, 2, 1))
        s = lax.dot_general(q_ref[q_rows(rb), :], k_ref[k_rows(rb), :], (((1,), (1,)), ((), ())),
                            preferred_element_type=F32)
        s_buf[slot] = s + bias[variant]

    def softmax(slot):
        s = s_buf[slot]
        m = jnp.max(s, axis=-1, keepdims=True)
        p = jnp.exp2(s - m)
        rden_buf[slot] = 1.0 / jnp.sum(p, axis=-1, keepdims=True)
        p_buf[slot] = p.astype(BF16)

    def values(rb, slot):
        o = jnp.dot(p_buf[slot], v_ref[k_rows(rb), :], preferred_element_type=F32)
        o_ref[q_rows(rb), :] = o * rden_buf[slot]

    scores(0, 0)
    scores(1, 1)
    softmax(0)

    def body(v, carry):
        t = 2 * v
        values(t, 0)
        softmax(1)
        scores(t + 2, 0)
        values(t + 1, 1)
        softmax(0)
        scores(t + 3, 1)
        return carry

    lax.fori_loop(0, nb // 2 - 1, body, 0)
    values(nb - 2, 0)
    softmax(1)
    values(nb - 1, 1)


def _attn(qk, v, rpb_flat, bsz, seq_len):
    t = qk.shape[0]
    kernel = functools.partial(_attn_kernel, seq_len=seq_len)
    return pl.pallas_call(
        kernel,
        grid=(NA_HEADS, bsz),
        in_specs=[
            pl.BlockSpec(memory_space=pltpu.SMEM),
            pl.BlockSpec((seq_len, HEAD_DIM), lambda h, b: (b, h)),
            pl.BlockSpec((seq_len, HEAD_DIM), lambda h, b: (b, NA_HEADS + h)),
            pl.BlockSpec((seq_len, HEAD_DIM), lambda h, b: (b, h)),
        ],
        out_specs=pl.BlockSpec((seq_len, HEAD_DIM), lambda h, b: (b, h)),
        out_shape=jax.ShapeDtypeStruct((t, NA_WIDTH), F32),
        scratch_shapes=[
            pltpu.VMEM((2 * NA_KH, GRID_W, LANES), F32),
            pltpu.VMEM((3, ATT_R * GRID_W, ATT_KW * GRID_W), F32),
            pltpu.VMEM((2, ATT_R * GRID_W, ATT_KW * GRID_W), F32),
            pltpu.VMEM((2, ATT_R * GRID_W, ATT_KW * GRID_W), BF16),
            pltpu.VMEM((2, ATT_R * GRID_W, 1), F32),
        ],
        compiler_params=_cparams(("arbitrary", "arbitrary")),
        name="attn",
    )(rpb_flat, qk, qk, v)


DFT_TW = 8


def _kron_mat(m):
    eye = np.eye(DFT_TW, dtype=ml_dtypes.bfloat16)
    return jnp.asarray(np.kron(m.astype(ml_dtypes.bfloat16), eye))


BF16_ROWS = 16


def _dft1_kernel(m_ref, x_ref, twr_ref, twi_ref, o_ref):
    kd, rows, c = x_ref.shape
    mo = o_ref.shape[0]
    kh = mo // 2
    for g in range(rows // BF16_ROWS):
        re, im = [], []
        for s in range(g * BF16_ROWS // DFT_TW, (g + 1) * BF16_ROWS // DFT_TW):
            sl = slice(s * DFT_TW, (s + 1) * DFT_TW)
            x = x_ref[:, sl, :].reshape(kd * DFT_TW, c)
            y = jnp.dot(m_ref[...], x.astype(BF16), preferred_element_type=F32).reshape(mo, DFT_TW, c)
            twr = _lane_tile(twr_ref[:, sl, :], c)
            twi = _lane_tile(twi_ref[:, sl, :], c)
            yr = y[:kh]
            yi = y[kh:]
            re.append(yr * twr - yi * twi)
            im.append(yr * twi + yi * twr)
        gl = slice(g * BF16_ROWS, (g + 1) * BF16_ROWS)
        o_ref[:kh, gl, :] = jnp.concatenate(re, axis=1).astype(BF16)
        o_ref[kh:, gl, :] = jnp.concatenate(im, axis=1).astype(BF16)


def _dft1(mk, x, tw, c=None, col=0):
    bsz, kd, n2, width = x.shape
    c = width if c is None else c
    mo = mk.shape[0] // DFT_TW
    rows = 2 * BF16_ROWS if c <= FN_WIDTH else BF16_ROWS
    return pl.pallas_call(
        _dft1_kernel,
        grid=(bsz, n2 // rows),
        in_specs=[
            pl.BlockSpec(mk.shape, lambda b, j: (0, 0)),
            pl.BlockSpec((None, kd, rows, c), lambda b, j: (b, 0, j, col)),
            pl.BlockSpec((mo // 2, rows, LANES), lambda b, j: (0, j, 0)),
            pl.BlockSpec((mo // 2, rows, LANES), lambda b, j: (0, j, 0)),
        ],
        out_specs=pl.BlockSpec((None, mo, rows, c), lambda b, j: (b, 0, j, 0)),
        out_shape=jax.ShapeDtypeStruct((bsz, mo, n2, c), BF16),
        compiler_params=_cparams(("arbitrary", "arbitrary")),
        name="dft1",
    )(mk, x, *tw)


def _stage1_fwd_mat(n1, kd, scale=1.0, rows=None):
    c, s = _cos_sin(n1 if rows is None else rows, kd, n1)
    return _kron_mat(np.concatenate([c, -s], axis=0) * scale)


def _stage2_mats(n2):
    c, s = _cos_sin(n2, n2, n2)
    fwd = np.block([[c, s], [-s, c]])
    inv = np.block([[c, -s], [s, c]])
    return jnp.asarray(fwd.astype(ml_dtypes.bfloat16)), jnp.asarray(inv.astype(ml_dtypes.bfloat16))


def _half_rows(n1):
    return n1 // 2 + DFT_TW


def _twiddle(n1, n2, rows=None):
    rows = n1 if rows is None else rows
    ang = 2.0 * np.pi * np.outer(np.arange(rows), np.arange(n2)) / (n1 * n2)
    twr = jnp.asarray(np.cos(ang).astype(np.float32))
    twi = jnp.asarray((-np.sin(ang)).astype(np.float32))
    shape = (rows, n2, LANES)
    return jnp.broadcast_to(twr[:, :, None], shape), jnp.broadcast_to(twi[:, :, None], shape)


def _lane_tile(x, width):
    reps = width // x.shape[-1]
    return x if reps == 1 else jnp.concatenate([x] * reps, axis=-1)


FFT_KC = 8


def _fnw_kernel(cs_ref, w_ref, o_ref):
    o_ref[...] = jnp.dot(cs_ref[...], w_ref[...], precision=HIGHEST, preferred_element_type=F32)


def _fnw(fn_w):
    c, s = _cos_sin(FN_GROUP_DIM, FN_GROUP_DIM, FN_GROUP_DIM)
    cs = jnp.asarray(np.concatenate([c, s], axis=0).astype(np.float32))
    return pl.pallas_call(
        _fnw_kernel,
        grid=(FN_GROUPS,),
        in_specs=[
            pl.BlockSpec((2 * FN_GROUP_DIM, FN_GROUP_DIM), lambda g: (0, 0)),
            pl.BlockSpec((None, FN_GROUP_DIM, FN_GROUP_DIM), lambda g: (g, 0, 0)),
        ],
        out_specs=pl.BlockSpec((None, 2 * FN_GROUP_DIM, FN_GROUP_DIM), lambda g: (g, 0, 0)),
        out_shape=jax.ShapeDtypeStruct((FN_GROUPS, 2 * FN_GROUP_DIM, FN_GROUP_DIM), F32),
        compiler_params=_cparams(("arbitrary",)),
        name="fnw",
    )(cs, fn_w)


def _fn2_kernel(y_ref, f2_ref, g_ref, b_ref, o_ref):
    n2 = DFT_N2
    for t in range(FFT_KC):
        rhs = jnp.concatenate([y_ref[0, t], y_ref[1, t]], axis=0)
        z = jnp.dot(f2_ref[...], rhs, preferred_element_type=F32)
        zr = z[:n2]
        zi = z[n2:]
        outs = []
        for g in range(FN_GROUPS):
            sl = slice(g * FN_GROUP_DIM, (g + 1) * FN_GROUP_DIM)
            zc = jnp.concatenate([zr[:, sl], zi[:, sl]], axis=1).astype(BF16)
            outs.append(jnp.dot(zc, g_ref[g].astype(BF16), preferred_element_type=F32))
        o_ref[:, t, :] = jnp.concatenate(outs, axis=1) + b_ref[...]


def _fourier(u, gcat, fn_b, bsz, seq_len):
    n2 = DFT_N2
    n1 = seq_len // n2
    m1 = _stage1_fwd_mat(n1, n1, scale=1.0 / math.sqrt(seq_len * FN_GROUP_DIM))
    y = _dft1(m1, u.reshape(bsz, n1, n2, u.shape[-1]), _twiddle(n1, n2), c=FN_WIDTH, col=KIN_U_FN_COL)
    y = y.reshape(bsz, 2, n1, n2, FN_WIDTH)
    f2, _ = _stage2_mats(n2)
    out = pl.pallas_call(
        _fn2_kernel,
        grid=(n1 // FFT_KC, bsz),
        in_specs=[
            pl.BlockSpec((None, 2, FFT_KC, n2, FN_WIDTH), lambda k, b: (b, 0, k, 0, 0)),
            pl.BlockSpec(f2.shape, lambda k, b: (0, 0)),
            pl.BlockSpec((FN_GROUPS, 2 * FN_GROUP_DIM, FN_GROUP_DIM), lambda k, b: (0, 0, 0)),
            pl.BlockSpec((1, FN_WIDTH), lambda k, b: (0, 0)),
        ],
        out_specs=pl.BlockSpec((None, n2, FFT_KC, FN_WIDTH), lambda k, b: (b, 0, k, 0)),
        out_shape=jax.ShapeDtypeStruct((bsz, n2, n1, FN_WIDTH), F32),
        compiler_params=_cparams(("arbitrary", "arbitrary")),
        name="fn2",
    )(y, f2, gcat, fn_b)
    return out.reshape(bsz * seq_len, FN_WIDTH)


HPRE_TM = 512


def _hpre_kernel(up_ref, um_ref, un_ref, cw_ref, cb_ref, z_ref, x0_ref, *, seq_len):
    tm = HPRE_TM
    i = pl.program_id(0)
    at_start = (i * tm) % seq_len == 0
    at_end = ((i + 1) * tm) % seq_len == 0
    u = um_ref[...]
    r = lax.broadcasted_iota(jnp.int32, (tm, 1), 0)
    prev_row = jnp.where(at_start, 0.0, up_ref[7:8, :])
    next_row = jnp.where(at_end, 0.0, un_ref[0:1, :])
    upv = jnp.where(r == 0, prev_row, pltpu.roll(u, 1, 0))
    dnv = jnp.where(r == tm - 1, next_row, pltpu.roll(u, tm - 1, 0))
    y = upv * cw_ref[0:1, :] + u * cw_ref[1:2, :] + dnv * cw_ref[2:3, :] + cb_ref[...]
    z_ref[...] = y[:, 2 * HY_WIDTH:] * y[:, HY_WIDTH:2 * HY_WIDTH]
    x0_ref[...] = y[:, :HY_WIDTH]


def _hpre(u, cw, cb, seq_len):
    t = u.shape[0]
    tm = HPRE_TM
    w = 3 * HY_WIDTH
    nb8 = t // 8
    return pl.pallas_call(
        functools.partial(_hpre_kernel, seq_len=seq_len),
        grid=(t // tm,),
        in_specs=[
            pl.BlockSpec((8, w), lambda i: (jnp.maximum(i * (tm // 8) - 1, 0), 0)),
            pl.BlockSpec((tm, w), lambda i: (i, 0)),
            pl.BlockSpec((8, w), lambda i: (jnp.minimum((i + 1) * (tm // 8), nb8 - 1), 0)),
            pl.BlockSpec((3, w), lambda i: (0, 0)),
            pl.BlockSpec((1, w), lambda i: (0, 0)),
        ],
        out_specs=[
            pl.BlockSpec((tm, HY_WIDTH), lambda i: (i, 0)),
            pl.BlockSpec((tm, HY_WIDTH), lambda i: (i, 0)),
        ],
        out_shape=[jax.ShapeDtypeStruct((t, HY_WIDTH), F32), jax.ShapeDtypeStruct((t, HY_WIDTH), F32)],
        compiler_params=_cparams(("arbitrary",)),
        name="hpre",
    )(u, u, u, cw, cb)


HFILT_TL = 512
HY_EMB_PAD = 128


def _hfilt_kernel(z_ref, w1_ref, b1_ref, f1_ref, w2_ref, b2_ref, f2_ref, w3_ref, b3_ref, dl_ref, o_ref):
    z = z_ref[...]
    h = jnp.sin(f1_ref[...] * (jnp.dot(z, w1_ref[...], precision=HIGHEST, preferred_element_type=F32) + b1_ref[...]))
    h = jnp.sin(f2_ref[...] * (jnp.dot(h, w2_ref[...], precision=HIGHEST, preferred_element_type=F32) + b2_ref[...]))
    h = jnp.dot(h, w3_ref[...], precision=HIGHEST, preferred_element_type=F32) + b3_ref[...]
    t = z[:, 0:1]
    h = h * jnp.exp(-t * dl_ref[...])
    row = lax.broadcasted_iota(jnp.int32, h.shape, 0) + pl.program_id(0) * HFILT_TL
    col = lax.broadcasted_iota(jnp.int32, h.shape, 1)
    o_ref[...] = jnp.where((row == 0) & (col >= HY_WIDTH), 0.0, h)


def _hyena_emb_np(seq_len):
    t = np.linspace(0.0, 1.0, seq_len)[:, None]
    bands = (HY_EMB_DIM - 1) // 2
    w = 2.0 * np.pi * np.arange(seq_len)[:, None] / seq_len
    fr = np.linspace(1e-4, bands - 1, bands)[None, :]
    z = np.concatenate([t, np.cos(fr * w), -np.sin(fr * w)], axis=-1)
    out = np.zeros((seq_len, HY_EMB_PAD), np.float32)
    out[:, :HY_EMB_DIM] = z
    return out


def _hfilt(seq_len, w1, b1, f1, w2, b2, f2, w3, b3):
    zemb = jnp.asarray(_hyena_emb_np(seq_len))
    max_decay = math.log(HY_TARGET) / HY_FAST_DECAY
    min_decay = math.log(HY_TARGET) / HY_SLOW_DECAY
    deltas = np.abs(np.linspace(min_decay, max_decay, HY_WIDTH))
    dl = jnp.asarray(np.tile(deltas, 2)[None, :].astype(np.float32))
    w1p = jnp.pad(w1, ((0, HY_EMB_PAD - HY_EMB_DIM), (0, 0)))
    tl = HFILT_TL
    full = lambda shape: pl.BlockSpec(shape, lambda i: (0,) * len(shape))
    return pl.pallas_call(
        _hfilt_kernel,
        grid=(seq_len // tl,),
        in_specs=[
            pl.BlockSpec((tl, HY_EMB_PAD), lambda i: (i, 0)),
            full((HY_EMB_PAD, HY_HIDDEN)), full((1, HY_HIDDEN)), full((1, HY_HIDDEN)),
            full((HY_HIDDEN, HY_HIDDEN)), full((1, HY_HIDDEN)), full((1, HY_HIDDEN)),
            full((HY_HIDDEN, 2 * HY_WIDTH)), full((1, 2 * HY_WIDTH)), full((1, 2 * HY_WIDTH)),
        ],
        out_specs=pl.BlockSpec((tl, 2 * HY_WIDTH), lambda i: (i, 0)),
        out_shape=jax.ShapeDtypeStruct((seq_len, 2 * HY_WIDTH), F32),
        compiler_params=_cparams(("arbitrary",)),
        name="hfilt",
    )(zemb, w1p, b1, f1, w2, b2, f2, w3, b3, dl)


def _hk2_kernel(y_ref, f2_ref, o_ref):
    n2 = DFT_N2
    for t in range(FFT_KC):
        rhs = jnp.concatenate([y_ref[0, t], y_ref[1, t]], axis=0)
        z = jnp.dot(f2_ref[...], rhs, preferred_element_type=F32)
        zr = z[:n2]
        zi = z[n2:]
        o_ref[0, t] = zr[:, :HY_WIDTH] + zr[:, HY_WIDTH:]
        o_ref[1, t] = zi[:, :HY_WIDTH] - zi[:, HY_WIDTH:]


def _hy2_kernel(y_ref, kf_ref, twr_ref, twi_ref, f2_ref, f2i_ref, o_ref):
    n2 = DFT_N2
    for t in range(FFT_KC):
        rhs = jnp.concatenate([y_ref[0, t], y_ref[1, t]], axis=0)
        z = jnp.dot(f2_ref[...], rhs, preferred_element_type=F32)
        zr = z[:n2]
        zi = z[n2:]
        kr = kf_ref[0, t]
        ki = kf_ref[1, t]
        pr = zr * kr - zi * ki
        pi = zr * ki + zi * kr
        rhs = jnp.concatenate([pr, pi], axis=0).astype(BF16)
        g = jnp.dot(f2i_ref[...], rhs, preferred_element_type=F32)
        gr = g[:n2]
        gi = g[n2:]
        twr = _lane_tile(twr_ref[t], HY_WIDTH)
        twi = _lane_tile(twi_ref[t], HY_WIDTH)
        o_ref[0, t] = (gr * twr + gi * twi).astype(BF16)
        o_ref[1, t] = (gi * twr - gr * twi).astype(BF16)


def _hy3_kernel(m_ref, g_ref, z_ref, x0_ref, d_ref, o_ref):
    kd, rows, c = g_ref.shape
    mo = o_ref.shape[0]
    g = g_ref[...].astype(F32)
    for s in range(rows // DFT_TW):
        sl = slice(s * DFT_TW, (s + 1) * DFT_TW)
        rhs = g[:, sl, :].reshape(kd * DFT_TW, c).astype(BF16)
        conv = jnp.dot(m_ref[...], rhs, preferred_element_type=F32).reshape(mo, DFT_TW, c)
        o_ref[:, sl, :] = (conv + z_ref[:, sl, :] * d_ref[...]) * x0_ref[:, sl, :]


def _hyena_filter_spectrum(seq_len, w1, b1, f1, w2, b2, f2, w3, b3):
    n = 2 * seq_len
    n2 = DFT_N2
    n1 = n // n2
    kh = _half_rows(n1)
    h = _hfilt(seq_len, w1, b1, f1, w2, b2, f2, w3, b3)
    m1 = _stage1_fwd_mat(n1, n1 // 2, rows=kh)
    y = _dft1(m1, h.reshape(1, n1 // 2, n2, 2 * HY_WIDTH), _twiddle(n1, n2, kh))
    y = y.reshape(2, kh, n2, 2 * HY_WIDTH)
    f2m, _ = _stage2_mats(n2)
    return pl.pallas_call(
        _hk2_kernel,
        grid=(kh // FFT_KC,),
        in_specs=[
            pl.BlockSpec((2, FFT_KC, n2, 2 * HY_WIDTH), lambda k: (0, k, 0, 0)),
            pl.BlockSpec(f2m.shape, lambda k: (0, 0)),
        ],
        out_specs=pl.BlockSpec((2, FFT_KC, n2, HY_WIDTH), lambda k: (0, k, 0, 0)),
        out_shape=jax.ShapeDtypeStruct((2, kh, n2, HY_WIDTH), F32),
        compiler_params=_cparams(("arbitrary",)),
        name="hk2",
    )(y, f2m)


def _hyena_conv(z, x0, kf, d_skip, bsz, seq_len):
    n = 2 * seq_len
    n2 = DFT_N2
    n1 = n // n2
    kh = _half_rows(n1)
    zv = z.reshape(bsz, n1 // 2, n2, HY_WIDTH)
    tw = _twiddle(n1, n2, kh)
    y = _dft1(_stage1_fwd_mat(n1, n1 // 2, rows=kh), zv, tw).reshape(bsz, 2, kh, n2, HY_WIDTH)
    f2m, f2i = _stage2_mats(n2)
    g = pl.pallas_call(
        _hy2_kernel,
        grid=(kh // FFT_KC, bsz),
        in_specs=[
            pl.BlockSpec((None, 2, FFT_KC, n2, HY_WIDTH), lambda k, b: (b, 0, k, 0, 0)),
            pl.BlockSpec((2, FFT_KC, n2, HY_WIDTH), lambda k, b: (0, k, 0, 0)),
            pl.BlockSpec((FFT_KC, n2, LANES), lambda k, b: (k, 0, 0)),
            pl.BlockSpec((FFT_KC, n2, LANES), lambda k, b: (k, 0, 0)),
            pl.BlockSpec(f2m.shape, lambda k, b: (0, 0)),
            pl.BlockSpec(f2i.shape, lambda k, b: (0, 0)),
        ],
        out_specs=pl.BlockSpec((None, 2, FFT_KC, n2, HY_WIDTH), lambda k, b: (b, 0, k, 0, 0)),
        out_shape=jax.ShapeDtypeStruct((bsz, 2, kh, n2, HY_WIDTH), BF16),
        compiler_params=_cparams(("arbitrary", "arbitrary")),
        name="hy2",
    )(y, kf, tw[0], tw[1], f2m, f2i)
    c, s = _cos_sin(n1 // 2, kh, n1)
    wt = np.where(np.arange(kh) < n1 // 2, 2.0, 0.0)
    wt[0] = 1.0
    wt[n1 // 2] = 1.0
    m3 = _kron_mat(np.concatenate([c * wt, -s * wt], axis=1) / n)
    rows = BF16_ROWS
    half =pl.BlockSpec((None, n1 // 2, rows, HY_WIDTH), lambda b, j: (b, 0, j, 0))
    out = pl.pallas_call(
        _hy3_kernel,
        grid=(bsz, n2 // rows),
        in_specs=[
            pl.BlockSpec(m3.shape, lambda b, j: (0, 0)),
            pl.BlockSpec((None, 2 * kh, rows, HY_WIDTH), lambda b, j: (b, 0, j, 0)),
            half,
            half,
            pl.BlockSpec((1, HY_WIDTH), lambda b, j: (0, 0)),
        ],
        out_specs=half,
        out_shape=jax.ShapeDtypeStruct((bsz, n1 // 2, n2, HY_WIDTH), F32),
        compiler_params=_cparams(("arbitrary", "arbitrary")),
        name="hy3",
    )(m3, g.reshape(bsz, 2 * kh, n2, HY_WIDTH), zv, x0.reshape(bsz, n1 // 2, n2, HY_WIDTH), d_skip)
    return out.reshape(bsz * seq_len, HY_WIDTH)


KOUT_TM = 512
KOUT_SUB = 256


def _rms(y):
    return y * lax.rsqrt(jnp.mean(y * y, axis=-1, keepdims=True) + EPS)


def _kout_kernel(ya_ref, yb_ref, yc_ref, x_ref, mod_ref, og_ref, n2g_ref, w_ref, xo_ref, h2_ref):
    for s in range(KOUT_TM // KOUT_SUB):
        sl = slice(s * KOUT_SUB, (s + 1) * KOUT_SUB)
        n = jnp.concatenate([_rms(ya_ref[sl, :]), _rms(yb_ref[sl, :]), _rms(yc_ref[sl, :])], axis=-1) * og_ref[...]
        y = jnp.dot(n.astype(BF16), w_ref[...], preferred_element_type=F32)
        xn = x_ref[sl, :] + mod_ref[2:3, :] * y
        xo_ref[sl, :] = xn
        h2 = _rms(xn) * n2g_ref[...]
        h2_ref[sl, :] = (h2 * (1.0 + mod_ref[4:5, :]) + mod_ref[3:4, :]).astype(BF16)


def _kout(ya, yb, yc, x, mod, og, n2g, w, l, seq_len):
    t = x.shape[0]
    tm = KOUT_TM
    row = lambda width: pl.BlockSpec((tm, width), lambda i: (i, 0))
    return pl.pallas_call(
        _kout_kernel,
        grid=(t // tm,),
        in_specs=[
            row(NA_WIDTH), row(FN_WIDTH), row(HY_WIDTH), row(D_MODEL),
            pl.BlockSpec((None, N_MOD, D_MODEL), lambda i: ((i * tm) // seq_len, 0, 0)),
            pl.BlockSpec((1, D_MODEL), lambda i: (0, 0)),
            pl.BlockSpec((1, D_MODEL), lambda i: (0, 0)),
            pl.BlockSpec((None, D_MODEL, D_MODEL), lambda i: (l, 0, 0)),
        ],
        out_specs=[row(D_MODEL), row(D_MODEL)],
        out_shape=[jax.ShapeDtypeStruct((t, D_MODEL), F32), jax.ShapeDtypeStruct((t, D_MODEL), BF16)],
        compiler_params=_cparams(("arbitrary",)),
        name="kout",
    )(ya, yb, yc, x, mod, og, n2g, w)


MLP_TM = 512
MLP_TF = 512
MLP_HALO = 16


def _gelu_exact(a):
    return 0.5 * a * (1.0 + lax.erf(a * (1.0 / math.sqrt(2.0))))


def _mlp_kernel(hp_ref, hm_ref, hn_ref, x_ref, mod_ref, wa_ref, wg_ref, cwb_ref, wd_ref, *rest,
                seq_len, emit_next):
    if emit_next:
        modn_ref, gn_ref, o_ref, hnext_ref, hext, acc = rest
    else:
        o_ref, hext, acc = rest
    tm = MLP_TM
    halo = MLP_HALO
    i = pl.program_id(0)
    j = pl.program_id(1)

    last = pl.num_programs(1) - 1

    def chunk():
        a_ext = jnp.dot(hext[...], wa_ref[...], preferred_element_type=F32)
        mid = a_ext[0:tm]
        r = lax.broadcasted_iota(jnp.int32, (tm, 1), 0)
        up = jnp.where(r == 0, a_ext[tm:tm + 1], pltpu.roll(mid, 1, 0))
        dn = jnp.where(r == tm - 1, a_ext[tm + 1:tm + 2], pltpu.roll(mid, tm - 1, 0))
        a = up * cwb_ref[0:1, :] + mid * cwb_ref[1:2, :] + dn * cwb_ref[2:3, :] + cwb_ref[3:4, :]
        gate = jnp.dot(hext[0:tm, :], wg_ref[...], preferred_element_type=F32)
        act = (_gelu_exact(a) * gate).astype(BF16)
        return jnp.dot(act, wd_ref[...], preferred_element_type=F32)

    @pl.when(j == 0)
    def _():
        at_start = (i * tm) % seq_len == 0
        at_end = ((i + 1) * tm) % seq_len == 0
        hext[0:tm, :] = hm_ref[...]
        r16 = lax.broadcasted_iota(jnp.int32, (halo, 1), 0)
        prev = pltpu.roll(hp_ref[...].astype(F32), 1, 0)
        nxt = pltpu.roll(hn_ref[...].astype(F32), 1, 0)
        extra = jnp.where((r16 == 0) & jnp.logical_not(at_start), prev,
                          jnp.where((r16 == 1) & jnp.logical_not(at_end), nxt, 0.0))
        hext[tm:, :] = extra.astype(BF16)
        acc[...] = chunk()

    @pl.when((j > 0) & (j < last))
    def _():
        acc[...] += chunk()

    @pl.when(j == last)
    def _():
        xo = x_ref[...] + mod_ref[5:6, :] * (acc[...] + chunk())
        o_ref[...] = xo
        if emit_next:
            hnext_ref[...] = _modnorm(xo, gn_ref[...], modn_ref[0:1, :], modn_ref[1:2, :]).astype(BF16)


def _mlp(h2, x, mod, w_up, cwb, w_down, l, seq_len, next_norm=None):
    t = x.shape[0]
    tm, tf, halo = MLP_TM, MLP_TF, MLP_HALO
    nf = D_FF // tf
    nbh = t // halo
    emit_next = next_norm is not None
    row = pl.BlockSpec((tm, D_MODEL), lambda i, j: (i, 0))
    modspec = pl.BlockSpec((None, N_MOD, D_MODEL), lambda i, j: ((i * tm) // seq_len, 0, 0))
    in_specs = [
        pl.BlockSpec((halo, D_MODEL), lambda i, j: (jnp.maximum(i * (tm // halo) - 1, 0), 0)),
        row,
        pl.BlockSpec((halo, D_MODEL), lambda i, j: (jnp.minimum((i + 1) * (tm // halo), nbh - 1), 0)),
        row,
        modspec,
        pl.BlockSpec((None, D_MODEL, tf), lambda i, j: (l, 0, j)),
        pl.BlockSpec((None, D_MODEL, tf), lambda i, j: (l, 0, nf + j)),
        pl.BlockSpec((SHORT_CONV + 1, tf), lambda i, j: (0, j)),
        pl.BlockSpec((None, tf, D_MODEL), lambda i, j: (l, j, 0)),
    ]
    args = [h2, h2, h2, x, mod, w_up, w_up, cwb, w_down]
    out_specs = [row]
    out_shape = [jax.ShapeDtypeStruct((t, D_MODEL), F32)]
    if emit_next:
        in_specs += [modspec, pl.BlockSpec((1, D_MODEL), lambda i, j: (0, 0))]
        args += list(next_norm)
        out_specs.append(row)
        out_shape.append(jax.ShapeDtypeStruct((t, D_MODEL), BF16))
    return pl.pallas_call(
        functools.partial(_mlp_kernel, seq_len=seq_len, emit_next=emit_next),
        grid=(t // tm, nf),
        in_specs=in_specs,
        out_specs=out_specs,
        out_shape=out_shape,
        scratch_shapes=[pltpu.VMEM((tm + halo, D_MODEL), BF16), pltpu.VMEM((tm, D_MODEL), F32)],
        compiler_params=_cparams(("arbitrary", "arbitrary")),
        name="mlp",
    )(*args)


def _layer(x, h, mod, p, l, bsz, seq_len, kf, next_norm):
    qk, v, u = _kin(h, p["w_in"], l, p["q_norm_g"][l][None], p["k_norm_g"][l][None])
    ya = _attn(qk, v, p["na_rpb"][l].reshape(-1), bsz, seq_len)
    yb = _fourier(u, p["gcat"][l], p["fn_b"][l][None], bsz, seq_len)
    z, x0 = _hpre(u, p["hy_conv_w"][l], p["hy_conv_b"][l][None], seq_len)
    yc = _hyena_conv(z, x0, kf, p["hy_d"][l][None], bsz, seq_len)
    x1, h2 = _kout(ya, yb, yc, x, mod, p["out_norm_g"][l][None], p["norm2_g"][l][None], p["w_out"], l, seq_len)
    out = _mlp(h2, x1, mod, p["mlp_w_up"], jnp.concatenate([p["mlp_conv_w"][l], p["mlp_conv_b"][l][None]], axis=0), p["mlp_w_down"], l,
               seq_len, next_norm)
    return (out[0], out[1]) if next_norm is not None else (out[0], None)


def kernel(x_prompt, x_sample, c_prompt, c_sample, ada_w, ada_b, norm1_g, w_in, q_norm_g, k_norm_g, na_rpb, fn_w, fn_b, hy_conv_w, hy_conv_b, hy_w1, hy_b1, hy_f1, hy_w2, hy_b2, hy_f2, hy_w3, hy_b3, hy_d, out_norm_g, w_out, norm2_g, mlp_w_up, mlp_conv_w, mlp_conv_b, mlp_w_down):
    depth = ada_w.shape[0]
    groups = [(x_prompt, c_prompt), (x_sample, c_sample)]
    nseq = sum(c.shape[0] for _, c in groups)
    nrow = -(-nseq // 16) * 16
    c_all = jnp.concatenate([c for _, c in groups] + [jnp.zeros((nrow - nseq, D_MODEL), F32)], axis=0)
    mod_all = _ada(c_all, ada_w, ada_b[:, None, :]).reshape(depth, nrow, N_MOD, D_MODEL)

    p = dict(
        w_in=w_in.astype(BF16), q_norm_g=q_norm_g, k_norm_g=k_norm_g, na_rpb=na_rpb,
        fn_b=fn_b, hy_conv_w=hy_conv_w, hy_conv_b=hy_conv_b, hy_d=hy_d, out_norm_g=out_norm_g,
        w_out=w_out.astype(BF16), norm2_g=norm2_g, mlp_w_up=mlp_w_up.astype(BF16), mlp_conv_w=mlp_conv_w,
        mlp_conv_b=mlp_conv_b, mlp_w_down=mlp_w_down.astype(BF16),
        gcat=[_fnw(fn_w[l]) for l in range(depth)],
    )

    outs = []
    seq_off = 0
    kf_cache = {}
    for x, c in groups:
        bsz, seq_len, _ = x.shape
        xt = x.reshape(bsz * seq_len, D_MODEL)
        mods = [mod_all[l, seq_off:seq_off + bsz] for l in range(depth)]
        ht = _knorm(xt, mods[0], norm1_g[0][None], seq_len)
        for l in range(depth):
            if (l, seq_len) not in kf_cache:
                kf_cache[(l, seq_len)] = _hyena_filter_spectrum(
                    seq_len, hy_w1[l], hy_b1[l][None], hy_f1[l][None], hy_w2[l], hy_b2[l][None], hy_f2[l][None],
                    hy_w3[l], hy_b3[l][None])
            next_norm = (mods[l + 1], norm1_g[l + 1][None]) if l + 1 < depth else None
            xt, ht = _layer(xt, ht, mods[l], p, l, bsz, seq_len, kf_cache[(l, seq_len)], next_norm)
        outs.append(xt.reshape(bsz, seq_len, D_MODEL))
        seq_off += bsz
    return tuple(outs)
```

```python
import functools
import math

import ml_dtypes
import numpy as np
import jax
import jax.numpy as jnp
from jax import lax
from jax.experimental import pallas as pl
from jax.experimental.pallas import tpu as pltpu

F32 = jnp.float32
BF16 = jnp.bfloat16

D_MODEL = 2048
GRID_W = 64
HEAD_DIM = 128
NA_HEADS = 8
NA_WIDTH = NA_HEADS * HEAD_DIM
NA_KH = 8
NA_KW = 16
FN_GROUPS = 4
FN_GROUP_DIM = 128
FN_WIDTH = 512
HY_WIDTH = 512
HY_EMB_DIM = 33
HY_HIDDEN = 64
HY_FAST_DECAY = 0.3
HY_SLOW_DECAY = 1.5
HY_TARGET = 1e-2
IN_PROJ = 3 * NA_WIDTH + FN_WIDTH + 3 * HY_WIDTH
D_FF = 5632
SHORT_CONV = 3
N_MOD = 6
EPS = 1e-6
NEG = -1e30
LOG2E = math.log2(math.e)

LANES = 128
DFT_N2 = 128
VMEM_LIMIT = 56 << 20

HIGHEST = lax.Precision.HIGHEST


def _cparams(sem):
    return pltpu.CompilerParams(dimension_semantics=sem, vmem_limit_bytes=VMEM_LIMIT)


def _cos_sin(n_out, n_in, period):
    ang = 2.0 * np.pi * np.outer(np.arange(n_out), np.arange(n_in)) / period
    return np.cos(ang), np.sin(ang)


def _ada_kernel(c_ref, w_ref, b_ref, o_ref):
    c = c_ref[...]
    s = c * (1.0 / (1.0 + jnp.exp(-c)))
    sh = s.astype(BF16)
    sl = (s - sh.astype(F32)).astype(BF16)
    w = w_ref[...]
    wh = w.astype(BF16)
    wl = (w - wh.astype(F32)).astype(BF16)
    nrow = c.shape[0]
    both = jnp.dot(jnp.concatenate([sh, sl], axis=0), wh, preferred_element_type=F32)
    o_ref[...] = both[:nrow] + both[nrow:] + jnp.dot(sh, wl, preferred_element_type=F32) + b_ref[...]


def _ada(c_all, ada_w, ada_b):
    depth = ada_w.shape[0]
    nrow = c_all.shape[0]
    tn = 1024
    ncol = N_MOD * D_MODEL
    return pl.pallas_call(
        _ada_kernel,
        grid=(depth, ncol // tn),
        in_specs=[
            pl.BlockSpec((nrow, D_MODEL), lambda l, j: (0, 0)),
            pl.BlockSpec((None, D_MODEL, tn), lambda l, j: (l, 0, j)),
            pl.BlockSpec((None, 1, tn), lambda l, j: (l, 0, j)),
        ],
        out_specs=pl.BlockSpec((None, nrow, tn), lambda l, j: (l, 0, j)),
        out_shape=jax.ShapeDtypeStruct((depth, nrow, ncol), F32),
        compiler_params=_cparams(("arbitrary", "arbitrary")),
        name="ada",
    )(c_all, ada_w, ada_b)


KIN_TM = 1024
KIN_TN = 512
KIN_Q_TILES = NA_WIDTH // KIN_TN
KIN_QK_TILES = 2 * KIN_Q_TILES
KIN_QKV_TILES = 3 * NA_WIDTH // KIN_TN
KIN_FN_TILE = KIN_QKV_TILES
KIN_HY_TILES = 3 * HY_WIDTH // KIN_TN


KNORM_TM = 512


def _modnorm(x, g, shift, scale):
    ms = jnp.mean(x * x, axis=-1, keepdims=True)
    return x * lax.rsqrt(ms + EPS) * g * (1.0 + scale) + shift


def _knorm_kernel(x_ref, mod_ref, g_ref, o_ref):
    o_ref[...] = _modnorm(x_ref[...], g_ref[...], mod_ref[0:1, :], mod_ref[1:2, :]).astype(BF16)


def _knorm(x, mod, g, seq_len):
    t = x.shape[0]
    tm = KNORM_TM
    return pl.pallas_call(
        _knorm_kernel,
        grid=(t // tm,),
        in_specs=[
            pl.BlockSpec((tm, D_MODEL), lambda i: (i, 0)),
            pl.BlockSpec((None, N_MOD, D_MODEL), lambda i: ((i * tm) // seq_len, 0, 0)),
            pl.BlockSpec((1, D_MODEL), lambda i: (0, 0)),
        ],
        out_specs=pl.BlockSpec((tm, D_MODEL), lambda i: (i, 0)),
        out_shape=jax.ShapeDtypeStruct((t, D_MODEL), BF16),
        compiler_params=_cparams(("arbitrary",)),
        name="knorm",
    )(x, mod, g)


def _head_rms(acc, gain):
    outs = []
    for hh in range(KIN_TN // HEAD_DIM):
        a = acc[:, hh * HEAD_DIM:(hh + 1) * HEAD_DIM]
        outs.append(a * lax.rsqrt(jnp.mean(a * a, axis=-1, keepdims=True) + EPS) * gain)
    return jnp.concatenate(outs, axis=-1)


def _kin_kernel(h_ref, w_ref, qg_ref, kg_ref, oqk_ref, ov_ref, ou_ref, raw):
    i = pl.program_id(0)
    j = pl.program_id(1)

    @pl.when((i == 0) & (j == 0))
    def _():
        raw[1] = jnp.zeros(raw.shape[1:], F32)

    gain = jnp.where(j - 1 < KIN_Q_TILES, qg_ref[...] * (HEAD_DIM ** -0.5 * LOG2E), kg_ref[...])

    def step(wslot, rslot):
        acc = jnp.dot(h_ref[...], w_ref[...], preferred_element_type=F32)
        oqk_ref[...] = _head_rms(raw[rslot], gain).astype(BF16)
        raw[wslot] = acc
        ov_ref[...] = acc.astype(BF16)
        ou_ref[...] = acc

    @pl.when(j % 2 == 0)
    def _():
        step(0, 1)

    @pl.when(j % 2 == 1)
    def _():
        step(1, 0)


KIN_U_HY_COL = 0
KIN_U_FN_COL = KIN_HY_TILES


def _kin(h, w, l, qg, kg):
    t = h.shape[0]
    tm = KIN_TM
    nj = IN_PROJ // KIN_TN
    v_spare = KIN_Q_TILES
    u_spare = KIN_HY_TILES + 1

    def v_col(j):
        return jnp.where(j < KIN_QK_TILES, v_spare, jnp.where(j < KIN_QKV_TILES, j - KIN_QK_TILES, v_spare + 1))

    def u_col(j):
        return jnp.where(j > KIN_FN_TILE, j - KIN_FN_TILE - 1 + KIN_U_HY_COL,
                         jnp.where(j == KIN_FN_TILE, KIN_U_FN_COL, u_spare))

    return pl.pallas_call(
        _kin_kernel,
        grid=(t // tm, nj),
        in_specs=[
            pl.BlockSpec((tm, D_MODEL), lambda i, j: (i, 0)),
            pl.BlockSpec((None, D_MODEL, KIN_TN), lambda i, j: (l, 0, j)),
            pl.BlockSpec((1, HEAD_DIM), lambda i, j: (0, 0)),
            pl.BlockSpec((1, HEAD_DIM), lambda i, j: (0, 0)),
        ],
        out_specs=[
            pl.BlockSpec((tm, KIN_TN), lambda i, j: (i, jnp.clip(j - 1, 0, KIN_QK_TILES))),
            pl.BlockSpec((tm, KIN_TN), lambda i, j: (i, v_col(j))),
            pl.BlockSpec((tm, KIN_TN), lambda i, j: (i, u_col(j))),
        ],
        out_shape=[
            jax.ShapeDtypeStruct((t, (KIN_QK_TILES + 1) * KIN_TN), BF16),
            jax.ShapeDtypeStruct((t, (v_spare + 2) * KIN_TN), BF16),
            jax.ShapeDtypeStruct((t, (u_spare + 1) * KIN_TN), F32),
        ],
        scratch_shapes=[pltpu.VMEM((2, tm, KIN_TN), F32)],
        compiler_params=_cparams(("arbitrary", "arbitrary")),
        name="kin",
    )(h, w, qg, kg)


ATT_R = 4
ATT_KW = ATT_R + NA_KH


def _attn_variant_tiles(rows, variant):
    if variant == 0:
        r0, ws = 0, 0
    elif variant == 1:
        r0 = ATT_R
        ws = r0 - NA_KH // 2
    else:
        r0, ws = rows - ATT_R, rows - ATT_KW
    tiles = []
    for a in range(ATT_R):
        r = r0 + a
        rs = min(max(r - NA_KH // 2, 0), rows - NA_KH)
        row = []
        for jp in range(ATT_KW // 2):
            kr = ws + 2 * jp
            valid_l = rs <= kr < rs + NA_KH
            valid_r = rs <= kr + 1 < rs + NA_KH
            d_left = kr - r + (NA_KH - 1)
            row.append((valid_l, valid_r, d_left))
        tiles.append(row)
    return tiles


def _attn_build_bias(rpb_ref, t2, bias, h, rows):
    n_rd = 2 * NA_KH - 1
    n_cd = 2 * NA_KW - 1
    cidx = lax.broadcasted_iota(jnp.int32, (GRID_W, LANES), 0)
    lane = lax.broadcasted_iota(jnp.int32, (GRID_W, LANES), 1)
    widx = lane & (GRID_W - 1)
    left = lane < GRID_W
    dcidx = jnp.clip(widx - cidx, -(NA_KW - 1), NA_KW - 1) + (NA_KW - 1)
    base = h * (n_rd * n_cd)
    for dp in range(n_rd + 1):
        acc = jnp.zeros((GRID_W, LANES), F32)
        for e in range(n_cd):
            vl = rpb_ref[base + (dp - 1) * n_cd + e] if dp - 1 >= 0 else 0.0
            vr = rpb_ref[base + dp * n_cd + e] if dp < n_rd else 0.0
            acc = jnp.where(dcidx == e, jnp.where(left, vl, vr), acc)
        t2[dp] = acc * LOG2E

    c_lo = jnp.clip(cidx - NA_KW // 2, 0, GRID_W - NA_KW)
    col_ok = (widx >= c_lo) & (widx < c_lo + NA_KW)
    for variant in range(3):
        tiles = _attn_variant_tiles(rows, variant)
        for a in range(ATT_R):
            for jp in range(ATT_KW // 2):
                valid_l, valid_r, d_left = tiles[a][jp]
                if not (valid_l or valid_r):
                    tile = jnp.full((GRID_W, LANES), NEG, F32)
                else:
                    ok = col_ok
                    if not valid_l:
                        ok = ok & (~left)
                    if not valid_r:
                        ok = ok & left
                    tile = jnp.where(ok, t2[d_left + 1], NEG)
                bias[variant, a * GRID_W:(a + 1) * GRID_W, jp * LANES:(jp + 1) * LANES] = tile


def _attn_kernel(rpb_ref, q_ref, k_ref, v_ref, o_ref, t2, bias, s_buf, p_buf, rden_buf, *, seq_len):
    rows = seq_len // GRID_W
    nb = rows // ATT_R

    @pl.when(pl.program_id(1) == 0)
    def _():
        _attn_build_bias(rpb_ref, t2, bias, pl.program_id(0), rows)

    def q_rows(rb):
        if isinstance(rb, int):
            return pl.ds(rb * (ATT_R * GRID_W), ATT_R * GRID_W)
        return pl.ds(pl.multiple_of(rb * (ATT_R * GRID_W), ATT_R * GRID_W), ATT_R * GRID_W)

    def k_rows(rb):
        if isinstance(rb, int):
            return pl.ds(min(max(rb * ATT_R - NA_KH // 2, 0), rows - ATT_KW) * GRID_W, ATT_KW * GRID_W)
        ws = jnp.clip(rb * ATT_R - NA_KH // 2, 0, rows - ATT_KW)
        return pl.ds(pl.multiple_of(ws * GRID_W, GRID_W), ATT_KW * GRID_W)

    def scores(rb, slot):
        if isinstance(rb, int):
            variant = 0 if rb == 0 else (2 if rb == nb - 1 else 1)
        else:
            variant = jnp.where(rb == 0, 0, jnp.where(rb == nb - 1, 2, 1))
        s = lax.dot_general(q_ref[q_rows(rb), :], k_ref[k_rows(rb), :], (((1,), (1,)), ((), ())),
                            preferred_element_type=F32)
        s_buf[slot] = s + bias[variant]

    def softmax(slot):
        s = s_buf[slot]
        m = jnp.max(s, axis=-1, keepdims=True)
        p = jnp.exp2(s - m)
        rden_buf[slot] = 1.0 / jnp.sum(p, axis=-1, keepdims=True)
        p_buf[slot] = p.astype(BF16)

    def values(rb, slot):
        o = jnp.dot(p_buf[slot], v_ref[k_rows(rb), :], preferred_element_type=F32)
        o_ref[q_rows(rb), :] = o * rden_buf[slot]

    for b in range(4):
        scores(b, b)
    softmax(0)
    softmax(1)

    def body(u, carry):
        t = 4 * u
        for half in (0, 2):
            values(t + half, half)
            values(t + half + 1, half + 1)
            softmax((half + 2) % 4)
            softmax((half + 3) % 4)
            scores(t + half + 4, half)
            scores(t + half + 5, half + 1)
        return carry

    lax.fori_loop(0, nb // 4 - 1, body, 0)
    values(nb - 4, 0)
    values(nb - 3, 1)
    softmax(2)
    softmax(3)
    values(nb - 2, 2)
    values(nb - 1, 3)


def _attn(qk, v, rpb_flat, bsz, seq_len):
    t = qk.shape[0]
    kernel = functools.partial(_attn_kernel, seq_len=seq_len)
    return pl.pallas_call(
        kernel,
        grid=(NA_HEADS, bsz),
        in_specs=[
            pl.BlockSpec(memory_space=pltpu.SMEM),
            pl.BlockSpec((seq_len, HEAD_DIM), lambda h, b: (b, h)),
            pl.BlockSpec((seq_len, HEAD_DIM), lambda h, b: (b, NA_HEADS + h)),
            pl.BlockSpec((seq_len, HEAD_DIM), lambda h, b: (b, h)),
        ],
        out_specs=pl.BlockSpec((seq_len, HEAD_DIM), lambda h, b: (b, h)),
        out_shape=jax.ShapeDtypeStruct((t, NA_WIDTH), F32),
        scratch_shapes=[
            pltpu.VMEM((2 * NA_KH, GRID_W, LANES), F32),
            pltpu.VMEM((3, ATT_R * GRID_W, ATT_KW * GRID_W), F32),
            pltpu.VMEM((4, ATT_R * GRID_W, ATT_KW * GRID_W), F32),
            pltpu.VMEM((4, ATT_R * GRID_W, ATT_KW * GRID_W), BF16),
            pltpu.VMEM((4, ATT_R * GRID_W, 1), F32),
        ],
        compiler_params=_cparams(("arbitrary", "arbitrary")),
        name="attn",
    )(rpb_flat, qk, qk, v)


DFT_TW = 8


def _kron_mat(m):
    eye = np.eye(DFT_TW, dtype=ml_dtypes.bfloat16)
    return jnp.asarray(np.kron(m.astype(ml_dtypes.bfloat16), eye))


BF16_ROWS = 16


def _dft1_kernel(m_ref, x_ref, twr_ref, twi_ref, o_ref):
    kd, rows, c = x_ref.shape
    mo = o_ref.shape[0]
    kh = mo // 2
    for g in range(rows // BF16_ROWS):
        re, im = [], []
        for s in range(g * BF16_ROWS // DFT_TW, (g + 1) * BF16_ROWS // DFT_TW):
            sl = slice(s * DFT_TW, (s + 1) * DFT_TW)
            x = x_ref[:, sl, :].reshape(kd * DFT_TW, c)
            y = jnp.dot(m_ref[...], x.astype(BF16), preferred_element_type=F32).reshape(mo, DFT_TW, c)
            twr = _lane_tile(twr_ref[:, sl, :], c)
            twi = _lane_tile(twi_ref[:, sl, :], c)
            yr = y[:kh]
            yi = y[kh:]
            re.append(yr * twr - yi * twi)
            im.append(yr * twi + yi * twr)
        gl = slice(g * BF16_ROWS, (g + 1) * BF16_ROWS)
        o_ref[:kh, gl, :] = jnp.concatenate(re, axis=1).astype(BF16)
        o_ref[kh:, gl, :] = jnp.concatenate(im, axis=1).astype(BF16)


def _dft1(mk, x, tw, c=None, col=0):
    bsz, kd, n2, width = x.shape
    c = width if c is None else c
    mo = mk.shape[0] // DFT_TW
    rows = 2 * BF16_ROWS if c <= FN_WIDTH else BF16_ROWS
    return pl.pallas_call(
        _dft1_kernel,
        grid=(bsz, n2 // rows),
        in_specs=[
            pl.BlockSpec(mk.shape, lambda b, j: (0, 0)),
            pl.BlockSpec((None, kd, rows, c), lambda b, j: (b, 0, j, col)),
            pl.BlockSpec((mo // 2, rows, LANES), lambda b, j: (0, j, 0)),
            pl.BlockSpec((mo // 2, rows, LANES), lambda b, j: (0, j, 0)),
        ],
        out_specs=pl.BlockSpec((None, mo, rows, c), lambda b, j: (b, 0, j, 0)),
        out_shape=jax.ShapeDtypeStruct((bsz, mo, n2, c), BF16),
        compiler_params=_cparams(("arbitrary", "arbitrary")),
        name="dft1",
    )(mk, x, *tw)


def _stage1_fwd_mat(n1, kd, scale=1.0, rows=None):
    c, s = _cos_sin(n1 if rows is None else rows, kd, n1)
    return _kron_mat(np.concatenate([c, -s], axis=0) * scale)


def _stage2_mats(n2):
    c, s = _cos_sin(n2, n2, n2)
    fwd = np.block([[c, s], [-s, c]])
    inv = np.block([[c, -s], [s, c]])
    return jnp.asarray(fwd.astype(ml_dtypes.bfloat16)), jnp.asarray(inv.astype(ml_dtypes.bfloat16))


def _half_rows(n1):
    return n1 // 2 + DFT_TW


def _twiddle(n1, n2, rows=None):
    rows = n1 if rows is None else rows
    ang = 2.0 * np.pi * np.outer(np.arange(rows), np.arange(n2)) / (n1 * n2)
    twr = jnp.asarray(np.cos(ang).astype(np.float32))
    twi = jnp.asarray((-np.sin(ang)).astype(np.float32))
    shape = (rows, n2, LANES)
    return jnp.broadcast_to(twr[:, :, None], shape), jnp.broadcast_to(twi[:, :, None], shape)


def _lane_tile(x, width):
    reps = width // x.shape[-1]
    return x if reps == 1 else jnp.concatenate([x] * reps, axis=-1)


FFT_KC = 8


def _fnw_kernel(cs_ref, w_ref, o_ref):
    o_ref[...] = jnp.dot(cs_ref[...], w_ref[...], precision=HIGHEST, preferred_element_type=F32)


def _fnw(fn_w):
    c, s = _cos_sin(FN_GROUP_DIM, FN_GROUP_DIM, FN_GROUP_DIM)
    cs = jnp.asarray(np.concatenate([c, s], axis=0).astype(np.float32))
    return pl.pallas_call(
        _fnw_kernel,
        grid=(FN_GROUPS,),
        in_specs=[
            pl.BlockSpec((2 * FN_GROUP_DIM, FN_GROUP_DIM), lambda g: (0, 0)),
            pl.BlockSpec((None, FN_GROUP_DIM, FN_GROUP_DIM), lambda g: (g, 0, 0)),
        ],
        out_specs=pl.BlockSpec((None, 2 * FN_GROUP_DIM, FN_GROUP_DIM), lambda g: (g, 0, 0)),
        out_shape=jax.ShapeDtypeStruct((FN_GROUPS, 2 * FN_GROUP_DIM, FN_GROUP_DIM), F32),
        compiler_params=_cparams(("arbitrary",)),
        name="fnw",
    )(cs, fn_w)


def _fn2_kernel(y_ref, f2_ref, g_ref, b_ref, o_ref):
    n2 = DFT_N2
    for t in range(FFT_KC):
        rhs = jnp.concatenate([y_ref[0, t], y_ref[1, t]], axis=0)
        z = jnp.dot(f2_ref[...], rhs, preferred_element_type=F32)
        zr = z[:n2]
        zi = z[n2:]
        outs = []
        for g in range(FN_GROUPS):
            sl = slice(g * FN_GROUP_DIM, (g + 1) * FN_GROUP_DIM)
            zc = jnp.concatenate([zr[:, sl], zi[:, sl]], axis=1).astype(BF16)
            outs.append(jnp.dot(zc, g_ref[g].astype(BF16), preferred_element_type=F32))
        o_ref[:, t, :] = jnp.concatenate(outs, axis=1) + b_ref[...]


def _fourier(u, gcat, fn_b, bsz, seq_len):
    n2 = DFT_N2
    n1 = seq_len // n2
    m1 = _stage1_fwd_mat(n1, n1, scale=1.0 / math.sqrt(seq_len * FN_GROUP_DIM))
    y = _dft1(m1, u.reshape(bsz, n1, n2, u.shape[-1]), _twiddle(n1, n2), c=FN_WIDTH, col=KIN_U_FN_COL)
    y = y.reshape(bsz, 2, n1, n2, FN_WIDTH)
    f2, _ = _stage2_mats(n2)
    out = pl.pallas_call(
        _fn2_kernel,
        grid=(n1 // FFT_KC, bsz),
        in_specs=[
            pl.BlockSpec((None, 2, FFT_KC, n2, FN_WIDTH), lambda k, b: (b, 0, k, 0, 0)),
            pl.BlockSpec(f2.shape, lambda k, b: (0, 0)),
            pl.BlockSpec((FN_GROUPS, 2 * FN_GROUP_DIM, FN_GROUP_DIM), lambda k, b: (0, 0, 0)),
            pl.BlockSpec((1, FN_WIDTH), lambda k, b: (0, 0)),
        ],
        out_specs=pl.BlockSpec((None, n2, FFT_KC, FN_WIDTH), lambda k, b: (b, 0, k, 0)),
        out_shape=jax.ShapeDtypeStruct((bsz, n2, n1, FN_WIDTH), F32),
        compiler_params=_cparams(("arbitrary", "arbitrary")),
        name="fn2",
    )(y, f2, gcat, fn_b)
    return out.reshape(bsz * seq_len, FN_WIDTH)


HPRE_TM = 512


def _hpre_kernel(up_ref, um_ref, un_ref, cw_ref, cb_ref, z_ref, x0_ref, *, seq_len):
    tm = HPRE_TM
    i = pl.program_id(0)
    at_start = (i * tm) % seq_len == 0
    at_end = ((i + 1) * tm) % seq_len == 0
    u = um_ref[...]
    r = lax.broadcasted_iota(jnp.int32, (tm, 1), 0)
    prev_row = jnp.where(at_start, 0.0, up_ref[7:8, :])
    next_row = jnp.where(at_end, 0.0, un_ref[0:1, :])
    upv = jnp.where(r == 0, prev_row, pltpu.roll(u, 1, 0))
    dnv = jnp.where(r == tm - 1, next_row, pltpu.roll(u, tm - 1, 0))
    y = upv * cw_ref[0:1, :] + u * cw_ref[1:2, :] + dnv * cw_ref[2:3, :] + cb_ref[...]
    z_ref[...] = y[:, 2 * HY_WIDTH:] * y[:, HY_WIDTH:2 * HY_WIDTH]
    x0_ref[...] = y[:, :HY_WIDTH]


def _hpre(u, cw, cb, seq_len):
    t = u.shape[0]
    tm = HPRE_TM
    w = 3 * HY_WIDTH
    nb8 = t // 8
    return pl.pallas_call(
        functools.partial(_hpre_kernel, seq_len=seq_len),
        grid=(t // tm,),
        in_specs=[
            pl.BlockSpec((8, w), lambda i: (jnp.maximum(i * (tm // 8) - 1, 0), 0)),
            pl.BlockSpec((tm, w), lambda i: (i, 0)),
            pl.BlockSpec((8, w), lambda i: (jnp.minimum((i + 1) * (tm // 8), nb8 - 1), 0)),
            pl.BlockSpec((3, w), lambda i: (0, 0)),
            pl.BlockSpec((1, w), lambda i: (0, 0)),
        ],
        out_specs=[
            pl.BlockSpec((tm, HY_WIDTH), lambda i: (i, 0)),
            pl.BlockSpec((tm, HY_WIDTH), lambda i: (i, 0)),
        ],
        out_shape=[jax.ShapeDtypeStruct((t, HY_WIDTH), F32), jax.ShapeDtypeStruct((t, HY_WIDTH), F32)],
        compiler_params=_cparams(("arbitrary",)),
        name="hpre",
    )(u, u, u, cw, cb)


HFILT_TL = 512
HY_EMB_PAD = 128


def _hfilt_kernel(z_ref, w1_ref, b1_ref, f1_ref, w2_ref, b2_ref, f2_ref, w3_ref, b3_ref, dl_ref, o_ref):
    z = z_ref[...]
    h = jnp.sin(f1_ref[...] * (jnp.dot(z, w1_ref[...], precision=HIGHEST, preferred_element_type=F32) + b1_ref[...]))
    h = jnp.sin(f2_ref[...] * (jnp.dot(h, w2_ref[...], precision=HIGHEST, preferred_element_type=F32) + b2_ref[...]))
    h = jnp.dot(h, w3_ref[...], precision=HIGHEST, preferred_element_type=F32) + b3_ref[...]
    t = z[:, 0:1]
    h = h * jnp.exp(-t * dl_ref[...])
    row = lax.broadcasted_iota(jnp.int32, h.shape, 0) + pl.program_id(0) * HFILT_TL
    col = lax.broadcasted_iota(jnp.int32, h.shape, 1)
    o_ref[...] = jnp.where((row == 0) & (col >= HY_WIDTH), 0.0, h)


def _hyena_emb_np(seq_len):
    t = np.linspace(0.0, 1.0, seq_len)[:, None]
    bands = (HY_EMB_DIM - 1) // 2
    w = 2.0 * np.pi * np.arange(seq_len)[:, None] / seq_len
    fr = np.linspace(1e-4, bands - 1, bands)[None, :]
    z = np.concatenate([t, np.cos(fr * w), -np.sin(fr * w)], axis=-1)
    out = np.zeros((seq_len, HY_EMB_PAD), np.float32)
    out[:, :HY_EMB_DIM] = z
    return out


def _hfilt(seq_len, w1, b1, f1, w2, b2, f2, w3, b3):
    zemb = jnp.asarray(_hyena_emb_np(seq_len))
    max_decay = math.log(HY_TARGET) / HY_FAST_DECAY
    min_decay = math.log(HY_TARGET) / HY_SLOW_DECAY
    deltas = np.abs(np.linspace(min_decay, max_decay, HY_WIDTH))
    dl = jnp.asarray(np.tile(deltas, 2)[None, :].astype(np.float32))
    w1p = jnp.pad(w1, ((0, HY_EMB_PAD - HY_EMB_DIM), (0, 0)))
    tl = HFILT_TL
    full = lambda shape: pl.BlockSpec(shape, lambda i: (0,) * len(shape))
    return pl.pallas_call(
        _hfilt_kernel,
        grid=(seq_len // tl,),
        in_specs=[
            pl.BlockSpec((tl, HY_EMB_PAD), lambda i: (i, 0)),
            full((HY_EMB_PAD, HY_HIDDEN)), full((1, HY_HIDDEN)), full((1, HY_HIDDEN)),
            full((HY_HIDDEN, HY_HIDDEN)), full((1, HY_HIDDEN)), full((1, HY_HIDDEN)),
            full((HY_HIDDEN, 2 * HY_WIDTH)), full((1, 2 * HY_WIDTH)), full((1, 2 * HY_WIDTH)),
        ],
        out_specs=pl.BlockSpec((tl, 2 * HY_WIDTH), lambda i: (i, 0)),
        out_shape=jax.ShapeDtypeStruct((seq_len, 2 * HY_WIDTH), F32),
        compiler_params=_cparams(("arbitrary",)),
        name="hfilt",
    )(zemb, w1p, b1, f1, w2, b2, f2, w3, b3, dl)


def _hk2_kernel(y_ref, f2_ref, o_ref):
    n2 = DFT_N2
    for t in range(FFT_KC):
        rhs = jnp.concatenate([y_ref[0, t], y_ref[1, t]], axis=0)
        z = jnp.dot(f2_ref[...], rhs, preferred_element_type=F32)
        zr = z[:n2]
        zi = z[n2:]
        o_ref[0, t] = zr[:, :HY_WIDTH] + zr[:, HY_WIDTH:]
        o_ref[1, t] = zi[:, :HY_WIDTH] - zi[:, HY_WIDTH:]


def _hy2_kernel(y_ref, kf_ref, twr_ref, twi_ref, f2_ref, f2i_ref, o_ref):
    n2 = DFT_N2
    for t in range(FFT_KC):
        rhs = jnp.concatenate([y_ref[0, t], y_ref[1, t]], axis=0)
        z = jnp.dot(f2_ref[...], rhs, preferred_element_type=F32)
        zr = z[:n2]
        zi = z[n2:]
        kr = kf_ref[0, t]
        ki = kf_ref[1, t]
        pr = zr * kr - zi * ki
        pi = zr * ki + zi * kr
        rhs = jnp.concatenate([pr, pi], axis=0).astype(BF16)
        g = jnp.dot(f2i_ref[...], rhs, preferred_element_type=F32)
        gr = g[:n2]
        gi = g[n2:]
        twr = _lane_tile(twr_ref[t], HY_WIDTH)
        twi = _lane_tile(twi_ref[t], HY_WIDTH)
        o_ref[0, t] = (gr * twr + gi * twi).astype(BF16)
        o_ref[1, t] = (gi * twr - gr * twi).astype(BF16)


def _hy3_kernel(m_ref, g_ref, z_ref, x0_ref, d_ref, o_ref):
    kd, rows, c = g_ref.shape
    mo = o_ref.shape[0]
    g = g_ref[...].astype(F32)
    for s in range(rows // DFT_TW):
        sl = slice(s * DFT_TW, (s + 1) * DFT_TW)
        rhs = g[:, sl, :].reshape(kd * DFT_TW, c).astype(BF16)
        conv = jnp.dot(m_ref[...], rhs, preferred_element_type=F32).reshape(mo, DFT_TW, c)
        o_ref[:, sl, :] = (conv + z_ref[:, sl, :] * d_ref[...]) * x0_ref[:, sl, :]


def _hyena_filter_spectrum(seq_len, w1, b1, f1, w2, b2, f2, w3, b3):
    n = 2 * seq_len
    n2 = DFT_N2
    n1 = n // n2
    kh = _half_rows(n1)
    h = _hfilt(seq_len, w1, b1, f1, w2, b2, f2, w3, b3)
    m1 = _stage1_fwd_mat(n1, n1 // 2, rows=kh)
    y = _dft1(m1, h.reshape(1, n1 // 2, n2, 2 * HY_WIDTH), _twiddle(n1, n2, kh))
    y = y.reshape(2, kh, n2, 2 * HY_WIDTH)
    f2m, _ = _stage2_mats(n2)
    return pl.pallas_call(
        _hk2_kernel,
        grid=(kh // FFT_KC,),
        in_specs=[
            pl.BlockSpec((2, FFT_KC, n2, 2 * HY_WIDTH), lambda k: (0, k, 0, 0)),
            pl.BlockSpec(f2m.shape, lambda k: (0, 0)),
        ],
        out_specs=pl.BlockSpec((2, FFT_KC, n2, HY_WIDTH), lambda k: (0, k, 0, 0)),
        out_shape=jax.ShapeDtypeStruct((2, kh, n2, HY_WIDTH), F32),
        compiler_params=_cparams(("arbitrary",)),
        name="hk2",
    )(y, f2m)


def _hyena_conv(z, x0, kf, d_skip, bsz, seq_len):
    n = 2 * seq_len
    n2 = DFT_N2
    n1 = n // n2
    kh = _half_rows(n1)
    zv = z.reshape(bsz, n1 // 2, n2, HY_WIDTH)
    tw = _twiddle(n1, n2, kh)
    y = _dft1(_stage1_fwd_mat(n1, n1 // 2, rows=kh), zv, tw).reshape(bsz, 2, kh, n2, HY_WIDTH)
    f2m, f2i = _stage2_mats(n2)
    g = pl.pallas_call(
        _hy2_kernel,
        grid=(kh // FFT_KC, bsz),
        in_specs=[
            pl.BlockSpec((None, 2, FFT_KC, n2, HY_WIDTH), lambda k, b: (b, 0, k, 0, 0)),
            pl.BlockSpec((2, FFT_KC, n2, HY_WIDTH), lambda k, b: (0, k, 0, 0)),
            pl.BlockSpec((FFT_KC, n2, LANES), lambda k, b: (k, 0, 0)),
            pl.BlockSpec((FFT_KC, n2, LANES), lambda k, b: (k, 0, 0)),
            pl.BlockSpec(f2m.shape, lambda k, b: (0, 0)),
            pl.BlockSpec(f2i.shape, lambda k, b: (0, 0)),
        ],
        out_specs=pl.BlockSpec((None, 2, FFT_KC, n2, HY_WIDTH), lambda k, b: (b, 0, k, 0, 0)),
        out_shape=jax.ShapeDtypeStruct((bsz, 2, kh, n2, HY_WIDTH), BF16),
        compiler_params=_cparams(("arbitrary", "arbitrary")),
        name="hy2",
    )(y, kf, tw[0], tw[1], f2m, f2i)
    c, s = _cos_sin(n1 // 2, kh, n1)
    wt = np.where(np.arange(kh) < n1 // 2, 2.0, 0.0)
    wt[0] = 1.0
    wt[n1 // 2] = 1.0
    m3 = _kron_mat(np.concatenate([c * wt, -s * wt], axis=1) / n)
    rows = BF16_ROWS
    half =pl.BlockSpec((None, n1 // 2, rows, HY_WIDTH), lambda b, j: (b, 0, j, 0))
    out = pl.pallas_call(
        _hy3_kernel,
        grid=(bsz, n2 // rows),
        in_specs=[
            pl.BlockSpec(m3.shape, lambda b, j: (0, 0)),
            pl.BlockSpec((None, 2 * kh, rows, HY_WIDTH), lambda b, j: (b, 0, j, 0)),
            half,
            half,
            pl.BlockSpec((1, HY_WIDTH), lambda b, j: (0, 0)),
        ],
        out_specs=half,
        out_shape=jax.ShapeDtypeStruct((bsz, n1 // 2, n2, HY_WIDTH), F32),
        compiler_params=_cparams(("arbitrary", "arbitrary")),
        name="hy3",
    )(m3, g.reshape(bsz, 2 * kh, n2, HY_WIDTH), zv, x0.reshape(bsz, n1 // 2, n2, HY_WIDTH), d_skip)
    return out.reshape(bsz * seq_len, HY_WIDTH)


KOUT_TM = 512
KOUT_SUB = 256


def _rms(y):
    return y * lax.rsqrt(jnp.mean(y * y, axis=-1, keepdims=True) + EPS)


def _kout_kernel(ya_ref, yb_ref, yc_ref, x_ref, mod_ref, og_ref, n2g_ref, w_ref, xo_ref, h2_ref):
    for s in range(KOUT_TM // KOUT_SUB):
        sl = slice(s * KOUT_SUB, (s + 1) * KOUT_SUB)
        n = jnp.concatenate([_rms(ya_ref[sl, :]), _rms(yb_ref[sl, :]), _rms(yc_ref[sl, :])], axis=-1) * og_ref[...]
        y = jnp.dot(n.astype(BF16), w_ref[...], preferred_element_type=F32)
        xn = x_ref[sl, :] + mod_ref[2:3, :] * y
        xo_ref[sl, :] = xn
        h2 = _rms(xn) * n2g_ref[...]
        h2_ref[sl, :] = (h2 * (1.0 + mod_ref[4:5, :]) + mod_ref[3:4, :]).astype(BF16)


def _kout(ya, yb, yc, x, mod, og, n2g, w, l, seq_len):
    t = x.shape[0]
    tm = KOUT_TM
    row = lambda width: pl.BlockSpec((tm, width), lambda i: (i, 0))
    return pl.pallas_call(
        _kout_kernel,
        grid=(t // tm,),
        in_specs=[
            row(NA_WIDTH), row(FN_WIDTH), row(HY_WIDTH), row(D_MODEL),
            pl.BlockSpec((None, N_MOD, D_MODEL), lambda i: ((i * tm) // seq_len, 0, 0)),
            pl.BlockSpec((1, D_MODEL), lambda i: (0, 0)),
            pl.BlockSpec((1, D_MODEL), lambda i: (0, 0)),
            pl.BlockSpec((None, D_MODEL, D_MODEL), lambda i: (l, 0, 0)),
        ],
        out_specs=[row(D_MODEL), row(D_MODEL)],
        out_shape=[jax.ShapeDtypeStruct((t, D_MODEL), F32), jax.ShapeDtypeStruct((t, D_MODEL), BF16)],
        compiler_params=_cparams(("arbitrary",)),
        name="kout",
    )(ya, yb, yc, x, mod, og, n2g, w)


MLP_TM = 512
MLP_TF = 512
MLP_HALO = 16


def _gelu_exact(a):
    return 0.5 * a * (1.0 + lax.erf(a * (1.0 / math.sqrt(2.0))))


def _mlp_kernel(hp_ref, hm_ref, hn_ref, x_ref, mod_ref, wa_ref, wg_ref, cwb_ref, wd_ref, *rest,
                seq_len, emit_next):
    if emit_next:
        modn_ref, gn_ref, o_ref, hnext_ref, hext, acc = rest
    else:
        o_ref, hext, acc = rest
    tm = MLP_TM
    halo = MLP_HALO
    i = pl.program_id(0)
    j = pl.program_id(1)

    last = pl.num_programs(1) - 1

    def chunk():
        a_ext = jnp.dot(hext[...], wa_ref[...], preferred_element_type=F32)
        mid = a_ext[0:tm]
        r = lax.broadcasted_iota(jnp.int32, (tm, 1), 0)
        up = jnp.where(r == 0, a_ext[tm:tm + 1], pltpu.roll(mid, 1, 0))
        dn = jnp.where(r == tm - 1, a_ext[tm + 1:tm + 2], pltpu.roll(mid, tm - 1, 0))
        a = up * cwb_ref[0:1, :] + mid * cwb_ref[1:2, :] + dn * cwb_ref[2:3, :] + cwb_ref[3:4, :]
        gate = jnp.dot(hext[0:tm, :], wg_ref[...], preferred_element_type=F32)
        act = (_gelu_exact(a) * gate).astype(BF16)
        return jnp.dot(act, wd_ref[...], preferred_element_type=F32)

    @pl.when(j == 0)
    def _():
        at_start = (i * tm) % seq_len == 0
        at_end = ((i + 1) * tm) % seq_len == 0
        hext[0:tm, :] = hm_ref[...]
        r16 = lax.broadcasted_iota(jnp.int32, (halo, 1), 0)
        prev = pltpu.roll(hp_ref[...].astype(F32), 1, 0)
        nxt = pltpu.roll(hn_ref[...].astype(F32), 1, 0)
        extra = jnp.where((r16 == 0) & jnp.logical_not(at_start), prev,
                          jnp.where((r16 == 1) & jnp.logical_not(at_end), nxt, 0.0))
        hext[tm:, :] = extra.astype(BF16)
        acc[...] = chunk()

    @pl.when((j > 0) & (j < last))
    def _():
        acc[...] += chunk()

    @pl.when(j == last)
    def _():
        xo = x_ref[...] + mod_ref[5:6, :] * (acc[...] + chunk())
        o_ref[...] = xo
        if emit_next:
            hnext_ref[...] = _modnorm(xo, gn_ref[...], modn_ref[0:1, :], modn_ref[1:2, :]).astype(BF16)


def _mlp(h2, x, mod, w_up, cwb, w_down, l, seq_len, next_norm=None):
    t = x.shape[0]
    tm, tf, halo = MLP_TM, MLP_TF, MLP_HALO
    nf = D_FF // tf
    nbh = t // halo
    emit_next = next_norm is not None
    row = pl.BlockSpec((tm, D_MODEL), lambda i, j: (i, 0))
    modspec = pl.BlockSpec((None, N_MOD, D_MODEL), lambda i, j: ((i * tm) // seq_len, 0, 0))
    in_specs = [
        pl.BlockSpec((halo, D_MODEL), lambda i, j: (jnp.maximum(i * (tm // halo) - 1, 0), 0)),
        row,
        pl.BlockSpec((halo, D_MODEL), lambda i, j: (jnp.minimum((i + 1) * (tm // halo), nbh - 1), 0)),
        row,
        modspec,
        pl.BlockSpec((None, D_MODEL, tf), lambda i, j: (l, 0, j)),
        pl.BlockSpec((None, D_MODEL, tf), lambda i, j: (l, 0, nf + j)),
        pl.BlockSpec((SHORT_CONV + 1, tf), lambda i, j: (0, j)),
        pl.BlockSpec((None, tf, D_MODEL), lambda i, j: (l, j, 0)),
    ]
    args = [h2, h2, h2, x, mod, w_up, w_up, cwb, w_down]
    out_specs = [row]
    out_shape = [jax.ShapeDtypeStruct((t, D_MODEL), F32)]
    if emit_next:
        in_specs += [modspec, pl.BlockSpec((1, D_MODEL), lambda i, j: (0, 0))]
        args += list(next_norm)
        out_specs.append(row)
        out_shape.append(jax.ShapeDtypeStruct((t, D_MODEL), BF16))
    return pl.pallas_call(
        functools.partial(_mlp_kernel, seq_len=seq_len, emit_next=emit_next),
        grid=(t // tm, nf),
        in_specs=in_specs,
        out_specs=out_specs,
        out_shape=out_shape,
        scratch_shapes=[pltpu.VMEM((tm + halo, D_MODEL), BF16), pltpu.VMEM((tm, D_MODEL), F32)],
        compiler_params=_cparams(("arbitrary", "arbitrary")),
        name="mlp",
    )(*args)


def _layer(x, h, mod, p, l, bsz, seq_len, kf, next_norm):
    qk, v, u = _kin(h, p["w_in"], l, p["q_norm_g"][l][None], p["k_norm_g"][l][None])
    ya = _attn(qk, v, p["na_rpb"][l].reshape(-1), bsz, seq_len)
    yb = _fourier(u, p["gcat"][l], p["fn_b"][l][None], bsz, seq_len)
    z, x0 = _hpre(u, p["hy_conv_w"][l], p["hy_conv_b"][l][None], seq_len)
    yc = _hyena_conv(z, x0, kf, p["hy_d"][l][None], bsz, seq_len)
    x1, h2 = _kout(ya, yb, yc, x, mod, p["out_norm_g"][l][None], p["norm2_g"][l][None], p["w_out"], l, seq_len)
    out = _mlp(h2, x1, mod, p["mlp_w_up"], jnp.concatenate([p["mlp_conv_w"][l], p["mlp_conv_b"][l][None]], axis=0), p["mlp_w_down"], l,
               seq_len, next_norm)
    return (out[0], out[1]) if next_norm is not None else (out[0], None)


def kernel(x_prompt, x_sample, c_prompt, c_sample, ada_w, ada_b, norm1_g, w_in, q_norm_g, k_norm_g, na_rpb, fn_w, fn_b, hy_conv_w, hy_conv_b, hy_w1, hy_b1, hy_f1, hy_w2, hy_b2, hy_f2, hy_w3, hy_b3, hy_d, out_norm_g, w_out, norm2_g, mlp_w_up, mlp_conv_w, mlp_conv_b, mlp_w_down):
    depth = ada_w.shape[0]
    groups = [(x_prompt, c_prompt), (x_sample, c_sample)]
    nseq = sum(c.shape[0] for _, c in groups)
    nrow = -(-nseq // 16) * 16
    c_all = jnp.concatenate([c for _, c in groups] + [jnp.zeros((nrow - nseq, D_MODEL), F32)], axis=0)
    mod_all = _ada(c_all, ada_w, ada_b[:, None, :]).reshape(depth, nrow, N_MOD, D_MODEL)

    p = dict(
        w_in=w_in.astype(BF16), q_norm_g=q_norm_g, k_norm_g=k_norm_g, na_rpb=na_rpb,
        fn_b=fn_b, hy_conv_w=hy_conv_w, hy_conv_b=hy_conv_b, hy_d=hy_d, out_norm_g=out_norm_g,
        w_out=w_out.astype(BF16), norm2_g=norm2_g, mlp_w_up=mlp_w_up.astype(BF16), mlp_conv_w=mlp_conv_w,
        mlp_conv_b=mlp_conv_b, mlp_w_down=mlp_w_down.astype(BF16),
        gcat=[_fnw(fn_w[l]) for l in range(depth)],
    )

    outs = []
    seq_off = 0
    kf_cache = {}
    for x, c in groups:
        bsz, seq_len, _ = x.shape
        xt = x.reshape(bsz * seq_len, D_MODEL)
        mods = [mod_all[l, seq_off:seq_off + bsz] for l in range(depth)]
        ht = _knorm(xt, mods[0], norm1_g[0][None], seq_len)
        for l in range(depth):
            if (l, seq_len) not in kf_cache:
                kf_cache[(l, seq_len)] = _hyena_filter_spectrum(
                    seq_len, hy_w1[l], hy_b1[l][None], hy_f1[l][None], hy_w2[l], hy_b2[l][None], hy_f2[l][None],
                    hy_w3[l], hy_b3[l][None])
            next_norm = (mods[l + 1], norm1_g[l + 1][None]) if l + 1 < depth else None
            xt, ht = _layer(xt, ht, mods[l], p, l, bsz, seq_len, kf_cache[(l, seq_len)], next_norm)
        outs.append(xt.reshape(bsz, seq_len, D_MODEL))
        seq_off += bsz
    return tuple(outs)
```

```python
import functools
import math

import ml_dtypes
import numpy as np
import jax
import jax.numpy as jnp
from jax import lax
from jax.experimental import pallas as pl
from jax.experimental.pallas import tpu as pltpu

F32 = jnp.float32
BF16 = jnp.bfloat16

D_MODEL = 2048
GRID_W = 64
HEAD_DIM = 128
NA_HEADS = 8
NA_WIDTH = NA_HEADS * HEAD_DIM
NA_KH = 8
NA_KW = 16
FN_GROUPS = 4
FN_GROUP_DIM = 128
FN_WIDTH = 512
HY_WIDTH = 512
HY_EMB_DIM = 33
HY_HIDDEN = 64
HY_FAST_DECAY = 0.3
HY_SLOW_DECAY = 1.5
HY_TARGET = 1e-2
IN_PROJ = 3 * NA_WIDTH + FN_WIDTH + 3 * HY_WIDTH
D_FF = 5632
SHORT_CONV = 3
N_MOD = 6
EPS = 1e-6
NEG = -1e30
LOG2E = math.log2(math.e)

LANES = 128
DFT_N2 = 128
VMEM_LIMIT = 56 << 20

HIGHEST = lax.Precision.HIGHEST


def _cparams(sem):
    return pltpu.CompilerParams(dimension_semantics=sem, vmem_limit_bytes=VMEM_LIMIT)


def _cos_sin(n_out, n_in, period):
    ang = 2.0 * np.pi * np.outer(np.arange(n_out), np.arange(n_in)) / period
    return np.cos(ang), np.sin(ang)


def _ada_kernel(c_ref, w_ref, b_ref, o_ref):
    c = c_ref[...]
    s = c * (1.0 / (1.0 + jnp.exp(-c)))
    sh = s.astype(BF16)
    sl = (s - sh.astype(F32)).astype(BF16)
    w = w_ref[...]
    wh = w.astype(BF16)
    wl = (w - wh.astype(F32)).astype(BF16)
    nrow = c.shape[0]
    both = jnp.dot(jnp.concatenate([sh, sl], axis=0), wh, preferred_element_type=F32)
    o_ref[...] = both[:nrow] + both[nrow:] + jnp.dot(sh, wl, preferred_element_type=F32) + b_ref[...]


def _ada(c_all, ada_w, ada_b):
    depth = ada_w.shape[0]
    nrow = c_all.shape[0]
    tn = 1024
    ncol = N_MOD * D_MODEL
    return pl.pallas_call(
        _ada_kernel,
        grid=(depth, ncol // tn),
        in_specs=[
            pl.BlockSpec((nrow, D_MODEL), lambda l, j: (0, 0)),
            pl.BlockSpec((None, D_MODEL, tn), lambda l, j: (l, 0, j)),
            pl.BlockSpec((None, 1, tn), lambda l, j: (l, 0, j)),
        ],
        out_specs=pl.BlockSpec((None, nrow, tn), lambda l, j: (l, 0, j)),
        out_shape=jax.ShapeDtypeStruct((depth, nrow, ncol), F32),
        compiler_params=_cparams(("arbitrary", "arbitrary")),
        name="ada",
    )(c_all, ada_w, ada_b)


KIN_TM = 1024
KIN_TN = 512
KIN_Q_TILES = NA_WIDTH // KIN_TN
KIN_QK_TILES = 2 * KIN_Q_TILES
KIN_QKV_TILES = 3 * NA_WIDTH // KIN_TN
KIN_FN_TILE = KIN_QKV_TILES
KIN_HY_TILES = 3 * HY_WIDTH // KIN_TN


KNORM_TM = 512


def _modnorm(x, g, shift, scale):
    ms = jnp.mean(x * x, axis=-1, keepdims=True)
    return x * lax.rsqrt(ms + EPS) * g * (1.0 + scale) + shift


def _knorm_kernel(x_ref, mod_ref, g_ref, o_ref):
    o_ref[...] = _modnorm(x_ref[...], g_ref[...], mod_ref[0:1, :], mod_ref[1:2, :]).astype(BF16)


def _knorm(x, mod, g, seq_len):
    t = x.shape[0]
    tm = KNORM_TM
    return pl.pallas_call(
        _knorm_kernel,
        grid=(t // tm,),
        in_specs=[
            pl.BlockSpec((tm, D_MODEL), lambda i: (i, 0)),
            pl.BlockSpec((None, N_MOD, D_MODEL), lambda i: ((i * tm) // seq_len, 0, 0)),
            pl.BlockSpec((1, D_MODEL), lambda i: (0, 0)),
        ],
        out_specs=pl.BlockSpec((tm, D_MODEL), lambda i: (i, 0)),
        out_shape=jax.ShapeDtypeStruct((t, D_MODEL), BF16),
        compiler_params=_cparams(("arbitrary",)),
        name="knorm",
    )(x, mod, g)


def _head_rms(acc, gain):
    outs = []
    for hh in range(KIN_TN // HEAD_DIM):
        a = acc[:, hh * HEAD_DIM:(hh + 1) * HEAD_DIM]
        outs.append(a * lax.rsqrt(jnp.mean(a * a, axis=-1, keepdims=True) + EPS) * gain)
    return jnp.concatenate(outs, axis=-1)


def _kin_kernel(h_ref, w_ref, qg_ref, kg_ref, oqk_ref, ov_ref, ou_ref, raw):
    i = pl.program_id(0)
    j = pl.program_id(1)

    @pl.when((i == 0) & (j == 0))
    def _():
        raw[1] = jnp.zeros(raw.shape[1:], F32)

    gain = jnp.where(j - 1 < KIN_Q_TILES, qg_ref[...] * (HEAD_DIM ** -0.5 * LOG2E), kg_ref[...])

    def step(wslot, rslot):
        acc = jnp.dot(h_ref[...], w_ref[...], preferred_element_type=F32)
        oqk_ref[...] = _head_rms(raw[rslot], gain).astype(BF16)
        raw[wslot] = acc
        ov_ref[...] = acc.astype(BF16)
        ou_ref[...] = acc

    @pl.when(j % 2 == 0)
    def _():
        step(0, 1)

    @pl.when(j % 2 == 1)
    def _():
        step(1, 0)


KIN_U_HY_COL = 0
KIN_U_FN_COL = KIN_HY_TILES


def _kin(h, w, l, qg, kg):
    t = h.shape[0]
    tm = KIN_TM
    nj = IN_PROJ // KIN_TN
    v_spare = KIN_Q_TILES
    u_spare = KIN_HY_TILES + 1

    def v_col(j):
        return jnp.where(j < KIN_QK_TILES, v_spare, jnp.where(j < KIN_QKV_TILES, j - KIN_QK_TILES, v_spare + 1))

    def u_col(j):
        return jnp.where(j > KIN_FN_TILE, j - KIN_FN_TILE - 1 + KIN_U_HY_COL,
                         jnp.where(j == KIN_FN_TILE, KIN_U_FN_COL, u_spare))

    return pl.pallas_call(
        _kin_kernel,
        grid=(t // tm, nj),
        in_specs=[
            pl.BlockSpec((tm, D_MODEL), lambda i, j: (i, 0)),
            pl.BlockSpec((None, D_MODEL, KIN_TN), lambda i, j: (l, 0, j)),
            pl.BlockSpec((1, HEAD_DIM), lambda i, j: (0, 0)),
            pl.BlockSpec((1, HEAD_DIM), lambda i, j: (0, 0)),
        ],
        out_specs=[
            pl.BlockSpec((tm, KIN_TN), lambda i, j: (i, jnp.clip(j - 1, 0, KIN_QK_TILES))),
            pl.BlockSpec((tm, KIN_TN), lambda i, j: (i, v_col(j))),
            pl.BlockSpec((tm, KIN_TN), lambda i, j: (i, u_col(j))),
        ],
        out_shape=[
            jax.ShapeDtypeStruct((t, (KIN_QK_TILES + 1) * KIN_TN), BF16),
            jax.ShapeDtypeStruct((t, (v_spare + 2) * KIN_TN), BF16),
            jax.ShapeDtypeStruct((t, (u_spare + 1) * KIN_TN), F32),
        ],
        scratch_shapes=[pltpu.VMEM((2, tm, KIN_TN), F32)],
        compiler_params=_cparams(("arbitrary", "arbitrary")),
        name="kin",
    )(h, w, qg, kg)


ATT_R = 4
ATT_KW = ATT_R + NA_KH


def _attn_variant_tiles(rows, variant):
    if variant == 0:
        r0, ws = 0, 0
    elif variant == 1:
        r0 = ATT_R
        ws = r0 - NA_KH // 2
    else:
        r0, ws = rows - ATT_R, rows - ATT_KW
    tiles = []
    for a in range(ATT_R):
        r = r0 + a
        rs = min(max(r - NA_KH // 2, 0), rows - NA_KH)
        row = []
        for jp in range(ATT_KW // 2):
            kr = ws + 2 * jp
            valid_l = rs <= kr < rs + NA_KH
            valid_r = rs <= kr + 1 < rs + NA_KH
            d_left = kr - r + (NA_KH - 1)
            row.append((valid_l, valid_r, d_left))
        tiles.append(row)
    return tiles


def _attn_build_bias(rpb_ref, t2, bias, h, rows):
    n_rd = 2 * NA_KH - 1
    n_cd = 2 * NA_KW - 1
    cidx = lax.broadcasted_iota(jnp.int32, (GRID_W, LANES), 0)
    lane = lax.broadcasted_iota(jnp.int32, (GRID_W, LANES), 1)
    widx = lane & (GRID_W - 1)
    left = lane < GRID_W
    dcidx = jnp.clip(widx - cidx, -(NA_KW - 1), NA_KW - 1) + (NA_KW - 1)
    base = h * (n_rd * n_cd)
    for dp in range(n_rd + 1):
        acc = jnp.zeros((GRID_W, LANES), F32)
        for e in range(n_cd):
            vl = rpb_ref[base + (dp - 1) * n_cd + e] if dp - 1 >= 0 else 0.0
            vr = rpb_ref[base + dp * n_cd + e] if dp < n_rd else 0.0
            acc = jnp.where(dcidx == e, jnp.where(left, vl, vr), acc)
        t2[dp] = acc * LOG2E

    c_lo = jnp.clip(cidx - NA_KW // 2, 0, GRID_W - NA_KW)
    col_ok = (widx >= c_lo) & (widx < c_lo + NA_KW)
    for variant in range(3):
        tiles = _attn_variant_tiles(rows, variant)
        for a in range(ATT_R):
            for jp in range(ATT_KW // 2):
                valid_l, valid_r, d_left = tiles[a][jp]
                if not (valid_l or valid_r):
                    tile = jnp.full((GRID_W, LANES), NEG, F32)
                else:
                    ok = col_ok
                    if not valid_l:
                        ok = ok & (~left)
                    if not valid_r:
                        ok = ok & left
                    tile = jnp.where(ok, t2[d_left + 1], NEG)
                bias[variant, a * GRID_W:(a + 1) * GRID_W, jp * LANES:(jp + 1) * LANES] = tile


def _attn_kernel(rpb_ref, q_ref, k_ref, v_ref, o_ref, t2, bias, s_buf, p_buf, rden_buf, *, seq_len):
    rows = seq_len // GRID_W
    nb = rows // ATT_R

    @pl.when(pl.program_id(1) == 0)
    def _():
        _attn_build_bias(rpb_ref, t2, bias, pl.program_id(0), rows)

    def q_rows(rb):
        if isinstance(rb, int):
            return pl.ds(rb * (ATT_R * GRID_W), ATT_R * GRID_W)
        return pl.ds(pl.multiple_of(rb * (ATT_R * GRID_W), ATT_R * GRID_W), ATT_R * GRID_W)

    def k_rows(rb):
        if isinstance(rb, int):
            return pl.ds(min(max(rb * ATT_R - NA_KH // 2, 0), rows - ATT_KW) * GRID_W, ATT_KW * GRID_W)
        ws = jnp.clip(rb * ATT_R - NA_KH // 2, 0, rows - ATT_KW)
        return pl.ds(pl.multiple_of(ws * GRID_W, GRID_W), ATT_KW * GRID_W)

    def scores(rb, slot):
        if isinstance(rb, int):
            variant = 0 if rb == 0 else (2 if rb == nb - 1 else 1)
        else:
            variant = jnp.where(rb == 0, 0, jnp.where(rb == nb - 1, 2, 1))
        s = lax.dot_general(q_ref[q_rows(rb), :], k_ref[k_rows(rb), :], (((1,), (1,)), ((), ())),
                            preferred_element_type=F32)
        s_buf[slot] = s + bias[variant]

    def softmax(slot):
        s = s_buf[slot]
        m = jnp.max(s, axis=-1, keepdims=True)
        p = jnp.exp2(s - m)
        rden_buf[slot] = 1.0 / jnp.sum(p, axis=-1, keepdims=True)
        p_buf[slot] = p.astype(BF16)

    def values(rb, slot):
        o = jnp.dot(p_buf[slot], v_ref[k_rows(rb), :], preferred_element_type=F32)
        o_ref[q_rows(rb), :] = o * rden_buf[slot]

    for b in range(4):
        scores(b, b)
    softmax(0)
    softmax(1)

    def body(u, carry):
        t = 4 * u
        for half in (0, 2):
            values(t + half, half)
            values(t + half + 1, half + 1)
            softmax((half + 2) % 4)
            softmax((half + 3) % 4)
            scores(t + half + 4, half)
            scores(t + half + 5, half + 1)
        return carry

    lax.fori_loop(0, nb // 4 - 1, body, 0)
    values(nb - 4, 0)
    values(nb - 3, 1)
    softmax(2)
    softmax(3)
    values(nb - 2, 2)
    values(nb - 1, 3)


def _attn(qk, v, rpb_flat, bsz, seq_len):
    t = qk.shape[0]
    kernel = functools.partial(_attn_kernel, seq_len=seq_len)
    return pl.pallas_call(
        kernel,
        grid=(NA_HEADS, bsz),
        in_specs=[
            pl.BlockSpec(memory_space=pltpu.SMEM),
            pl.BlockSpec((seq_len, HEAD_DIM), lambda h, b: (b, h)),
            pl.BlockSpec((seq_len, HEAD_DIM), lambda h, b: (b, NA_HEADS + h)),
            pl.BlockSpec((seq_len, HEAD_DIM), lambda h, b: (b, h)),
        ],
        out_specs=pl.BlockSpec((seq_len, HEAD_DIM), lambda h, b: (b, h)),
        out_shape=jax.ShapeDtypeStruct((t, NA_WIDTH), F32),
        scratch_shapes=[
            pltpu.VMEM((2 * NA_KH, GRID_W, LANES), F32),
            pltpu.VMEM((3, ATT_R * GRID_W, ATT_KW * GRID_W), F32),
            pltpu.VMEM((4, ATT_R * GRID_W, ATT_KW * GRID_W), F32),
            pltpu.VMEM((4, ATT_R * GRID_W, ATT_KW * GRID_W), BF16),
            pltpu.VMEM((4, ATT_R * GRID_W, 1), F32),
        ],
        compiler_params=_cparams(("arbitrary", "arbitrary")),
        name="attn",
    )(rpb_flat, qk, qk, v)


DFT_TW = 8


def _kron_mat(m):
    eye = np.eye(DFT_TW, dtype=ml_dtypes.bfloat16)
    return jnp.asarray(np.kron(m.astype(ml_dtypes.bfloat16), eye))


BF16_ROWS = 16


def _dft1_kernel(m_ref, x_ref, twr_ref, twi_ref, o_ref):
    kd, rows, c = x_ref.shape
    mo = o_ref.shape[0]
    kh = mo // 2
    for g in range(rows // BF16_ROWS):
        re, im = [], []
        for s in range(g * BF16_ROWS // DFT_TW, (g + 1) * BF16_ROWS // DFT_TW):
            sl = slice(s * DFT_TW, (s + 1) * DFT_TW)
            x = x_ref[:, sl, :].reshape(kd * DFT_TW, c)
            y = jnp.dot(m_ref[...], x.astype(BF16), preferred_element_type=F32).reshape(mo, DFT_TW, c)
            twr = _lane_tile(twr_ref[:, sl, :], c)
            twi = _lane_tile(twi_ref[:, sl, :], c)
            yr = y[:kh]
            yi = y[kh:]
            re.append(yr * twr - yi * twi)
            im.append(yr * twi + yi * twr)
        gl = slice(g * BF16_ROWS, (g + 1) * BF16_ROWS)
        o_ref[:kh, gl, :] = jnp.concatenate(re, axis=1).astype(BF16)
        o_ref[kh:, gl, :] = jnp.concatenate(im, axis=1).astype(BF16)


def _dft1(mk, x, tw, c=None, col=0):
    bsz, kd, n2, width = x.shape
    c = width if c is None else c
    mo = mk.shape[0] // DFT_TW
    rows = 2 * BF16_ROWS if c <= FN_WIDTH else BF16_ROWS
    return pl.pallas_call(
        _dft1_kernel,
        grid=(bsz, n2 // rows),
        in_specs=[
            pl.BlockSpec(mk.shape, lambda b, j: (0, 0)),
            pl.BlockSpec((None, kd, rows, c), lambda b, j: (b, 0, j, col)),
            pl.BlockSpec((mo // 2, rows, LANES), lambda b, j: (0, j, 0)),
            pl.BlockSpec((mo // 2, rows, LANES), lambda b, j: (0, j, 0)),
        ],
        out_specs=pl.BlockSpec((None, mo, rows, c), lambda b, j: (b, 0, j, 0)),
        out_shape=jax.ShapeDtypeStruct((bsz, mo, n2, c), BF16),
        compiler_params=_cparams(("arbitrary", "arbitrary")),
        name="dft1",
    )(mk, x, *tw)


def _stage1_fwd_mat(n1, kd, scale=1.0, rows=None):
    c, s = _cos_sin(n1 if rows is None else rows, kd, n1)
    return _kron_mat(np.concatenate([c, -s], axis=0) * scale)


def _stage2_mats(n2):
    c, s = _cos_sin(n2, n2, n2)
    fwd = np.block([[c, s], [-s, c]])
    inv = np.block([[c, -s], [s, c]])
    return jnp.asarray(fwd.astype(ml_dtypes.bfloat16)), jnp.asarray(inv.astype(ml_dtypes.bfloat16))


def _half_rows(n1):
    return n1 // 2 + DFT_TW


def _twiddle(n1, n2, rows=None):
    rows = n1 if rows is None else rows
    ang = 2.0 * np.pi * np.outer(np.arange(rows), np.arange(n2)) / (n1 * n2)
    twr = jnp.asarray(np.cos(ang).astype(np.float32))
    twi = jnp.asarray((-np.sin(ang)).astype(np.float32))
    shape = (rows, n2, LANES)
    return jnp.broadcast_to(twr[:, :, None], shape), jnp.broadcast_to(twi[:, :, None], shape)


def _lane_tile(x, width):
    reps = width // x.shape[-1]
    return x if reps == 1 else jnp.concatenate([x] * reps, axis=-1)


FFT_KC = 8


def _fnw_kernel(cs_ref, w_ref, o_ref):
    o_ref[...] = jnp.dot(cs_ref[...], w_ref[...], precision=HIGHEST, preferred_element_type=F32)


def _fnw(fn_w):
    c, s = _cos_sin(FN_GROUP_DIM, FN_GROUP_DIM, FN_GROUP_DIM)
    cs = jnp.asarray(np.concatenate([c, s], axis=0).astype(np.float32))
    return pl.pallas_call(
        _fnw_kernel,
        grid=(FN_GROUPS,),
        in_specs=[
            pl.BlockSpec((2 * FN_GROUP_DIM, FN_GROUP_DIM), lambda g: (0, 0)),
            pl.BlockSpec((None, FN_GROUP_DIM, FN_GROUP_DIM), lambda g: (g, 0, 0)),
        ],
        out_specs=pl.BlockSpec((None, 2 * FN_GROUP_DIM, FN_GROUP_DIM), lambda g: (g, 0, 0)),
        out_shape=jax.ShapeDtypeStruct((FN_GROUPS, 2 * FN_GROUP_DIM, FN_GROUP_DIM), F32),
        compiler_params=_cparams(("arbitrary",)),
        name="fnw",
    )(cs, fn_w)


def _fn2_kernel(y_ref, f2_ref, g_ref, b_ref, o_ref):
    n2 = DFT_N2
    for t in range(FFT_KC):
        rhs = jnp.concatenate([y_ref[0, t], y_ref[1, t]], axis=0)
        z = jnp.dot(f2_ref[...], rhs, preferred_element_type=F32)
        zr = z[:n2]
        zi = z[n2:]
        outs = []
        for g in range(FN_GROUPS):
            sl = slice(g * FN_GROUP_DIM, (g + 1) * FN_GROUP_DIM)
            zc = jnp.concatenate([zr[:, sl], zi[:, sl]], axis=1).astype(BF16)
            outs.append(jnp.dot(zc, g_ref[g].astype(BF16), preferred_element_type=F32))
        o_ref[:, t, :] = jnp.concatenate(outs, axis=1) + b_ref[...]


def _fourier(u, gcat, fn_b, bsz, seq_len):
    n2 = DFT_N2
    n1 = seq_len // n2
    m1 = _stage1_fwd_mat(n1, n1, scale=1.0 / math.sqrt(seq_len * FN_GROUP_DIM))
    y = _dft1(m1, u.reshape(bsz, n1, n2, u.shape[-1]), _twiddle(n1, n2), c=FN_WIDTH, col=KIN_U_FN_COL)
    y = y.reshape(bsz, 2, n1, n2, FN_WIDTH)
    f2, _ = _stage2_mats(n2)
    out = pl.pallas_call(
        _fn2_kernel,
        grid=(n1 // FFT_KC, bsz),
        in_specs=[
            pl.BlockSpec((None, 2, FFT_KC, n2, FN_WIDTH), lambda k, b: (b, 0, k, 0, 0)),
            pl.BlockSpec(f2.shape, lambda k, b: (0, 0)),
            pl.BlockSpec((FN_GROUPS, 2 * FN_GROUP_DIM, FN_GROUP_DIM), lambda k, b: (0, 0, 0)),
            pl.BlockSpec((1, FN_WIDTH), lambda k, b: (0, 0)),
        ],
        out_specs=pl.BlockSpec((None, n2, FFT_KC, FN_WIDTH), lambda k, b: (b, 0, k, 0)),
        out_shape=jax.ShapeDtypeStruct((bsz, n2, n1, FN_WIDTH), F32),
        compiler_params=_cparams(("arbitrary", "arbitrary")),
        name="fn2",
    )(y, f2, gcat, fn_b)
    return out.reshape(bsz * seq_len, FN_WIDTH)


HPRE_TM = 512


def _hpre_kernel(up_ref, um_ref, un_ref, cw_ref, cb_ref, z_ref, x0_ref, *, seq_len):
    tm = HPRE_TM
    i = pl.program_id(0)
    at_start = (i * tm) % seq_len == 0
    at_end = ((i + 1) * tm) % seq_len == 0
    u = um_ref[...]
    r = lax.broadcasted_iota(jnp.int32, (tm, 1), 0)
    prev_row = jnp.where(at_start, 0.0, up_ref[7:8, :])
    next_row = jnp.where(at_end, 0.0, un_ref[0:1, :])
    upv = jnp.where(r == 0, prev_row, pltpu.roll(u, 1, 0))
    dnv = jnp.where(r == tm - 1, next_row, pltpu.roll(u, tm - 1, 0))
    y = upv * cw_ref[0:1, :] + u * cw_ref[1:2, :] + dnv * cw_ref[2:3, :] + cb_ref[...]
    z_ref[...] = y[:, 2 * HY_WIDTH:] * y[:, HY_WIDTH:2 * HY_WIDTH]
    x0_ref[...] = y[:, :HY_WIDTH]


def _hpre(u, cw, cb, seq_len):
    t = u.shape[0]
    tm = HPRE_TM
    w = 3 * HY_WIDTH
    nb8 = t // 8
    return pl.pallas_call(
        functools.partial(_hpre_kernel, seq_len=seq_len),
        grid=(t // tm,),
        in_specs=[
            pl.BlockSpec((8, w), lambda i: (jnp.maximum(i * (tm // 8) - 1, 0), 0)),
            pl.BlockSpec((tm, w), lambda i: (i, 0)),
            pl.BlockSpec((8, w), lambda i: (jnp.minimum((i + 1) * (tm // 8), nb8 - 1), 0)),
            pl.BlockSpec((3, w), lambda i: (0, 0)),
            pl.BlockSpec((1, w), lambda i: (0, 0)),
        ],
        out_specs=[
            pl.BlockSpec((tm, HY_WIDTH), lambda i: (i, 0)),
            pl.BlockSpec((tm, HY_WIDTH), lambda i: (i, 0)),
        ],
        out_shape=[jax.ShapeDtypeStruct((t, HY_WIDTH), F32), jax.ShapeDtypeStruct((t, HY_WIDTH), F32)],
        compiler_params=_cparams(("arbitrary",)),
        name="hpre",
    )(u, u, u, cw, cb)


HFILT_TL = 512
HY_EMB_PAD = 128


def _hfilt_kernel(zt_ref, t_ref, w1t_ref, b1_ref, f1_ref, w2t_ref, b2_ref, f2_ref, w3_ref, b3_ref, dl_ref, o_ref):
    h = jnp.dot(w1t_ref[...], zt_ref[...], precision=HIGHEST, preferred_element_type=F32)
    h = jnp.sin(f1_ref[...] * (h + b1_ref[...]))
    h = jnp.dot(w2t_ref[...], h, precision=HIGHEST, preferred_element_type=F32)
    h = jnp.sin(f2_ref[...] * (h + b2_ref[...]))
    h = lax.dot_general(h, w3_ref[...], (((0,), (0,)), ((), ())), precision=HIGHEST,
                        preferred_element_type=F32) + b3_ref[...]
    h = h * jnp.exp(-t_ref[...] * dl_ref[...])
    row = lax.broadcasted_iota(jnp.int32, h.shape, 0) + pl.program_id(0) * HFILT_TL
    col = lax.broadcasted_iota(jnp.int32, h.shape, 1)
    o_ref[...] = jnp.where((row == 0) & (col >= HY_WIDTH), 0.0, h)


def _hyena_emb_np(seq_len):
    t = np.linspace(0.0, 1.0, seq_len)[:, None]
    bands = (HY_EMB_DIM - 1) // 2
    w = 2.0 * np.pi * np.arange(seq_len)[:, None] / seq_len
    fr = np.linspace(1e-4, bands - 1, bands)[None, :]
    z = np.concatenate([t, np.cos(fr * w), -np.sin(fr * w)], axis=-1)
    out = np.zeros((seq_len, HY_EMB_PAD), np.float32)
    out[:, :HY_EMB_DIM] = z
    return out


def _hfilt(seq_len, w1, b1, f1, w2, b2, f2, w3, b3):
    emb = _hyena_emb_np(seq_len)
    zemb_t = jnp.asarray(np.ascontiguousarray(emb.T))
    t_col = jnp.asarray(emb[:, 0:1])
    max_decay = math.log(HY_TARGET) / HY_FAST_DECAY
    min_decay = math.log(HY_TARGET) / HY_SLOW_DECAY
    deltas = np.abs(np.linspace(min_decay, max_decay, HY_WIDTH))
    dl = jnp.asarray(np.tile(deltas, 2)[None, :].astype(np.float32))
    w1t = jnp.pad(w1, ((0, HY_EMB_PAD - HY_EMB_DIM), (0, 0))).T
    tl = HFILT_TL
    full = lambda shape: pl.BlockSpec(shape, lambda i: (0,) * len(shape))
    return pl.pallas_call(
        _hfilt_kernel,
        grid=(seq_len // tl,),
        in_specs=[
            pl.BlockSpec((HY_EMB_PAD, tl), lambda i: (0, i)),
            pl.BlockSpec((tl, 1), lambda i: (i, 0)),
            full((HY_HIDDEN, HY_EMB_PAD)), full((HY_HIDDEN, 1)), full((HY_HIDDEN, 1)),
            full((HY_HIDDEN, HY_HIDDEN)), full((HY_HIDDEN, 1)), full((HY_HIDDEN, 1)),
            full((HY_HIDDEN, 2 * HY_WIDTH)), full((1, 2 * HY_WIDTH)), full((1, 2 * HY_WIDTH)),
        ],
        out_specs=pl.BlockSpec((tl, 2 * HY_WIDTH), lambda i: (i, 0)),
        out_shape=jax.ShapeDtypeStruct((seq_len, 2 * HY_WIDTH), F32),
        compiler_params=_cparams(("arbitrary",)),
        name="hfilt",
    )(zemb_t, t_col, w1t, b1.T, f1.T, w2.T, b2.T, f2.T, w3, b3, dl)


def _hk2_kernel(y_ref, f2_ref, o_ref):
    n2 = DFT_N2
    for t in range(FFT_KC):
        rhs = jnp.concatenate([y_ref[0, t], y_ref[1, t]], axis=0)
        z = jnp.dot(f2_ref[...], rhs, preferred_element_type=F32)
        zr = z[:n2]
        zi = z[n2:]
        o_ref[0, t] = zr[:, :HY_WIDTH] + zr[:, HY_WIDTH:]
        o_ref[1, t] = zi[:, :HY_WIDTH] - zi[:, HY_WIDTH:]


def _hy2_kernel(y_ref, kf_ref, twr_ref, twi_ref, f2_ref, f2i_ref, o_ref):
    n2 = DFT_N2
    for t in range(FFT_KC):
        rhs = jnp.concatenate([y_ref[0, t], y_ref[1, t]], axis=0)
        z = jnp.dot(f2_ref[...], rhs, preferred_element_type=F32)
        zr = z[:n2]
        zi = z[n2:]
        kr = kf_ref[0, t]
        ki = kf_ref[1, t]
        pr = zr * kr - zi * ki
        pi = zr * ki + zi * kr
        rhs = jnp.concatenate([pr, pi], axis=0).astype(BF16)
        g = jnp.dot(f2i_ref[...], rhs, preferred_element_type=F32)
        gr = g[:n2]
        gi = g[n2:]
        twr = _lane_tile(twr_ref[t], HY_WIDTH)
        twi = _lane_tile(twi_ref[t], HY_WIDTH)
        o_ref[0, t] = (gr * twr + gi * twi).astype(BF16)
        o_ref[1, t] = (gi * twr - gr * twi).astype(BF16)


def _hy3_kernel(m_ref, g_ref, z_ref, x0_ref, d_ref, o_ref):
    kd, rows, c = g_ref.shape
    mo = o_ref.shape[0]
    g = g_ref[...].astype(F32)
    for s in range(rows // DFT_TW):
        sl = slice(s * DFT_TW, (s + 1) * DFT_TW)
        rhs = g[:, sl, :].reshape(kd * DFT_TW, c).astype(BF16)
        conv = jnp.dot(m_ref[...], rhs, preferred_element_type=F32).reshape(mo, DFT_TW, c)
        o_ref[:, sl, :] = (conv + z_ref[:, sl, :] * d_ref[...]) * x0_ref[:, sl, :]


def _hyena_filter_spectrum(seq_len, w1, b1, f1, w2, b2, f2, w3, b3):
    n = 2 * seq_len
    n2 = DFT_N2
    n1 = n // n2
    kh = _half_rows(n1)
    h = _hfilt(seq_len, w1, b1, f1, w2, b2, f2, w3, b3)
    m1 = _stage1_fwd_mat(n1, n1 // 2, rows=kh)
    y = _dft1(m1, h.reshape(1, n1 // 2, n2, 2 * HY_WIDTH), _twiddle(n1, n2, kh))
    y = y.reshape(2, kh, n2, 2 * HY_WIDTH)
    f2m, _ = _stage2_mats(n2)
    return pl.pallas_call(
        _hk2_kernel,
        grid=(kh // FFT_KC,),
        in_specs=[
            pl.BlockSpec((2, FFT_KC, n2, 2 * HY_WIDTH), lambda k: (0, k, 0, 0)),
            pl.BlockSpec(f2m.shape, lambda k: (0, 0)),
        ],
        out_specs=pl.BlockSpec((2, FFT_KC, n2, HY_WIDTH), lambda k: (0, k, 0, 0)),
        out_shape=jax.ShapeDtypeStruct((2, kh, n2, HY_WIDTH), F32),
        compiler_params=_cparams(("arbitrary",)),
        name="hk2",
    )(y, f2m)


def _hyena_conv(z, x0, kf, d_skip, bsz, seq_len):
    n = 2 * seq_len
    n2 = DFT_N2
    n1 = n // n2
    kh = _half_rows(n1)
    zv = z.reshape(bsz, n1 // 2, n2, HY_WIDTH)
    tw = _twiddle(n1, n2, kh)
    y = _dft1(_stage1_fwd_mat(n1, n1 // 2, rows=kh), zv, tw).reshape(bsz, 2, kh, n2, HY_WIDTH)
    f2m, f2i = _stage2_mats(n2)
    g = pl.pallas_call(
        _hy2_kernel,
        grid=(kh // FFT_KC, bsz),
        in_specs=[
            pl.BlockSpec((None, 2, FFT_KC, n2, HY_WIDTH), lambda k, b: (b, 0, k, 0, 0)),
            pl.BlockSpec((2, FFT_KC, n2, HY_WIDTH), lambda k, b: (0, k, 0, 0)),
            pl.BlockSpec((FFT_KC, n2, LANES), lambda k, b: (k, 0, 0)),
            pl.BlockSpec((FFT_KC, n2, LANES), lambda k, b: (k, 0, 0)),
            pl.BlockSpec(f2m.shape, lambda k, b: (0, 0)),
            pl.BlockSpec(f2i.shape, lambda k, b: (0, 0)),
        ],
        out_specs=pl.BlockSpec((None, 2, FFT_KC, n2, HY_WIDTH), lambda k, b: (b, 0, k, 0, 0)),
        out_shape=jax.ShapeDtypeStruct((bsz, 2, kh, n2, HY_WIDTH), BF16),
        compiler_params=_cparams(("arbitrary", "arbitrary")),
        name="hy2",
    )(y, kf, tw[0], tw[1], f2m, f2i)
    c, s = _cos_sin(n1 // 2, kh, n1)
    wt = np.where(np.arange(kh) < n1 // 2, 2.0, 0.0)
    wt[0] = 1.0
    wt[n1 // 2] = 1.0
    m3 = _kron_mat(np.concatenate([c * wt, -s * wt], axis=1) / n)
    rows = BF16_ROWS
    half =pl.BlockSpec((None, n1 // 2, rows, HY_WIDTH), lambda b, j: (b, 0, j, 0))
    out = pl.pallas_call(
        _hy3_kernel,
        grid=(bsz, n2 // rows),
        in_specs=[
            pl.BlockSpec(m3.shape, lambda b, j: (0, 0)),
            pl.BlockSpec((None, 2 * kh, rows, HY_WIDTH), lambda b, j: (b, 0, j, 0)),
            half,
            half,
            pl.BlockSpec((1, HY_WIDTH), lambda b, j: (0, 0)),
        ],
        out_specs=half,
        out_shape=jax.ShapeDtypeStruct((bsz, n1 // 2, n2, HY_WIDTH), F32),
        compiler_params=_cparams(("arbitrary", "arbitrary")),
        name="hy3",
    )(m3, g.reshape(bsz, 2 * kh, n2, HY_WIDTH), zv, x0.reshape(bsz, n1 // 2, n2, HY_WIDTH), d_skip)
    return out.reshape(bsz * seq_len, HY_WIDTH)


KOUT_TM = 512
KOUT_SUB = 256


def _rms(y):
    return y * lax.rsqrt(jnp.mean(y * y, axis=-1, keepdims=True) + EPS)


def _kout_kernel(ya_ref, yb_ref, yc_ref, x_ref, mod_ref, og_ref, n2g_ref, w_ref, xo_ref, h2_ref):
    for s in range(KOUT_TM // KOUT_SUB):
        sl = slice(s * KOUT_SUB, (s + 1) * KOUT_SUB)
        n = jnp.concatenate([_rms(ya_ref[sl, :]), _rms(yb_ref[sl, :]), _rms(yc_ref[sl, :])], axis=-1) * og_ref[...]
        y = jnp.dot(n.astype(BF16), w_ref[...], preferred_element_type=F32)
        xn = x_ref[sl, :] + mod_ref[2:3, :] * y
        xo_ref[sl, :] = xn
        h2 = _rms(xn) * n2g_ref[...]
        h2_ref[sl, :] = (h2 * (1.0 + mod_ref[4:5, :]) + mod_ref[3:4, :]).astype(BF16)


def _kout(ya, yb, yc, x, mod, og, n2g, w, l, seq_len):
    t = x.shape[0]
    tm = KOUT_TM
    row = lambda width: pl.BlockSpec((tm, width), lambda i: (i, 0))
    return pl.pallas_call(
        _kout_kernel,
        grid=(t // tm,),
        in_specs=[
            row(NA_WIDTH), row(FN_WIDTH), row(HY_WIDTH), row(D_MODEL),
            pl.BlockSpec((None, N_MOD, D_MODEL), lambda i: ((i * tm) // seq_len, 0, 0)),
            pl.BlockSpec((1, D_MODEL), lambda i: (0, 0)),
            pl.BlockSpec((1, D_MODEL), lambda i: (0, 0)),
            pl.BlockSpec((None, D_MODEL, D_MODEL), lambda i: (l, 0, 0)),
        ],
        out_specs=[row(D_MODEL), row(D_MODEL)],
        out_shape=[jax.ShapeDtypeStruct((t, D_MODEL), F32), jax.ShapeDtypeStruct((t, D_MODEL), BF16)],
        compiler_params=_cparams(("arbitrary",)),
        name="kout",
    )(ya, yb, yc, x, mod, og, n2g, w)


MLP_TM = 512
MLP_TF = 512
MLP_HALO = 16


def _gelu_exact(a):
    return 0.5 * a * (1.0 + lax.erf(a * (1.0 / math.sqrt(2.0))))


def _mlp_kernel(hp_ref, hm_ref, hn_ref, x_ref, mod_ref, wa_ref, wg_ref, cwb_ref, wd_ref, *rest,
                seq_len, emit_next):
    if emit_next:
        modn_ref, gn_ref, o_ref, hnext_ref, hext, acc = rest
    else:
        o_ref, hext, acc = rest
    tm = MLP_TM
    halo = MLP_HALO
    i = pl.program_id(0)
    j = pl.program_id(1)

    last = pl.num_programs(1) - 1

    def chunk():
        a_ext = jnp.dot(hext[...], wa_ref[...], preferred_element_type=F32)
        mid = a_ext[0:tm]
        r = lax.broadcasted_iota(jnp.int32, (tm, 1), 0)
        up = jnp.where(r == 0, a_ext[tm:tm + 1], pltpu.roll(mid, 1, 0))
        dn = jnp.where(r == tm - 1, a_ext[tm + 1:tm + 2], pltpu.roll(mid, tm - 1, 0))
        a = up * cwb_ref[0:1, :] + mid * cwb_ref[1:2, :] + dn * cwb_ref[2:3, :] + cwb_ref[3:4, :]
        gate = jnp.dot(hext[0:tm, :], wg_ref[...], preferred_element_type=F32)
        act = (_gelu_exact(a) * gate).astype(BF16)
        return jnp.dot(act, wd_ref[...], preferred_element_type=F32)

    @pl.when(j == 0)
    def _():
        at_start = (i * tm) % seq_len == 0
        at_end = ((i + 1) * tm) % seq_len == 0
        hext[0:tm, :] = hm_ref[...]
        r16 = lax.broadcasted_iota(jnp.int32, (halo, 1), 0)
        prev = pltpu.roll(hp_ref[...].astype(F32), 1, 0)
        nxt = pltpu.roll(hn_ref[...].astype(F32), 1, 0)
        extra = jnp.where((r16 == 0) & jnp.logical_not(at_start), prev,
                          jnp.where((r16 == 1) & jnp.logical_not(at_end), nxt, 0.0))
        hext[tm:, :] = extra.astype(BF16)
        acc[...] = chunk()

    @pl.when((j > 0) & (j < last))
    def _():
        acc[...] += chunk()

    @pl.when(j == last)
    def _():
        xo = x_ref[...] + mod_ref[5:6, :] * (acc[...] + chunk())
        o_ref[...] = xo
        if emit_next:
            hnext_ref[...] = _modnorm(xo, gn_ref[...], modn_ref[0:1, :], modn_ref[1:2, :]).astype(BF16)


def _mlp(h2, x, mod, w_up, cwb, w_down, l, seq_len, next_norm=None):
    t = x.shape[0]
    tm, tf, halo = MLP_TM, MLP_TF, MLP_HALO
    nf = D_FF // tf
    nbh = t // halo
    emit_next = next_norm is not None
    row = pl.BlockSpec((tm, D_MODEL), lambda i, j: (i, 0))
    modspec = pl.BlockSpec((None, N_MOD, D_MODEL), lambda i, j: ((i * tm) // seq_len, 0, 0))
    in_specs = [
        pl.BlockSpec((halo, D_MODEL), lambda i, j: (jnp.maximum(i * (tm // halo) - 1, 0), 0)),
        row,
        pl.BlockSpec((halo, D_MODEL), lambda i, j: (jnp.minimum((i + 1) * (tm // halo), nbh - 1), 0)),
        row,
        modspec,
        pl.BlockSpec((None, D_MODEL, tf), lambda i, j: (l, 0, j)),
        pl.BlockSpec((None, D_MODEL, tf), lambda i, j: (l, 0, nf + j)),
        pl.BlockSpec((SHORT_CONV + 1, tf), lambda i, j: (0, j)),
        pl.BlockSpec((None, tf, D_MODEL), lambda i, j: (l, j, 0)),
    ]
    args = [h2, h2, h2, x, mod, w_up, w_up, cwb, w_down]
    out_specs = [row]
    out_shape = [jax.ShapeDtypeStruct((t, D_MODEL), F32)]
    if emit_next:
        in_specs += [modspec, pl.BlockSpec((1, D_MODEL), lambda i, j: (0, 0))]
        args += list(next_norm)
        out_specs.append(row)
        out_shape.append(jax.ShapeDtypeStruct((t, D_MODEL), BF16))
    return pl.pallas_call(
        functools.partial(_mlp_kernel, seq_len=seq_len, emit_next=emit_next),
        grid=(t // tm, nf),
        in_specs=in_specs,
        out_specs=out_specs,
        out_shape=out_shape,
        scratch_shapes=[pltpu.VMEM((tm + halo, D_MODEL), BF16), pltpu.VMEM((tm, D_MODEL), F32)],
        compiler_params=_cparams(("arbitrary", "arbitrary")),
        name="mlp",
    )(*args)


def _layer(x, h, mod, p, l, bsz, seq_len, kf, next_norm):
    qk, v, u = _kin(h, p["w_in"], l, p["q_norm_g"][l][None], p["k_norm_g"][l][None])
    ya = _attn(qk, v, p["na_rpb"][l].reshape(-1), bsz, seq_len)
    yb = _fourier(u, p["gcat"][l], p["fn_b"][l][None], bsz, seq_len)
    z, x0 = _hpre(u, p["hy_conv_w"][l], p["hy_conv_b"][l][None], seq_len)
    yc = _hyena_conv(z, x0, kf, p["hy_d"][l][None], bsz, seq_len)
    x1, h2 = _kout(ya, yb, yc, x, mod, p["out_norm_g"][l][None], p["norm2_g"][l][None], p["w_out"], l, seq_len)
    out = _mlp(h2, x1, mod, p["mlp_w_up"], jnp.concatenate([p["mlp_conv_w"][l], p["mlp_conv_b"][l][None]], axis=0), p["mlp_w_down"], l,
               seq_len, next_norm)
    return (out[0], out[1]) if next_norm is not None else (out[0], None)


def kernel(x_prompt, x_sample, c_prompt, c_sample, ada_w, ada_b, norm1_g, w_in, q_norm_g, k_norm_g, na_rpb, fn_w, fn_b, hy_conv_w, hy_conv_b, hy_w1, hy_b1, hy_f1, hy_w2, hy_b2, hy_f2, hy_w3, hy_b3, hy_d, out_norm_g, w_out, norm2_g, mlp_w_up, mlp_conv_w, mlp_conv_b, mlp_w_down):
    depth = ada_w.shape[0]
    groups = [(x_prompt, c_prompt), (x_sample, c_sample)]
    nseq = sum(c.shape[0] for _, c in groups)
    nrow = -(-nseq // 16) * 16
    c_all = jnp.concatenate([c for _, c in groups] + [jnp.zeros((nrow - nseq, D_MODEL), F32)], axis=0)
    mod_all = _ada(c_all, ada_w, ada_b[:, None, :]).reshape(depth, nrow, N_MOD, D_MODEL)

    p = dict(
        w_in=w_in.astype(BF16), q_norm_g=q_norm_g, k_norm_g=k_norm_g, na_rpb=na_rpb,
        fn_b=fn_b, hy_conv_w=hy_conv_w, hy_conv_b=hy_conv_b, hy_d=hy_d, out_norm_g=out_norm_g,
        w_out=w_out.astype(BF16), norm2_g=norm2_g, mlp_w_up=mlp_w_up.astype(BF16), mlp_conv_w=mlp_conv_w,
        mlp_conv_b=mlp_conv_b, mlp_w_down=mlp_w_down.astype(BF16),
        gcat=[_fnw(fn_w[l]) for l in range(depth)],
    )

    outs = []
    seq_off = 0
    kf_cache = {}
    for x, c in groups:
        bsz, seq_len, _ = x.shape
        xt = x.reshape(bsz * seq_len, D_MODEL)
        mods = [mod_all[l, seq_off:seq_off + bsz] for l in range(depth)]
        ht = _knorm(xt, mods[0], norm1_g[0][None], seq_len)
        for l in range(depth):
            if (l, seq_len) not in kf_cache:
                kf_cache[(l, seq_len)] = _hyena_filter_spectrum(
                    seq_len, hy_w1[l], hy_b1[l][None], hy_f1[l][None], hy_w2[l], hy_b2[l][None], hy_f2[l][None],
                    hy_w3[l], hy_b3[l][None])
            next_norm = (mods[l + 1], norm1_g[l + 1][None]) if l + 1 < depth else None
            xt, ht = _layer(xt, ht, mods[l], p, l, bsz, seq_len, kf_cache[(l, seq_len)], next_norm)
        outs.append(xt.reshape(bsz, seq_len, D_MODEL))
        seq_off += bsz
    return tuple(outs)
```
